```python
import math
import jax, jax.numpy as jnp
from jax import lax
import numpy as np

D_MODEL = 1024
BATCH = 8
SEQ = 2048
DEPTH = 1

MIX_WIDTH = D_MODEL
M_HEADS = 4
M_HEAD_DIM = (MIX_WIDTH // 2) // M_HEADS
M_WIDTH = M_HEADS * M_HEAD_DIM
CONV_WIDTH = 4
CHUNK = 64
D_HEADS = 4
D_HEAD_DIM = (MIX_WIDTH // 2) // (2 * D_HEADS)
D_WIDTH = D_HEADS * 2 * D_HEAD_DIM
Q_BLOCK = 128
P_HEADS = 8
N_KEYS = 128
N_EXPERTS = N_KEYS * N_KEYS
P_TOPK = 16
P_QUERY_DIM = 256
P_HALF = P_QUERY_DIM // 2
TOKEN_BLOCK = 128

EPS = 1e-6
NEG = -1e30

kernel_name = "hybrid_mlstm_diffattn_peer_block"


def rmsnorm(x, g):
    xf = x.astype(jnp.float32)
    y = xf * lax.rsqrt(jnp.mean(xf * xf, axis=-1, keepdims=True) + EPS)
    return (y * g.astype(jnp.float32)).astype(x.dtype)


def causal_dwconv(u, w, b):
    C = u.shape[-1]
    y = lax.conv_general_dilated(
        u, w[:, None, :].astype(u.dtype), window_strides=(1,),
        padding=[(CONV_WIDTH - 1, 0)],
        dimension_numbers=("NWC", "WIO", "NWC"),
        feature_group_count=C)
    return y + b.astype(u.dtype)


def mlstm_chunkwise(q, k, v, ig, lf):
    Bb, H, S, d = q.shape
    nc = S // CHUNK

    def to_chunks(a):
        a = a.reshape(a.shape[:2] + (nc, CHUNK) + a.shape[3:])
        return jnp.moveaxis(a, 2, 0)

    qc, kc, vc, ic, fc = (to_chunks(a) for a in (q, k, v, ig, lf))
    causal = jnp.tril(jnp.ones((CHUNK, CHUNK), dtype=bool))

    def step(carry, inp):
        C, n, m = carry
        qb, kb, vb, ib, fb = inp
        b = jnp.cumsum(fb, axis=-1)
        logD = jnp.where(causal, b[..., :, None] - b[..., None, :] + ib[..., None, :], -jnp.inf)
        m_t = jnp.maximum(b + m[..., None], jnp.max(logD, axis=-1))
        Dw = jnp.exp(logD - m_t[..., None])
        inter = jnp.exp(b + m[..., None] - m_t)
        sqk = jnp.einsum('bhtd,bhsd->bhts', qb, kb) * Dw
        num = (jnp.einsum('bhts,bhsv->bhtv', sqk, vb)
               + inter[..., None] * jnp.einsum('bhtd,bhdv->bhtv', qb, C))
        den = jnp.sum(sqk, axis=-1) + inter * jnp.einsum('bhtd,bhd->bht', qb, n)
        h = num / jnp.maximum(jnp.abs(den), jnp.exp(-m_t))[..., None]
        bL = b[..., -1]
        logw = bL[..., None] - b + ib
        m_new = jnp.maximum(bL + m, jnp.max(logw, axis=-1))
        w = jnp.exp(logw - m_new[..., None])
        decay = jnp.exp(bL + m - m_new)
        C_new = decay[..., None, None] * C + jnp.einsum('bhs,bhsd,bhsv->bhdv', w, kb, vb)
        n_new = decay[..., None] * n + jnp.einsum('bhs,bhsd->bhd', w, kb)
        return (C_new, n_new, m_new), h

    init = (jnp.zeros((Bb, H, d, d), jnp.float32),
            jnp.zeros((Bb, H, d), jnp.float32),
            jnp.zeros((Bb, H), jnp.float32))
    _, hs = lax.scan(step, init, (qc, kc, vc, ic, fc))
    return jnp.moveaxis(hs, 0, 2).reshape(Bb, H, S, d)


def diff_attention(q, k, v, lam):
    Bb, H, _, S, dh = q.shape
    nb = S // Q_BLOCK
    kpos = jnp.arange(S)
    scale = D_HEAD_DIM ** -0.5

    def block(i):
        qb = lax.dynamic_slice_in_dim(q, i * Q_BLOCK, Q_BLOCK, axis=3)
        s = jnp.einsum('bhptd,bhpsd->bhpts', qb, k) * scale
        qpos = i * Q_BLOCK + jnp.arange(Q_BLOCK)
        mask = kpos[None, :] <= qpos[:, None]
        a = jax.nn.softmax(jnp.where(mask, s, NEG), axis=-1)
        att = a[:, :, 0] - lam * a[:, :, 1]
        return jnp.einsum('bhts,bhsv->bhtv', att, v)

    o = lax.map(block, jnp.arange(nb))
    return jnp.moveaxis(o, 0, 2).reshape(Bb, H, S, v.shape[-1])


def peer(h, w_query, sub_keys, u_tab, v_tab):
    Bb, S, D = h.shape
    xt = h.reshape(-1, TOKEN_BLOCK, D)

    def block(xb):
        q = (xb @ w_query).reshape(TOKEN_BLOCK, P_HEADS, 2, P_HALF)
        s = jnp.einsum('thpk,hpnk->thpn', q, sub_keys).astype(jnp.float32)
        sv, si = lax.top_k(s, P_TOPK)
        cand = (sv[:, :, 0, :, None] + sv[:, :, 1, None, :]).reshape(TOKEN_BLOCK, P_HEADS, P_TOPK * P_TOPK)
        cidx = (si[:, :, 0, :, None] * N_KEYS + si[:, :, 1, None, :]).reshape(TOKEN_BLOCK, P_HEADS, P_TOPK * P_TOPK)
        fv, fpos = lax.top_k(cand, P_TOPK)
        eidx = jnp.take_along_axis(cidx, fpos, axis=-1)
        g = jax.nn.softmax(fv, axis=-1)
        u = u_tab[eidx]
        act = jax.nn.gelu(jnp.einsum('td,thkd->thk', xb, u).astype(jnp.float32), approximate=False)
        return jnp.einsum('thk,thkd->td', (g * act).astype(xb.dtype), v_tab[eidx])

    return lax.map(block, xt).reshape(Bb, S, D)


def setup_inputs(seed: int = 0) -> dict:
    key = jax.random.key(seed)
    ks = jax.random.split(key, 24)
    L, D = DEPTH, D_MODEL
    n_in = 4 * M_WIDTH + 2 * M_HEADS + 3 * D_WIDTH
    nrm = lambda k, shape, s: jax.random.normal(k, shape, jnp.float32) * s
    f_bias = jnp.linspace(3.0, 6.0, M_HEADS, dtype=jnp.float32)
    gate_b = jnp.concatenate([
        nrm(ks[7], (L, M_HEADS), 0.1),
        f_bias[None, :] + nrm(ks[8], (L, M_HEADS), 0.1)], axis=-1)
    return {
        "x": nrm(ks[0], (BATCH, SEQ, D), 1.0),
        "c": nrm(ks[1], (BATCH, D), 1.0),
        "ada_w": nrm(ks[2], (L, D, 6 * D), 0.5 * D ** -0.5),
        "ada_b": nrm(ks[3], (L, 6 * D), 0.02),
        "norm1_g": 1.0 + nrm(ks[4], (L, D), 0.02),
        "w_in": nrm(ks[5], (L, D, n_in), D ** -0.5),
        "conv_w": nrm(ks[6], (L, CONV_WIDTH, 2 * M_WIDTH), CONV_WIDTH ** -0.5),
        "conv_b": nrm(ks[9], (L, 2 * M_WIDTH), 0.02),
        "mlstm_gate_b": gate_b,
        "mlstm_norm_g": 1.0 + nrm(ks[10], (L, M_WIDTH), 0.02),
        "lambda_q1": nrm(ks[11], (L, D_HEAD_DIM), 0.1),
        "lambda_k1": nrm(ks[12], (L, D_HEAD_DIM), 0.1),
        "lambda_q2": nrm(ks[13], (L, D_HEAD_DIM), 0.1),
        "lambda_k2": nrm(ks[14], (L, D_HEAD_DIM), 0.1),
        "diff_norm_g": 1.0 + nrm(ks[15], (L, 2 * D_HEAD_DIM), 0.02),
        "w_out": nrm(ks[16], (L, MIX_WIDTH, D), MIX_WIDTH ** -0.5),
        "norm2_g": 1.0 + nrm(ks[17], (L, D), 0.02),
        "peer_w_query": nrm(ks[18], (L, D, P_HEADS * P_QUERY_DIM), D ** -0.5),
        "peer_sub_keys": nrm(ks[19], (L, P_HEADS, 2, N_KEYS, P_HALF), P_HALF ** -0.5),
        "peer_u": nrm(ks[20], (L, N_EXPERTS, D), D ** -0.5),
        "peer_v": nrm(ks[21], (L, N_EXPERTS, D), 0.1),
        "final_g": 1.0 + nrm(ks[22], (D,), 0.02),
    }


def reference(x, c, ada_w, ada_b, norm1_g, w_in, conv_w, conv_b, mlstm_gate_b,
              mlstm_norm_g, lambda_q1, lambda_k1, lambda_q2, lambda_k2, diff_norm_g,
              w_out, norm2_g, peer_w_query, peer_sub_keys, peer_u, peer_v, final_g):
    Bb, S, D = x.shape
    split_pts = [M_WIDTH, 2 * M_WIDTH, 3 * M_WIDTH, 4 * M_WIDTH,
                 4 * M_WIDTH + M_HEADS, 4 * M_WIDTH + 2 * M_HEADS,
                 4 * M_WIDTH + 2 * M_HEADS + D_WIDTH,
                 4 * M_WIDTH + 2 * M_HEADS + 2 * D_WIDTH]
    for l in range(DEPTH):
        mod = jax.nn.silu(c) @ ada_w[l] + ada_b[l]
        sh1, sc1, g1, sh2, sc2, g2 = jnp.split(mod[:, None, :], 6, axis=-1)

        h = rmsnorm(x, norm1_g[l]) * (1.0 + sc1) + sh1
        p = h @ w_in[l]
        mq, mk, mv, mo, mi, mf, dq, dk, dv = jnp.split(p, split_pts, axis=-1)

        qk = jax.nn.silu(causal_dwconv(jnp.concatenate([mq, mk], axis=-1), conv_w[l], conv_b[l]))
        mq, mk = jnp.split(qk, 2, axis=-1)
        heads = lambda a: a.reshape(Bb, S, M_HEADS, M_HEAD_DIM).transpose(0, 2, 1, 3).astype(jnp.float32)
        q_m = heads(mq)
        k_m = heads(mk) * (M_HEAD_DIM ** -0.5)
        v_m = heads(mv)
        gb = mlstm_gate_b[l].astype(jnp.float32)
        ig = (mi.astype(jnp.float32) + gb[:M_HEADS]).transpose(0, 2, 1)
        lf = jax.nn.log_sigmoid(mf.astype(jnp.float32) + gb[M_HEADS:]).transpose(0, 2, 1)
        hm = mlstm_chunkwise(q_m, k_m, v_m, ig, lf).transpose(0, 2, 1, 3)
        hm = rmsnorm(hm, mlstm_norm_g[l].reshape(M_HEADS, M_HEAD_DIM)).reshape(Bb, S, M_WIDTH)
        hm = (hm * jax.nn.sigmoid(mo.astype(jnp.float32))).astype(x.dtype)

        lam_init = 0.8 - 0.6 * math.exp(-0.3 * l)
        lam = (jnp.exp(jnp.sum(lambda_q1[l].astype(jnp.float32) * lambda_k1[l].astype(jnp.float32)))
               - jnp.exp(jnp.sum(lambda_q2[l].astype(jnp.float32) * lambda_k2[l].astype(jnp.float32)))
               + lam_init)
        qk_heads = lambda a: a.reshape(Bb, S, D_HEADS, 2, D_HEAD_DIM).transpose(0, 2, 3, 1, 4).astype(jnp.float32)
        q_d, k_d = qk_heads(dq), qk_heads(dk)
        v_d = dv.reshape(Bb, S, D_HEADS, 2 * D_HEAD_DIM).transpose(0, 2, 1, 3).astype(jnp.float32)
        od = diff_attention(q_d, k_d, v_d, lam)
        od = rmsnorm(od, diff_norm_g[l]) * (1.0 - lam_init)
        od = od.transpose(0, 2, 1, 3).reshape(Bb, S, D_WIDTH).astype(x.dtype)

        y = jnp.concatenate([hm, od], axis=-1) @ w_out[l]
        x = x + g1 * y

        h2 = rmsnorm(x, norm2_g[l]) * (1.0 + sc2) + sh2
        x = x + g2 * peer(h2, peer_w_query[l], peer_sub_keys[l], peer_u[l], peer_v[l])

    return rmsnorm(x, final_g)
```

```python
import functools
import math

import jax
import jax.numpy as jnp
from jax import lax
from jax.experimental import pallas as pl
from jax.experimental.pallas import tpu as pltpu

F32 = jnp.float32
BF16 = jnp.bfloat16
EPS = 1e-6
LANES = 128
VMEM_LIMIT = 56 * 1024 * 1024
M_CHUNK = 128
CONV_WIDTH = 4
P_TOPK = 16

NT = (((1,), (1,)), ((), ()))
TN = (((0,), (0,)), ((), ()))


def _params(*sem):
    return pltpu.CompilerParams(dimension_semantics=sem, vmem_limit_bytes=VMEM_LIMIT)


def _adaln_kernel(c_ref, w_ref, b_ref, o_ref):
    c = c_ref[...]
    sc = (c * jax.nn.sigmoid(c)).astype(BF16)
    o_ref[...] = jnp.dot(sc, w_ref[...].astype(BF16), preferred_element_type=F32) + b_ref[...]


def _adaln(c, w, b):
    bsz, d = c.shape
    n = w.shape[1]
    tn = 1536
    return pl.pallas_call(
        _adaln_kernel,
        grid=(n // tn,),
        in_specs=[pl.BlockSpec((bsz, d), lambda j: (0, 0)),
                  pl.BlockSpec((d, tn), lambda j: (0, j)),
                  pl.BlockSpec((1, tn), lambda j: (0, j))],
        out_specs=pl.BlockSpec((bsz, tn), lambda j: (0, j)),
        out_shape=jax.ShapeDtypeStruct((bsz, n), F32),
        compiler_params=_params("arbitrary"),
        name="adaln",
    )(c, w, b.reshape(1, n))


def _inproj_kernel(x_ref, sh_ref, sc_ref, g_ref, wm_ref, wg_ref, wd_ref, pm_ref, pg_ref, pd_ref):
    x = x_ref[0]
    ms = jnp.mean(x * x, axis=-1, keepdims=True)
    h = x * lax.rsqrt(ms + EPS) * g_ref[...]
    h = h * (1.0 + sc_ref[0]) + sh_ref[0]
    hb = h.astype(BF16)
    pm_ref[0] = jnp.dot(hb, wm_ref[...], preferred_element_type=F32)
    pg_ref[0] = jnp.dot(hb, wg_ref[...], preferred_element_type=F32)
    pd_ref[0] = jnp.dot(hb, wd_ref[...], preferred_element_type=F32).astype(BF16)


def _inproj(x, sh, sc, g, wm, wg, wd, tm=512):
    bsz, s, d = x.shape
    tm = min(tm, s)
    nm, ng, nd = wm.shape[1], wg.shape[1], wd.shape[1]
    tok = lambda b, i: (b, i, 0)
    per_b = lambda b, i: (b, 0, 0)
    const = lambda b, i: (0, 0)
    return pl.pallas_call(
        _inproj_kernel,
        grid=(bsz, s // tm),
        in_specs=[pl.BlockSpec((1, tm, d), tok),
                  pl.BlockSpec((1, 1, d), per_b),
                  pl.BlockSpec((1, 1, d), per_b),
                  pl.BlockSpec((1, d), const),
                  pl.BlockSpec((d, nm), const),
                  pl.BlockSpec((d, ng), const),
                  pl.BlockSpec((d, nd), const)],
        out_specs=[pl.BlockSpec((1, tm, nm), tok),
                   pl.BlockSpec((1, tm, ng), tok),
                   pl.BlockSpec((1, tm, nd), tok)],
        out_shape=[jax.ShapeDtypeStruct((bsz, s, nm), F32),
                   jax.ShapeDtypeStruct((bsz, s, ng), F32),
                   jax.ShapeDtypeStruct((bsz, s, nd), BF16)],
        compiler_params=_params("arbitrary", "arbitrary"),
        name="inproj",
    )(x, sh, sc, g, wm, wg, wd)


def _causal_conv(u, w, b):
    row = lax.broadcasted_iota(jnp.int32, u.shape, 0)
    y = u * w[CONV_WIDTH - 1:CONV_WIDTH, :]
    for k in range(CONV_WIDTH - 1):
        shift = CONV_WIDTH - 1 - k
        us = jnp.where(row >= shift, pltpu.roll(u, shift, axis=0), 0.0)
        y = y + us * w[k:k + 1, :]
    return y + b


def _cumsum_lanes(x):
    lane = lax.broadcasted_iota(jnp.int32, x.shape, 1)
    s = 1
    while s < x.shape[1]:
        x = x + jnp.where(lane >= s, pltpu.roll(x, s, axis=1), 0.0)
        s *= 2
    return x


def _mlstm_kernel(gb_ref, mq_ref, mk_ref, mv_ref, mo_ref, g_ref, cwq_ref, cwk_ref, cbq_ref, cbk_ref,
                  ng_ref, o_ref, q_scr, k_scr, b_scr, i_scr, *, n_heads, head_dim):
    h = pl.program_id(1)
    L = M_CHUNK
    nc = b_scr.shape[0]
    silu = lambda a: a * jax.nn.sigmoid(a)
    q_scr[...] = silu(_causal_conv(mq_ref[0], cwq_ref[...], cbq_ref[...])).astype(BF16)
    k_scr[...] = (silu(_causal_conv(mk_ref[0], cwk_ref[...], cbk_ref[...])) * (head_dim ** -0.5)).astype(BF16)

    i_scr[...] = g_ref[0, h] + gb_ref[h]
    fpre = g_ref[0, n_heads + h] + gb_ref[n_heads + h]
    lf = jnp.minimum(fpre, 0.0) - jnp.log1p(jnp.exp(-jnp.abs(fpre)))
    b_scr[...] = _cumsum_lanes(lf)

    r_i = lax.broadcasted_iota(jnp.int32, (L, L), 0)
    c_i = lax.broadcasted_iota(jnp.int32, (L, L), 1)
    eye = r_i == c_i
    causal = c_i <= r_i
    norm_g = ng_ref[...]

    def to_col(row):
        return jnp.sum(jnp.where(eye, row, 0.0), axis=1, keepdims=True)

    def chunk(c, carry):
        C, n, m = carry
        t0 = pl.multiple_of(c * L, L)
        qc = q_scr[pl.ds(t0, L), :]
        kc = k_scr[pl.ds(t0, L), :]
        vc = mv_ref[0, pl.ds(t0, L), :].astype(BF16)
        b_row = b_scr[pl.ds(c, 1), :]
        i_row = i_scr[pl.ds(c, 1), :]
        b_col = to_col(b_row)
        i_col = to_col(i_row)

        logD = jnp.where(causal, b_col - b_row + i_row, -jnp.inf)
        m_t = jnp.maximum(b_col + m, jnp.max(logD, axis=1, keepdims=True))
        Dw = jnp.exp(logD - m_t)
        inter = jnp.exp(b_col + m - m_t)
        sqk = lax.dot_general(qc, kc, NT, preferred_element_type=F32) * Dw
        num = (jnp.dot(sqk.astype(BF16), vc, preferred_element_type=F32)
               + inter * jnp.dot(qc, C.astype(BF16), preferred_element_type=F32))
        den = (jnp.sum(sqk, axis=1, keepdims=True)
               + inter * jnp.sum(qc.astype(F32) * n, axis=1, keepdims=True))
        hh = num / jnp.maximum(jnp.abs(den), jnp.exp(-m_t))

        bL = b_row[:, L - 1:L]
        m_new = jnp.maximum(bL + m, jnp.max(bL - b_row + i_row, axis=1, keepdims=True))
        w_col = jnp.exp(bL - b_col + i_col - m_new)
        decay = jnp.exp(bL + m - m_new)
        kw = kc.astype(F32) * w_col
        C_new = decay * C + lax.dot_general(kw.astype(BF16), vc, TN, preferred_element_type=F32)
        n_new = decay * n + jnp.sum(kw, axis=0, keepdims=True)

        y = hh * lax.rsqrt(jnp.mean(hh * hh, axis=-1, keepdims=True) + EPS) * norm_g
        y = y * jax.nn.sigmoid(mo_ref[0, pl.ds(t0, L), :])
        o_ref[0, pl.ds(t0, L), :] = y.astype(o_ref.dtype)
        return C_new, n_new, m_new

    init = (jnp.zeros((head_dim, head_dim), F32), jnp.zeros((1, head_dim), F32), jnp.zeros((1, 1), F32))
    lax.fori_loop(0, nc, chunk, init)


def _mlstm(pm, gates, gate_b, conv_w, conv_b, norm_g, n_heads, head_dim):
    bsz, s, _ = pm.shape
    nc = s // M_CHUNK
    hd = head_dim
    col = lambda off: (lambda b, h: (b, 0, off + h))
    wcol = lambda off: (lambda b, h: (0, off + h))
    kern = functools.partial(_mlstm_kernel, n_heads=n_heads, head_dim=hd)
    return pl.pallas_call(
        kern,
        grid=(bsz, n_heads),
        in_specs=[pl.BlockSpec(memory_space=pltpu.SMEM),
                  pl.BlockSpec((1, s, hd), col(0)),
                  pl.BlockSpec((1, s, hd), col(n_heads)),
                  pl.BlockSpec((1, s, hd), col(2 * n_heads)),
                  pl.BlockSpec((1, s, hd), col(3 * n_heads)),
                  pl.BlockSpec((1, 2 * n_heads, nc, M_CHUNK), lambda b, h: (b, 0, 0, 0)),
                  pl.BlockSpec((CONV_WIDTH, hd), wcol(0)),
                  pl.BlockSpec((CONV_WIDTH, hd), wcol(n_heads)),
                  pl.BlockSpec((1, hd), wcol(0)),
                  pl.BlockSpec((1, hd), wcol(n_heads)),
                  pl.BlockSpec((1, hd), wcol(0))],
        out_specs=pl.BlockSpec((1, s, hd), col(0)),
        out_shape=jax.ShapeDtypeStruct((bsz, s, n_heads * hd), BF16),
        scratch_shapes=[pltpu.VMEM((s, hd), BF16), pltpu.VMEM((s, hd), BF16),
                        pltpu.VMEM((nc, M_CHUNK), F32), pltpu.VMEM((nc, M_CHUNK), F32)],
        compiler_params=_params("arbitrary", "arbitrary"),
        name="mlstm",
    )(gate_b, pm, pm, pm, pm, gates, conv_w, conv_w, conv_b, conv_b, norm_g)


def _diffattn_kernel(q_ref, k_ref, v_ref, lam_ref, ng_ref, o_ref, *, dh, lam_init, tk):
    i = pl.program_id(2)
    tq = q_ref.shape[1]
    q = q_ref[0]
    lane = lax.broadcasted_iota(jnp.int32, q.shape, 1)
    zero = jnp.zeros_like(q)
    q1 = jnp.where(lane < dh, q, zero)
    q2 = jnp.where(lane < dh, zero, q)
    scale = dh ** -0.5
    lp = lam_ref[...]
    lam = (jnp.exp(jnp.sum(lp[0:1] * lp[1:2], axis=1, keepdims=True))
           - jnp.exp(jnp.sum(lp[2:3] * lp[3:4], axis=1, keepdims=True)) + lam_init)
    qpos = i * tq + lax.broadcasted_iota(jnp.int32, (tq, tk), 0)
    kloc = lax.broadcasted_iota(jnp.int32, (tq, tk), 1)

    def step(j, carry):
        k0 = pl.multiple_of(j * tk, tk)
        kb = k_ref[0, pl.ds(k0, tk), :]
        vb = v_ref[0, pl.ds(k0, tk), :]
        mask = (k0 + kloc) <= qpos
        new = []
        for qq, (m, l, acc) in zip((q1, q2), carry):
            s = lax.dot_general(qq, kb, NT, preferred_element_type=F32) * scale
            s = jnp.where(mask, s, -jnp.inf)
            m_new = jnp.maximum(m, jnp.max(s, axis=1, keepdims=True))
            alpha = jnp.exp(m - m_new)
            p = jnp.exp(s - m_new)
            l_new = alpha * l + jnp.sum(p, axis=1, keepdims=True)
            acc_new = alpha * acc + jnp.dot(p.astype(BF16), vb, preferred_element_type=F32)
            new.append((m_new, l_new, acc_new))
        return tuple(new)

    dv = v_ref.shape[2]
    one = (jnp.full((tq, 1), -jnp.inf, F32), jnp.zeros((tq, 1), F32), jnp.zeros((tq, dv), F32))
    n_kv = (i * tq + tq + tk - 1) // tk
    (m1, l1, a1), (m2, l2, a2) = lax.fori_loop(0, n_kv, step, (one, one))
    o = a1 / l1 - lam * (a2 / l2)
    y = o * lax.rsqrt(jnp.mean(o * o, axis=-1, keepdims=True) + EPS) * ng_ref[...] * (1.0 - lam_init)
    o_ref[0] = y.astype(o_ref.dtype)


def _diffattn(pd, lam_p, norm_g, n_heads, dh, lam_init, tq=256, tk=256):
    bsz, s, _ = pd.shape
    hw = 2 * dh
    kern = functools.partial(_diffattn_kernel, dh=dh, lam_init=lam_init, tk=tk)
    return pl.pallas_call(
        kern,
        grid=(bsz, n_heads, s // tq),
        in_specs=[pl.BlockSpec((1, tq, hw), lambda b, h, i: (b, i, h)),
                  pl.BlockSpec((1, s, hw), lambda b, h, i: (b, 0, n_heads + h)),
                  pl.BlockSpec((1, s, hw), lambda b, h, i: (b, 0, 2 * n_heads + h)),
                  pl.BlockSpec((4, dh), lambda b, h, i: (0, 0)),
                  pl.BlockSpec((1, hw), lambda b, h, i: (0, 0))],
        out_specs=pl.BlockSpec((1, tq, hw), lambda b, h, i: (b, i, h)),
        out_shape=jax.ShapeDtypeStruct((bsz, s, n_heads * hw), BF16),
        compiler_params=_params("arbitrary", "arbitrary", "arbitrary"),
        name="diffattn",
    )(pd, pd, pd, lam_p, norm_g)


def _outproj_kernel(hm_ref, od_ref, x_ref, g1_ref, sh_ref, sc_ref, ng_ref, wm_ref, wd_ref, wq_ref,
                    x1_ref, h2_ref, q_ref):
    y = (jnp.dot(hm_ref[0], wm_ref[...], preferred_element_type=F32)
         + jnp.dot(od_ref[0], wd_ref[...], preferred_element_type=F32))
    x1 = x_ref[0] + g1_ref[0] * y
    x1_ref[0] = x1
    ms = jnp.mean(x1 * x1, axis=-1, keepdims=True)
    h2 = x1 * lax.rsqrt(ms + EPS) * ng_ref[...]
    h2 = (h2 * (1.0 + sc_ref[0]) + sh_ref[0]).astype(BF16)
    h2_ref[0] = h2
    q_ref[0] = jnp.dot(h2, wq_ref[...], preferred_element_type=F32).astype(BF16)


def _outproj(hm, od, x, g1, sh2, sc2, ng, wm, wd, wq, tm=512):
    bsz, s, d = x.shape
    tm = min(tm, s)
    wm_w, wd_w, nq = hm.shape[2], od.shape[2], wq.shape[1]
    tok = lambda b, i: (b, i, 0)
    per_b = lambda b, i: (b, 0, 0)
    const = lambda b, i: (0, 0)
    return pl.pallas_call(
        _outproj_kernel,
        grid=(bsz, s // tm),
        in_specs=[pl.BlockSpec((1, tm, wm_w), tok),
                  pl.BlockSpec((1, tm, wd_w), tok),
                  pl.BlockSpec((1, tm, d), tok),
                  pl.BlockSpec((1, 1, d), per_b),
                  pl.BlockSpec((1, 1, d), per_b),
                  pl.BlockSpec((1, 1, d), per_b),
                  pl.BlockSpec((1, d), const),
                  pl.BlockSpec((wm_w, d), const),
                  pl.BlockSpec((wd_w, d), const),
                  pl.BlockSpec((d, nq), const)],
        out_specs=[pl.BlockSpec((1, tm, d), tok),
                   pl.BlockSpec((1, tm, d), tok),
                   pl.BlockSpec((1, tm, nq), tok)],
        out_shape=[jax.ShapeDtypeStruct((bsz, s, d), F32),
                   jax.ShapeDtypeStruct((bsz, s, d), BF16),
                   jax.ShapeDtypeStruct((bsz, s, nq), BF16)],
        compiler_params=_params("arbitrary", "arbitrary"),
        name="outproj",
    )(hm, od, x, g1, sh2, sc2, ng, wm, wd, wq)


def _topk_rows(s, k):
    n_rows = s.shape[0]
    row = lax.broadcasted_iota(jnp.int32, s.shape, 0)
    vals, idxs = [], []
    for _ in range(k):
        m = jnp.max(s, axis=0, keepdims=True)
        idx = jnp.min(jnp.where(s == m, row, n_rows), axis=0, keepdims=True)
        vals.append(m)
        idxs.append(idx)
        s = jnp.where(row == idx, -jnp.inf, s)
    return jnp.concatenate(vals, axis=0), jnp.concatenate(idxs, axis=0)


def _route_kernel(q_ref, keys_ref, i_ref, j_ref, g_ref, sv_scr, si_scr, it_scr, jt_scr, gt_scr,
                  *, n_heads, n_keys):
    K = P_TOPK

    def stage1(hp, _):
        c0 = pl.multiple_of(hp * LANES, LANES)
        qb = q_ref[:, pl.ds(c0, LANES)]
        st = lax.dot_general(keys_ref[hp], qb, NT, preferred_element_type=F32)
        v, ix = _topk_rows(st, K)
        sv_scr[hp] = v
        si_scr[hp] = ix
        return 0

    lax.fori_loop(0, 2 * n_heads, stage1, 0)

    def stage2(h, _):
        sv0, sv1 = sv_scr[2 * h], sv_scr[2 * h + 1]
        si0, si1 = si_scr[2 * h], si_scr[2 * h + 1]
        cand = jnp.concatenate([sv0[a:a + 1, :] + sv1 for a in range(K)], axis=0)
        cidx = jnp.concatenate([si0[a:a + 1, :] * n_keys + si1 for a in range(K)], axis=0)
        fv, fpos = _topk_rows(cand, K)
        row = lax.broadcasted_iota(jnp.int32, cand.shape, 0)
        eidx = jnp.concatenate(
            [jnp.sum(jnp.where(row == fpos[r:r + 1, :], cidx, 0), axis=0, keepdims=True) for r in range(K)],
            axis=0)
        e = jnp.exp(fv - fv[0:1, :])
        g = e / jnp.sum(e, axis=0, keepdims=True)
        r0 = pl.multiple_of(h * K, K)
        it_scr[pl.ds(r0, K), :] = eidx // n_keys
        jt_scr[pl.ds(r0, K), :] = eidx % n_keys
        gt_scr[pl.ds(r0, K), :] = g
        return 0

    lax.fori_loop(0, n_heads, stage2, 0)
    i_ref[...] = it_scr[...].T
    j_ref[...] = jt_scr[...].T
    g_ref[...] = gt_scr[...].T


def _route(q, keys, n_heads, n_keys):
    n = q.shape[0]
    t = LANES
    slots = n_heads * P_TOPK
    kern = functools.partial(_route_kernel, n_heads=n_heads, n_keys=n_keys)
    out = jax.ShapeDtypeStruct((n, slots), jnp.int32)
    return pl.pallas_call(
        kern,
        grid=(n // t,),
        in_specs=[pl.BlockSpec((t, q.shape[1]), lambda i: (i, 0)),
                  pl.BlockSpec(keys.shape, lambda i: (0, 0, 0))],
        out_specs=[pl.BlockSpec((t, slots), lambda i: (i, 0))] * 3,
        out_shape=[out, out, jax.ShapeDtypeStruct((n, slots), F32)],
        scratch_shapes=[pltpu.VMEM((2 * n_heads, P_TOPK, t), F32),
                        pltpu.VMEM((2 * n_heads, P_TOPK, t), jnp.int32),
                        pltpu.VMEM((slots, t), jnp.int32),
                        pltpu.VMEM((slots, t), jnp.int32),
                        pltpu.VMEM((slots, t), F32)],
        compiler_params=_params("arbitrary"),
        name="route",
    )(q, keys)


def _gatemat_kernel(i_ref, j_ref, g_ref, o_ref, *, n_keys):
    tb = o_ref.shape[0]
    slots = i_ref.shape[1]
    sub = lax.broadcasted_iota(jnp.int32, (n_keys, slots), 0)

    def body(t, _):
        i_row = i_ref[pl.ds(t, 1), :]
        j_row = j_ref[pl.ds(t, 1), :]
        g_row = g_ref[pl.ds(t, 1), :]
        a_t = jnp.where(sub == i_row, 1.0, 0.0).astype(BF16)
        b_t = jnp.where(sub == j_row, g_row, 0.0).astype(BF16)
        o_ref[t] = lax.dot_general(a_t, b_t, NT, preferred_element_type=F32).astype(o_ref.dtype)
        return 0

    lax.fori_loop(0, tb, body, 0)


def _gatemat(islot, jslot, gate, n_keys, tb=128):
    n, slots = islot.shape
    kern = functools.partial(_gatemat_kernel, n_keys=n_keys)
    return pl.pallas_call(
        kern,
        grid=(n // tb,),
        in_specs=[pl.BlockSpec((tb, slots), lambda i: (i, 0))] * 3,
        out_specs=pl.BlockSpec((tb, n_keys, n_keys), lambda i: (i, 0, 0)),
        out_shape=jax.ShapeDtypeStruct((n, n_keys, n_keys), BF16),
        compiler_params=_params("arbitrary"),
        name="gatemat",
    )(islot, jslot, gate)


def _experts_kernel(h2_ref, u_ref, v_ref, m_ref, x1_ref, g2_ref, fg_ref, o_ref, acc_ref, *, final):
    j = pl.program_id(2)

    @pl.when(j == 0)
    def _():
        acc_ref[...] = jnp.zeros_like(acc_ref)

    s = lax.dot_general(h2_ref[0], u_ref[...], NT, preferred_element_type=F32)
    act = 0.5 * s * (1.0 + lax.erf(s * (2.0 ** -0.5)))
    w = (m_ref[0].astype(F32) * act).astype(BF16)
    acc_ref[...] += jnp.dot(w, v_ref[...], preferred_element_type=F32)

    @pl.when(j == pl.num_programs(2) - 1)
    def _():
        x2 = x1_ref[0] + g2_ref[0] * acc_ref[...]
        if final:
            x2 = x2 * lax.rsqrt(jnp.mean(x2 * x2, axis=-1, keepdims=True) + EPS) * fg_ref[...]
        o_ref[0] = x2


def _experts(h2, u, v, gmat, x1, g2, final_g, final, tm=1024, te=512):
    bsz, s, d = x1.shape
    tm = min(tm, s)
    ne = u.shape[0]
    kern = functools.partial(_experts_kernel, final=final)
    return pl.pallas_call(
        kern,
        grid=(bsz, s // tm, ne // te),
        in_specs=[pl.BlockSpec((1, tm, d), lambda b, i, j: (b, i, 0)),
                  pl.BlockSpec((te, d), lambda b, i, j: (j, 0)),
                  pl.BlockSpec((te, d), lambda b, i, j: (j, 0)),
                  pl.BlockSpec((1, tm, te), lambda b, i, j: (b, i, j)),
                  pl.BlockSpec((1, tm, d), lambda b, i, j: (b, i, 0)),
                  pl.BlockSpec((1, 1, d), lambda b, i, j: (b, 0, 0)),
                  pl.BlockSpec((1, d), lambda b, i, j: (0, 0))],
        out_specs=pl.BlockSpec((1, tm, d), lambda b, i, j: (b, i, 0)),
        out_shape=jax.ShapeDtypeStruct((bsz, s, d), F32),
        scratch_shapes=[pltpu.VMEM((tm, d), F32)],
        compiler_params=_params("arbitrary", "arbitrary", "arbitrary"),
        name="experts",
    )(h2, u, v, gmat, x1, g2, final_g)


def kernel(x, c, ada_w, ada_b, norm1_g, w_in, conv_w, conv_b, mlstm_gate_b, mlstm_norm_g, lambda_q1,
           lambda_k1, lambda_q2, lambda_k2, diff_norm_g, w_out, norm2_g, peer_w_query, peer_sub_keys,
           peer_u, peer_v, final_g):
    bsz, s, d = x.shape
    depth = ada_w.shape[0]
    m_heads = mlstm_gate_b.shape[1] // 2
    m_width = mlstm_norm_g.shape[1]
    m_hd = m_width // m_heads
    dh = lambda_q1.shape[1]
    d_width = w_out.shape[1] - m_width
    d_heads = d_width // (2 * dh)
    p_heads, _, n_keys, _ = peer_sub_keys.shape[1:]
    n_m = 4 * m_width
    n_g = 2 * m_heads
    assert s % M_CHUNK == 0 and m_hd == LANES and 2 * dh == LANES and n_keys == LANES

    for l in range(depth):
        mod = _adaln(c, ada_w[l], ada_b[l])
        sh1, sc1, g1, sh2, sc2, g2 = (a.reshape(bsz, 1, d) for a in jnp.split(mod, 6, axis=-1))

        w = w_in[l].astype(BF16)
        w_m = w[:, :n_m]
        w_g = jnp.pad(w[:, n_m:n_m + n_g], ((0, 0), (0, LANES - n_g)))
        w_d = w[:, n_m + n_g:]
        pm, pg, pd = _inproj(x, sh1, sc1, norm1_g[l].reshape(1, d), w_m, w_g, w_d)

        gates = pg[:, :, :n_g].transpose(0, 2, 1).reshape(bsz, n_g, s // M_CHUNK, M_CHUNK)
        hm = _mlstm(pm, gates, mlstm_gate_b[l], conv_w[l], conv_b[l].reshape(1, -1),
                    mlstm_norm_g[l].reshape(1, -1), m_heads, m_hd)

        lam_init = 0.8 - 0.6 * math.exp(-0.3 * l)
        lam_p = jnp.stack([lambda_q1[l], lambda_k1[l], lambda_q2[l], lambda_k2[l]])
        od = _diffattn(pd, lam_p, diff_norm_g[l].reshape(1, -1), d_heads, dh, lam_init)

        wo = w_out[l].astype(BF16)
        x1, h2, q = _outproj(hm, od, x, g1, sh2, sc2, norm2_g[l].reshape(1, d), wo[:m_width], wo[m_width:],
                             peer_w_query[l].astype(BF16))

        keys = peer_sub_keys[l].reshape(2 * p_heads, n_keys, -1).astype(BF16)
        islot, jslot, gate = _route(q.reshape(bsz * s, -1), keys, p_heads, n_keys)
        gmat = _gatemat(islot, jslot, gate, n_keys).reshape(bsz, s, n_keys * n_keys)
        x = _experts(h2, peer_u[l].astype(BF16), peer_v[l].astype(BF16), gmat, x1, g2,
                     final_g.reshape(1, d), final=(l == depth - 1))
    return x
```

```python
import functools
import math

import jax
import jax.numpy as jnp
from jax import lax
from jax.experimental import pallas as pl
from jax.experimental.pallas import tpu as pltpu

F32 = jnp.float32
BF16 = jnp.bfloat16
EPS = 1e-6
LANES = 128
SUBLANES = 8
VMEM_LIMIT = 56 * 1024 * 1024
M_CHUNK = 128
CONV_WIDTH = 4
P_TOPK = 16

NT = (((1,), (1,)), ((), ()))
TN = (((0,), (0,)), ((), ()))


def _params(*sem):
    return pltpu.CompilerParams(dimension_semantics=sem, vmem_limit_bytes=VMEM_LIMIT)


def _adaln_kernel(c_ref, w_ref, b_ref, o_ref):
    c = c_ref[...]
    sc = (c * jax.nn.sigmoid(c)).astype(BF16)
    o_ref[...] = jnp.dot(sc, w_ref[...].astype(BF16), preferred_element_type=F32) + b_ref[...]


def _adaln(c, w, b):
    bsz, d = c.shape
    n = w.shape[1]
    tn = 1536
    return pl.pallas_call(
        _adaln_kernel,
        grid=(n // tn,),
        in_specs=[pl.BlockSpec((bsz, d), lambda j: (0, 0)),
                  pl.BlockSpec((d, tn), lambda j: (0, j)),
                  pl.BlockSpec((1, tn), lambda j: (0, j))],
        out_specs=pl.BlockSpec((bsz, tn), lambda j: (0, j)),
        out_shape=jax.ShapeDtypeStruct((bsz, n), F32),
        compiler_params=_params("arbitrary"),
        name="adaln",
    )(c, w, b.reshape(1, n))


def _inproj_kernel(x_ref, sh_ref, sc_ref, g_ref, wm_ref, wg_ref, wd_ref, pm_ref, pg_ref, pd_ref):
    x = x_ref[0]
    ms = jnp.mean(x * x, axis=-1, keepdims=True)
    h = x * lax.rsqrt(ms + EPS) * g_ref[...]
    h = h * (1.0 + sc_ref[0]) + sh_ref[0]
    hb = h.astype(BF16)
    pm_ref[0] = jnp.dot(hb, wm_ref[...], preferred_element_type=F32)
    pg_ref[0] = jnp.dot(hb, wg_ref[...], preferred_element_type=F32)
    pd_ref[0] = jnp.dot(hb, wd_ref[...], preferred_element_type=F32).astype(BF16)


def _inproj(x, sh, sc, g, wm, wg, wd, tm=512):
    bsz, s, d = x.shape
    tm = min(tm, s)
    nm, ng, nd = wm.shape[1], wg.shape[1], wd.shape[1]
    tok = lambda b, i: (b, i, 0)
    per_b = lambda b, i: (b, 0, 0)
    const = lambda b, i: (0, 0)
    return pl.pallas_call(
        _inproj_kernel,
        grid=(bsz, s // tm),
        in_specs=[pl.BlockSpec((1, tm, d), tok),
                  pl.BlockSpec((1, 1, d), per_b),
                  pl.BlockSpec((1, 1, d), per_b),
                  pl.BlockSpec((1, d), const),
                  pl.BlockSpec((d, nm), const),
                  pl.BlockSpec((d, ng), const),
                  pl.BlockSpec((d, nd), const)],
        out_specs=[pl.BlockSpec((1, tm, nm), tok),
                   pl.BlockSpec((1, tm, ng), tok),
                   pl.BlockSpec((1, tm, nd), tok)],
        out_shape=[jax.ShapeDtypeStruct((bsz, s, nm), F32),
                   jax.ShapeDtypeStruct((bsz, s, ng), F32),
                   jax.ShapeDtypeStruct((bsz, s, nd), BF16)],
        compiler_params=_params("arbitrary", "arbitrary"),
        name="inproj",
    )(x, sh, sc, g, wm, wg, wd)


def _causal_conv(u, w, b):
    row = lax.broadcasted_iota(jnp.int32, u.shape, 0)
    y = u * w[CONV_WIDTH - 1:CONV_WIDTH, :]
    for k in range(CONV_WIDTH - 1):
        shift = CONV_WIDTH - 1 - k
        us = jnp.where(row >= shift, pltpu.roll(u, shift, axis=0), 0.0)
        y = y + us * w[k:k + 1, :]
    return y + b


def _cumsum_lanes(x):
    lane = lax.broadcasted_iota(jnp.int32, x.shape, 1)
    s = 1
    while s < x.shape[1]:
        x = x + jnp.where(lane >= s, pltpu.roll(x, s, axis=1), 0.0)
        s *= 2
    return x


def _mlstm_kernel(gb_ref, mq_ref, mk_ref, mv_ref, mo_ref, g_ref, cwq_ref, cwk_ref, cbq_ref, cbk_ref,
                  ng_ref, o_ref, q_scr, k_scr, b_scr, i_scr, *, n_heads, head_dim):
    h = pl.program_id(1)
    L = M_CHUNK
    nc = b_scr.shape[0]
    silu = lambda a: a * jax.nn.sigmoid(a)
    q_scr[...] = silu(_causal_conv(mq_ref[0], cwq_ref[...], cbq_ref[...])).astype(BF16)
    k_scr[...] = (silu(_causal_conv(mk_ref[0], cwk_ref[...], cbk_ref[...])) * (head_dim ** -0.5)).astype(BF16)

    i_scr[...] = g_ref[0, h] + gb_ref[h]
    fpre = g_ref[0, n_heads + h] + gb_ref[n_heads + h]
    lf = jnp.minimum(fpre, 0.0) - jnp.log1p(jnp.exp(-jnp.abs(fpre)))
    b_scr[...] = _cumsum_lanes(lf)

    r_i = lax.broadcasted_iota(jnp.int32, (L, L), 0)
    c_i = lax.broadcasted_iota(jnp.int32, (L, L), 1)
    eye = r_i == c_i
    causal = c_i <= r_i
    norm_g = ng_ref[...]

    def to_col(row):
        return jnp.sum(jnp.where(eye, row, 0.0), axis=1, keepdims=True)

    def chunk(c, carry):
        C, n, m = carry
        t0 = pl.multiple_of(c * L, L)
        qc = q_scr[pl.ds(t0, L), :]
        kc = k_scr[pl.ds(t0, L), :]
        vc = mv_ref[0, pl.ds(t0, L), :].astype(BF16)
        b_row = b_scr[pl.ds(c, 1), :]
        i_row = i_scr[pl.ds(c, 1), :]
        b_col = to_col(b_row)
        i_col = to_col(i_row)

        logD = jnp.where(causal, b_col - b_row + i_row, -jnp.inf)
        m_t = jnp.maximum(b_col + m, jnp.max(logD, axis=1, keepdims=True))
        Dw = jnp.exp(logD - m_t)
        inter = jnp.exp(b_col + m - m_t)
        sqk = lax.dot_general(qc, kc, NT, preferred_element_type=F32) * Dw
        num = (jnp.dot(sqk.astype(BF16), vc, preferred_element_type=F32)
               + inter * jnp.dot(qc, C.astype(BF16), preferred_element_type=F32))
        den = (jnp.sum(sqk, axis=1, keepdims=True)
               + inter * jnp.sum(qc.astype(F32) * n, axis=1, keepdims=True))
        hh = num / jnp.maximum(jnp.abs(den), jnp.exp(-m_t))

        bL = b_row[:, L - 1:L]
        m_new = jnp.maximum(bL + m, jnp.max(bL - b_row + i_row, axis=1, keepdims=True))
        w_col = jnp.exp(bL - b_col + i_col - m_new)
        decay = jnp.exp(bL + m - m_new)
        kw = kc.astype(F32) * w_col
        C_new = decay * C + lax.dot_general(kw.astype(BF16), vc, TN, preferred_element_type=F32)
        n_new = decay * n + jnp.sum(kw, axis=0, keepdims=True)

        y = hh * lax.rsqrt(jnp.mean(hh * hh, axis=-1, keepdims=True) + EPS) * norm_g
        y = y * jax.nn.sigmoid(mo_ref[0, pl.ds(t0, L), :])
        o_ref[0, pl.ds(t0, L), :] = y.astype(o_ref.dtype)
        return C_new, n_new, m_new

    init = (jnp.zeros((head_dim, head_dim), F32), jnp.zeros((1, head_dim), F32), jnp.zeros((1, 1), F32))
    lax.fori_loop(0, nc, chunk, init)


def _mlstm(pm, gates, gate_b, conv_w, conv_b, norm_g, n_heads, head_dim):
    bsz, s, _ = pm.shape
    nc = s // M_CHUNK
    hd = head_dim
    col = lambda off: (lambda b, h: (b, 0, off + h))
    wcol = lambda off: (lambda b, h: (0, off + h))
    kern = functools.partial(_mlstm_kernel, n_heads=n_heads, head_dim=hd)
    return pl.pallas_call(
        kern,
        grid=(bsz, n_heads),
        in_specs=[pl.BlockSpec(memory_space=pltpu.SMEM),
                  pl.BlockSpec((1, s, hd), col(0)),
                  pl.BlockSpec((1, s, hd), col(n_heads)),
                  pl.BlockSpec((1, s, hd), col(2 * n_heads)),
                  pl.BlockSpec((1, s, hd), col(3 * n_heads)),
                  pl.BlockSpec((1, 2 * n_heads, nc, M_CHUNK), lambda b, h: (b, 0, 0, 0)),
                  pl.BlockSpec((CONV_WIDTH, hd), wcol(0)),
                  pl.BlockSpec((CONV_WIDTH, hd), wcol(n_heads)),
                  pl.BlockSpec((1, hd), wcol(0)),
                  pl.BlockSpec((1, hd), wcol(n_heads)),
                  pl.BlockSpec((1, hd), wcol(0))],
        out_specs=pl.BlockSpec((1, s, hd), col(0)),
        out_shape=jax.ShapeDtypeStruct((bsz, s, n_heads * hd), BF16),
        scratch_shapes=[pltpu.VMEM((s, hd), BF16), pltpu.VMEM((s, hd), BF16),
                        pltpu.VMEM((nc, M_CHUNK), F32), pltpu.VMEM((nc, M_CHUNK), F32)],
        compiler_params=_params("arbitrary", "arbitrary"),
        name="mlstm",
    )(gate_b, pm, pm, pm, pm, gates, conv_w, conv_w, conv_b, conv_b, norm_g)


def _diffattn_kernel(q_ref, k_ref, v_ref, lam_ref, ng_ref, o_ref, *, dh, lam_init, tk):
    i = pl.program_id(2)
    tq = q_ref.shape[1]
    q = q_ref[0]
    lane = lax.broadcasted_iota(jnp.int32, q.shape, 1)
    zero = jnp.zeros_like(q)
    q1 = jnp.where(lane < dh, q, zero)
    q2 = jnp.where(lane < dh, zero, q)
    scale = dh ** -0.5
    lp = lam_ref[...]
    lam = (jnp.exp(jnp.sum(lp[0:1] * lp[1:2], axis=1, keepdims=True))
           - jnp.exp(jnp.sum(lp[2:3] * lp[3:4], axis=1, keepdims=True)) + lam_init)
    qpos = i * tq + lax.broadcasted_iota(jnp.int32, (tq, tk), 0)
    kloc = lax.broadcasted_iota(jnp.int32, (tq, tk), 1)

    def step(j, carry):
        k0 = pl.multiple_of(j * tk, tk)
        kb = k_ref[0, pl.ds(k0, tk), :]
        vb = v_ref[0, pl.ds(k0, tk), :]
        mask = (k0 + kloc) <= qpos
        new = []
        for qq, (m, l, acc) in zip((q1, q2), carry):
            s = lax.dot_general(qq, kb, NT, preferred_element_type=F32) * scale
            s = jnp.where(mask, s, -jnp.inf)
            m_new = jnp.maximum(m, jnp.max(s, axis=1, keepdims=True))
            alpha = jnp.exp(m - m_new)
            p = jnp.exp(s - m_new)
            l_new = alpha * l + jnp.sum(p, axis=1, keepdims=True)
            acc_new = alpha * acc + jnp.dot(p.astype(BF16), vb, preferred_element_type=F32)
            new.append((m_new, l_new, acc_new))
        return tuple(new)

    dv = v_ref.shape[2]
    one = (jnp.full((tq, 1), -jnp.inf, F32), jnp.zeros((tq, 1), F32), jnp.zeros((tq, dv), F32))
    n_kv = (i * tq + tq + tk - 1) // tk
    (m1, l1, a1), (m2, l2, a2) = lax.fori_loop(0, n_kv, step, (one, one))
    o = a1 / l1 - lam * (a2 / l2)
    y = o * lax.rsqrt(jnp.mean(o * o, axis=-1, keepdims=True) + EPS) * ng_ref[...] * (1.0 - lam_init)
    o_ref[0] = y.astype(o_ref.dtype)


def _diffattn(pd, lam_p, norm_g, n_heads, dh, lam_init, tq=256, tk=256):
    bsz, s, _ = pd.shape
    hw = 2 * dh
    kern = functools.partial(_diffattn_kernel, dh=dh, lam_init=lam_init, tk=tk)
    return pl.pallas_call(
        kern,
        grid=(bsz, n_heads, s // tq),
        in_specs=[pl.BlockSpec((1, tq, hw), lambda b, h, i: (b, i, h)),
                  pl.BlockSpec((1, s, hw), lambda b, h, i: (b, 0, n_heads + h)),
                  pl.BlockSpec((1, s, hw), lambda b, h, i: (b, 0, 2 * n_heads + h)),
                  pl.BlockSpec((4, dh), lambda b, h, i: (0, 0)),
                  pl.BlockSpec((1, hw), lambda b, h, i: (0, 0))],
        out_specs=pl.BlockSpec((1, tq, hw), lambda b, h, i: (b, i, h)),
        out_shape=jax.ShapeDtypeStruct((bsz, s, n_heads * hw), BF16),
        compiler_params=_params("arbitrary", "arbitrary", "arbitrary"),
        name="diffattn",
    )(pd, pd, pd, lam_p, norm_g)


def _outproj_kernel(hm_ref, od_ref, x_ref, g1_ref, sh_ref, sc_ref, ng_ref, wm_ref, wd_ref, wq_ref,
                    x1_ref, h2_ref, q_ref):
    y = (jnp.dot(hm_ref[0], wm_ref[...], preferred_element_type=F32)
         + jnp.dot(od_ref[0], wd_ref[...], preferred_element_type=F32))
    x1 = x_ref[0] + g1_ref[0] * y
    x1_ref[0] = x1
    ms = jnp.mean(x1 * x1, axis=-1, keepdims=True)
    h2 = x1 * lax.rsqrt(ms + EPS) * ng_ref[...]
    h2 = (h2 * (1.0 + sc_ref[0]) + sh_ref[0]).astype(BF16)
    h2_ref[0] = h2
    q_ref[0] = jnp.dot(h2, wq_ref[...], preferred_element_type=F32).astype(BF16)


def _outproj(hm, od, x, g1, sh2, sc2, ng, wm, wd, wq, tm=512):
    bsz, s, d = x.shape
    tm = min(tm, s)
    wm_w, wd_w, nq = hm.shape[2], od.shape[2], wq.shape[1]
    tok = lambda b, i: (b, i, 0)
    per_b = lambda b, i: (b, 0, 0)
    const = lambda b, i: (0, 0)
    return pl.pallas_call(
        _outproj_kernel,
        grid=(bsz, s // tm),
        in_specs=[pl.BlockSpec((1, tm, wm_w), tok),
                  pl.BlockSpec((1, tm, wd_w), tok),
                  pl.BlockSpec((1, tm, d), tok),
                  pl.BlockSpec((1, 1, d), per_b),
                  pl.BlockSpec((1, 1, d), per_b),
                  pl.BlockSpec((1, 1, d), per_b),
                  pl.BlockSpec((1, d), const),
                  pl.BlockSpec((wm_w, d), const),
                  pl.BlockSpec((wd_w, d), const),
                  pl.BlockSpec((d, nq), const)],
        out_specs=[pl.BlockSpec((1, tm, d), tok),
                   pl.BlockSpec((1, tm, d), tok),
                   pl.BlockSpec((1, tm, nq), tok)],
        out_shape=[jax.ShapeDtypeStruct((bsz, s, d), F32),
                   jax.ShapeDtypeStruct((bsz, s, d), BF16),
                   jax.ShapeDtypeStruct((bsz, s, nq), BF16)],
        compiler_params=_params("arbitrary", "arbitrary"),
        name="outproj",
    )(hm, od, x, g1, sh2, sc2, ng, wm, wd, wq)


def _topk_rows(s, k, payload=None):
    n_rows, t = s.shape
    nv = n_rows // SUBLANES
    slab = lambda a, v: a[SUBLANES * v:SUBLANES * (v + 1), :]
    slabs = [slab(s, v) for v in range(nv)]
    sub = lax.broadcasted_iota(jnp.int32, (SUBLANES, t), 0)
    out_row = lax.broadcasted_iota(jnp.int32, (k, t), 0)
    vals = jnp.zeros((k, t), s.dtype)
    rows = jnp.zeros((k, t), jnp.int32)
    pays = None if payload is None else jnp.zeros((k, t), payload.dtype)
    for r in range(k):
        nodes = [(slabs[v], v) for v in range(nv)]
        while len(nodes) > 1:
            nxt = []
            for a in range(0, len(nodes) - 1, 2):
                (va, ia), (vb, ib) = nodes[a], nodes[a + 1]
                c = va >= vb
                nxt.append((jnp.where(c, va, vb), jnp.where(c, ia, ib)))
            if len(nodes) % 2:
                nxt.append(nodes[-1])
            nodes = nxt
        m8, v8 = nodes[0]
        m = jnp.max(m8, axis=0, keepdims=True)
        idx = jnp.min(jnp.where(m8 == m, v8 * SUBLANES + sub, n_rows), axis=0, keepdims=True)
        hits = [sub == idx - SUBLANES * v for v in range(nv)]
        vals = jnp.where(out_row == r, m, vals)
        rows = jnp.where(out_row == r, idx, rows)
        if payload is not None:
            picked = [jnp.where(hits[v], slab(payload, v), 0) for v in range(nv)]
            p = jnp.sum(functools.reduce(jnp.add, picked), axis=0, keepdims=True)
            pays = jnp.where(out_row == r, p, pays)
        slabs = [jnp.where(hits[v], -jnp.inf, slabs[v]) for v in range(nv)]
    return vals, rows, pays


def _candidate_layout(k):
    slabs, cur = [], []

    def flush():
        n_valid = len(cur)
        while len(cur) < SUBLANES:
            r = len(cur)
            period = 1
            while period < max(b for _, b in cur[:n_valid]) + 1:
                period *= 2
            b = cur[r - period][1] if r >= period else r
            cur.append((cur[n_valid - 1][0], b))
        slabs.append(([a for a, _ in cur], [b for _, b in cur], n_valid))
        cur.clear()

    for a in range(k):
        nb = k // (a + 1)
        for b0 in range(0, nb, SUBLANES):
            group = [(a, b) for b in range(b0, min(nb, b0 + SUBLANES))]
            if cur and (len(cur) + len(group) > SUBLANES or (nb == 1 and a % SUBLANES == 0)):
                flush()
            cur.extend(group)
    if cur:
        flush()
    return slabs


def _rows_by_pattern(x, pattern, sub):
    p0 = pattern[0]
    if p0 % SUBLANES == 0 and pattern == list(range(p0, p0 + SUBLANES)):
        return x[p0:p0 + SUBLANES, :]
    period = SUBLANES
    while period > 1 and all(pattern[r] == pattern[r % (period // 2)] for r in range(SUBLANES)):
        period //= 2
    pos = sub if period == SUBLANES else sub & (period - 1)
    out = x[p0:p0 + 1, :]
    for r in range(1, period):
        if pattern[r] != pattern[r - 1]:
            out = jnp.where(pos >= r, x[pattern[r]:pattern[r] + 1, :], out)
    return jnp.broadcast_to(out, (SUBLANES, x.shape[1]))


def _route_kernel(q_ref, keys_ref, i_ref, j_ref, g_ref, sv_scr, si_scr, it_scr, jt_scr, gt_scr,
                  *, n_heads, n_keys):
    K = P_TOPK
    t = q_ref.shape[0]
    sub = lax.broadcasted_iota(jnp.int32, (SUBLANES, t), 0)
    layout = _candidate_layout(K)

    def stage1(h, _):
        for p in range(2):
            hp = 2 * h + p
            c0 = pl.multiple_of(hp * LANES, LANES)
            qb = q_ref[:, pl.ds(c0, LANES)]
            st = lax.dot_general(keys_ref[hp], qb, NT, preferred_element_type=F32)
            v, ix, _ = _topk_rows(st, K)
            sv_scr[hp] = v
            si_scr[hp] = ix
        return 0

    lax.fori_loop(0, n_heads, stage1, 0)

    def stage2(hh, _):
        for u in range(2):
            h = 2 * hh + u
            sv0, sv1 = sv_scr[2 * h], sv_scr[2 * h + 1]
            si0, si1 = si_scr[2 * h], si_scr[2 * h + 1]
            cand, cidx = [], []
            for a_pat, b_pat, n_valid in layout:
                c = _rows_by_pattern(sv0, a_pat, sub) + _rows_by_pattern(sv1, b_pat, sub)
                if n_valid < SUBLANES:
                    c = jnp.where(sub < n_valid, c, -jnp.inf)
                cand.append(c)
                cidx.append(_rows_by_pattern(si0, a_pat, sub) * n_keys + _rows_by_pattern(si1, b_pat, sub))
            fv, _, eidx = _topk_rows(jnp.concatenate(cand, axis=0), K, jnp.concatenate(cidx, axis=0))
            e = jnp.exp(fv - fv[0:1, :])
            g = e / jnp.sum(e, axis=0, keepdims=True)
            r0 = pl.multiple_of(h * K, K)
            it_scr[pl.ds(r0, K), :] = eidx // n_keys
            jt_scr[pl.ds(r0, K), :] = eidx % n_keys
            gt_scr[pl.ds(r0, K), :] = g
        return 0

    lax.fori_loop(0, n_heads // 2, stage2, 0)
    i_ref[...] = it_scr[...].T
    j_ref[...] = jt_scr[...].T
    g_ref[...] = gt_scr[...].T


def _route(q, keys, n_heads, n_keys):
    n = q.shape[0]
    t = LANES
    slots = n_heads * P_TOPK
    kern = functools.partial(_route_kernel, n_heads=n_heads, n_keys=n_keys)
    out = jax.ShapeDtypeStruct((n, slots), jnp.int32)
    return pl.pallas_call(
        kern,
        grid=(n // t,),
        in_specs=[pl.BlockSpec((t, q.shape[1]), lambda i: (i, 0)),
                  pl.BlockSpec(keys.shape, lambda i: (0, 0, 0))],
        out_specs=[pl.BlockSpec((t, slots), lambda i: (i, 0))] * 3,
        out_shape=[out, out, jax.ShapeDtypeStruct((n, slots), F32)],
        scratch_shapes=[pltpu.VMEM((2 * n_heads, P_TOPK, t), F32),
                        pltpu.VMEM((2 * n_heads, P_TOPK, t), jnp.int32),
                        pltpu.VMEM((slots, t), jnp.int32),
                        pltpu.VMEM((slots, t), jnp.int32),
                        pltpu.VMEM((slots, t), F32)],
        compiler_params=_params("arbitrary"),
        name="route",
    )(q, keys)


GATE_UNROLL = 8


def _gatemat_kernel(i_ref, j_ref, g_ref, o_ref, w_scr, *, n_keys, stride):
    tb, slots = i_ref.shape
    sub = lax.broadcasted_iota(jnp.int32, (n_keys, slots), 0)

    def tokens(k, _):
        t0 = pl.multiple_of(k * GATE_UNROLL, GATE_UNROLL)
        for u in range(GATE_UNROLL):
            t = t0 + u
            i_row = i_ref[pl.ds(t, 1), :]
            j_row = j_ref[pl.ds(t, 1), :]
            g_row = g_ref[pl.ds(t, 1), :]
            a_t = jnp.where(sub == i_row, 1.0, 0.0).astype(BF16)
            b_t = jnp.where(sub == j_row, g_row, 0.0).astype(BF16)
            m_t = lax.dot_general(a_t, b_t, NT, preferred_element_type=F32)
            w_scr[pl.ds(t, n_keys, stride=stride), :] = m_t
        return 0

    lax.fori_loop(0, tb // GATE_UNROLL, tokens, 0)

    def blocks(i, _):
        r0 = pl.multiple_of(i * stride, SUBLANES)
        o_ref[i] = w_scr[pl.ds(r0, tb), :].astype(o_ref.dtype)
        return 0

    lax.fori_loop(0, n_keys, blocks, 0)


def _gatemat(islot, jslot, gate, n_keys, tb=128):
    n, slots = islot.shape
    stride = tb + SUBLANES
    kern = functools.partial(_gatemat_kernel, n_keys=n_keys, stride=stride)
    return pl.pallas_call(
        kern,
        grid=(n // tb,),
        in_specs=[pl.BlockSpec((tb, slots), lambda i: (i, 0))] * 3,
        out_specs=pl.BlockSpec((n_keys, tb, n_keys), lambda i: (0, i, 0)),
        out_shape=jax.ShapeDtypeStruct((n_keys, n, n_keys), BF16),
        scratch_shapes=[pltpu.VMEM((n_keys * stride, n_keys), F32)],
        compiler_params=_params("arbitrary"),
        name="gatemat",
    )(islot, jslot, gate)


def _experts_kernel(h2_ref, u_ref, v_ref, m_ref, x1_ref, g2_ref, fg_ref, o_ref, acc_ref, *, final):
    j = pl.program_id(2)

    @pl.when(j == 0)
    def _():
        acc_ref[...] = jnp.zeros_like(acc_ref)

    s = lax.dot_general(h2_ref[0], u_ref[...], NT, preferred_element_type=F32)
    act = 0.5 * s * (1.0 + lax.erf(s * (2.0 ** -0.5)))
    gate = jnp.concatenate([m_ref[ib] for ib in range(m_ref.shape[0])], axis=1)
    w = (gate.astype(F32) * act).astype(BF16)
    acc_ref[...] += jnp.dot(w, v_ref[...], preferred_element_type=F32)

    @pl.when(j == pl.num_programs(2) - 1)
    def _():
        x2 = x1_ref[0] + g2_ref[0] * acc_ref[...]
        if final:
            x2 = x2 * lax.rsqrt(jnp.mean(x2 * x2, axis=-1, keepdims=True) + EPS) * fg_ref[...]
        o_ref[0] = x2


def _experts(h2, u, v, gmat, x1, g2, final_g, final, tm=1024, te=512):
    bsz, s, d = x1.shape
    tm = min(tm, s)
    ne = u.shape[0]
    n_keys = gmat.shape[2]
    n_tiles = s // tm
    kern = functools.partial(_experts_kernel, final=final)
    return pl.pallas_call(
        kern,
        grid=(bsz, n_tiles, ne // te),
        in_specs=[pl.BlockSpec((1, tm, d), lambda b, i, j: (b, i, 0)),
                  pl.BlockSpec((te, d), lambda b, i, j: (j, 0)),
                  pl.BlockSpec((te, d), lambda b, i, j: (j, 0)),
                  pl.BlockSpec((te // n_keys, tm, n_keys), lambda b, i, j: (j, b * n_tiles + i, 0)),
                  pl.BlockSpec((1, tm, d), lambda b, i, j: (b, i, 0)),
                  pl.BlockSpec((1, 1, d), lambda b, i, j: (b, 0, 0)),
                  pl.BlockSpec((1, d), lambda b, i, j: (0, 0))],
        out_specs=pl.BlockSpec((1, tm, d), lambda b, i, j: (b, i, 0)),
        out_shape=jax.ShapeDtypeStruct((bsz, s, d), F32),
        scratch_shapes=[pltpu.VMEM((tm, d), F32)],
        compiler_params=_params("arbitrary", "arbitrary", "arbitrary"),
        name="experts",
    )(h2, u, v, gmat, x1, g2, final_g)


def kernel(x, c, ada_w, ada_b, norm1_g, w_in, conv_w, conv_b, mlstm_gate_b, mlstm_norm_g, lambda_q1,
           lambda_k1, lambda_q2, lambda_k2, diff_norm_g, w_out, norm2_g, peer_w_query, peer_sub_keys,
           peer_u, peer_v, final_g):
    bsz, s, d = x.shape
    depth = ada_w.shape[0]
    m_heads = mlstm_gate_b.shape[1] // 2
    m_width = mlstm_norm_g.shape[1]
    m_hd = m_width // m_heads
    dh = lambda_q1.shape[1]
    d_width = w_out.shape[1] - m_width
    d_heads = d_width // (2 * dh)
    p_heads, _, n_keys, _ = peer_sub_keys.shape[1:]
    n_m = 4 * m_width
    n_g = 2 * m_heads
    assert s % M_CHUNK == 0 and m_hd == LANES and 2 * dh == LANES and n_keys == LANES
    assert peer_sub_keys.shape[-1] == LANES and p_heads % 2 == 0

    for l in range(depth):
        mod = _adaln(c, ada_w[l], ada_b[l])
        sh1, sc1, g1, sh2, sc2, g2 = (a.reshape(bsz, 1, d) for a in jnp.split(mod, 6, axis=-1))

        w = w_in[l].astype(BF16)
        w_m = w[:, :n_m]
        w_g = jnp.pad(w[:, n_m:n_m + n_g], ((0, 0), (0, LANES - n_g)))
        w_d = w[:, n_m + n_g:]
        pm, pg, pd = _inproj(x, sh1, sc1, norm1_g[l].reshape(1, d), w_m, w_g, w_d)

        gates = pg[:, :, :n_g].transpose(0, 2, 1).reshape(bsz, n_g, s // M_CHUNK, M_CHUNK)
        hm = _mlstm(pm, gates, mlstm_gate_b[l], conv_w[l], conv_b[l].reshape(1, -1),
                    mlstm_norm_g[l].reshape(1, -1), m_heads, m_hd)

        lam_init = 0.8 - 0.6 * math.exp(-0.3 * l)
        lam_p = jnp.stack([lambda_q1[l], lambda_k1[l], lambda_q2[l], lambda_k2[l]])
        od = _diffattn(pd, lam_p, diff_norm_g[l].reshape(1, -1), d_heads, dh, lam_init)

        wo = w_out[l].astype(BF16)
        x1, h2, q = _outproj(hm, od, x, g1, sh2, sc2, norm2_g[l].reshape(1, d), wo[:m_width], wo[m_width:],
                             peer_w_query[l].astype(BF16))

        keys = peer_sub_keys[l].reshape(2 * p_heads, n_keys, -1).astype(BF16)
        islot, jslot, gate = _route(q.reshape(bsz * s, -1), keys, p_heads, n_keys)
        gmat = _gatemat(islot, jslot, gate, n_keys)
        x = _experts(h2, peer_u[l].astype(BF16), peer_v[l].astype(BF16), gmat, x1, g2,
                     final_g.reshape(1, d), final=(l == depth - 1))
    return x
```

```python
import functools
import math

import jax
import jax.numpy as jnp
from jax import lax
from jax.experimental import pallas as pl
from jax.experimental.pallas import tpu as pltpu

F32 = jnp.float32
BF16 = jnp.bfloat16
EPS = 1e-6
LANES = 128
SUBLANES = 8
VMEM_LIMIT = 56 * 1024 * 1024
M_CHUNK = 128
CONV_WIDTH = 4
P_TOPK = 16

NT = (((1,), (1,)), ((), ()))
TN = (((0,), (0,)), ((), ()))


def _params(*sem):
    return pltpu.CompilerParams(dimension_semantics=sem, vmem_limit_bytes=VMEM_LIMIT)


def _adaln_kernel(c_ref, w_ref, b_ref, o_ref):
    c = c_ref[...]
    sc = (c * jax.nn.sigmoid(c)).astype(BF16)
    o_ref[...] = jnp.dot(sc, w_ref[...].astype(BF16), preferred_element_type=F32) + b_ref[...]


def _adaln(c, w, b):
    bsz, d = c.shape
    n = w.shape[1]
    tn = 1536
    return pl.pallas_call(
        _adaln_kernel,
        grid=(n // tn,),
        in_specs=[pl.BlockSpec((bsz, d), lambda j: (0, 0)),
                  pl.BlockSpec((d, tn), lambda j: (0, j)),
                  pl.BlockSpec((1, tn), lambda j: (0, j))],
        out_specs=pl.BlockSpec((bsz, tn), lambda j: (0, j)),
        out_shape=jax.ShapeDtypeStruct((bsz, n), F32),
        compiler_params=_params("arbitrary"),
        name="adaln",
    )(c, w, b.reshape(1, n))


def _inproj_kernel(x_ref, sh_ref, sc_ref, g_ref, wm_ref, wg_ref, wd_ref, pm_ref, pg_ref, pd_ref):
    x = x_ref[0]
    ms = jnp.mean(x * x, axis=-1, keepdims=True)
    h = x * lax.rsqrt(ms + EPS) * g_ref[...]
    h = h * (1.0 + sc_ref[0]) + sh_ref[0]
    hb = h.astype(BF16)
    pm_ref[0] = jnp.dot(hb, wm_ref[...], preferred_element_type=F32)
    pg_ref[0] = jnp.dot(hb, wg_ref[...], preferred_element_type=F32)
    pd_ref[0] = jnp.dot(hb, wd_ref[...], preferred_element_type=F32).astype(BF16)


def _inproj(x, sh, sc, g, wm, wg, wd, tm=512):
    bsz, s, d = x.shape
    tm = min(tm, s)
    nm, ng, nd = wm.shape[1], wg.shape[1], wd.shape[1]
    tok = lambda b, i: (b, i, 0)
    per_b = lambda b, i: (b, 0, 0)
    const = lambda b, i: (0, 0)
    return pl.pallas_call(
        _inproj_kernel,
        grid=(bsz, s // tm),
        in_specs=[pl.BlockSpec((1, tm, d), tok),
                  pl.BlockSpec((1, 1, d), per_b),
                  pl.BlockSpec((1, 1, d), per_b),
                  pl.BlockSpec((1, d), const),
                  pl.BlockSpec((d, nm), const),
                  pl.BlockSpec((d, ng), const),
                  pl.BlockSpec((d, nd), const)],
        out_specs=[pl.BlockSpec((1, tm, nm), tok),
                   pl.BlockSpec((1, tm, ng), tok),
                   pl.BlockSpec((1, tm, nd), tok)],
        out_shape=[jax.ShapeDtypeStruct((bsz, s, nm), F32),
                   jax.ShapeDtypeStruct((bsz, s, ng), F32),
                   jax.ShapeDtypeStruct((bsz, s, nd), BF16)],
        compiler_params=_params("arbitrary", "arbitrary"),
        name="inproj",
    )(x, sh, sc, g, wm, wg, wd)


def _causal_conv(u, tail, w, b):
    ext = jnp.concatenate([tail, u], axis=0)
    y = u * w[CONV_WIDTH - 1:CONV_WIDTH, :]
    for k in range(CONV_WIDTH - 1):
        shift = CONV_WIDTH - 1 - k
        y = y + pltpu.roll(ext, shift, axis=0)[SUBLANES:, :] * w[k:k + 1, :]
    return y + b


def _cumsum_lanes(x):
    lane = lax.broadcasted_iota(jnp.int32, x.shape, 1)
    s = 1
    while s < x.shape[1]:
        x = x + jnp.where(lane >= s, pltpu.roll(x, s, axis=1), 0.0)
        s *= 2
    return x


def _mlstm_kernel(gb_ref, mq_ref, mk_ref, mv_ref, mo_ref, g_ref, cwq_ref, cwk_ref, cbq_ref, cbk_ref,
                  ng_ref, o_ref, b_scr, i_scr, *, n_heads, head_dim, group):
    h0 = pl.program_id(1) * group
    L = M_CHUNK
    hd = head_dim
    nc = b_scr.shape[1]
    silu = lambda a: a * jax.nn.sigmoid(a)

    for u in range(group):
        i_scr[u] = g_ref[0, h0 + u] + gb_ref[h0 + u]
        fpre = g_ref[0, n_heads + h0 + u] + gb_ref[n_heads + h0 + u]
        lf = jnp.minimum(fpre, 0.0) - jnp.log1p(jnp.exp(-jnp.abs(fpre)))
        b_scr[u] = _cumsum_lanes(lf)

    r_i = lax.broadcasted_iota(jnp.int32, (L, L), 0)
    c_i = lax.broadcasted_iota(jnp.int32, (L, L), 1)
    eye = r_i == c_i
    causal = c_i <= r_i

    def to_col(row):
        return jnp.sum(jnp.where(eye, row, 0.0), axis=1, keepdims=True)

    def chunk(c, carry):
        t0 = pl.multiple_of(c * L, L)
        return tuple(head_chunk(c, t0, u, carry[u]) for u in range(group))

    def head_chunk(c, t0, u, carry):
        C, n, m, q_tail, k_tail = carry
        cols = slice(u * hd, (u + 1) * hd)
        q_raw = mq_ref[0, pl.ds(t0, L), cols]
        k_raw = mk_ref[0, pl.ds(t0, L), cols]
        qc = silu(_causal_conv(q_raw, q_tail, cwq_ref[:, cols], cbq_ref[:, cols])).astype(BF16)
        kc = (silu(_causal_conv(k_raw, k_tail, cwk_ref[:, cols], cbk_ref[:, cols])) * (hd ** -0.5)).astype(BF16)
        vc = mv_ref[0, pl.ds(t0, L), cols].astype(BF16)
        b_row = b_scr[u, pl.ds(c, 1), :]
        i_row = i_scr[u, pl.ds(c, 1), :]
        b_col = to_col(b_row)
        i_col = to_col(i_row)

        logD = jnp.where(causal, b_col - b_row + i_row, -jnp.inf)
        m_t = jnp.maximum(b_col + m, jnp.max(logD, axis=1, keepdims=True))
        Dw = jnp.exp(logD - m_t)
        inter = jnp.exp(b_col + m - m_t)
        sqk = lax.dot_general(qc, kc, NT, preferred_element_type=F32) * Dw
        num = (jnp.dot(sqk.astype(BF16), vc, preferred_element_type=F32)
               + inter * jnp.dot(qc, C.astype(BF16), preferred_element_type=F32))
        den = (jnp.sum(sqk, axis=1, keepdims=True)
               + inter * jnp.sum(qc.astype(F32) * n, axis=1, keepdims=True))
        hh = num / jnp.maximum(jnp.abs(den), jnp.exp(-m_t))

        bL = b_row[:, L - 1:L]
        m_new = jnp.maximum(bL + m, jnp.max(bL - b_row + i_row, axis=1, keepdims=True))
        w_col = jnp.exp(bL - b_col + i_col - m_new)
        decay = jnp.exp(bL + m - m_new)
        kw = kc.astype(F32) * w_col
        C_new = decay * C + lax.dot_general(kw.astype(BF16), vc, TN, preferred_element_type=F32)
        n_new = decay * n + jnp.sum(kw, axis=0, keepdims=True)

        y = hh * lax.rsqrt(jnp.mean(hh * hh, axis=-1, keepdims=True) + EPS) * ng_ref[:, cols]
        y = y * jax.nn.sigmoid(mo_ref[0, pl.ds(t0, L), cols])
        o_ref[0, pl.ds(t0, L), cols] = y.astype(o_ref.dtype)
        return C_new, n_new, m_new, q_raw[L - SUBLANES:, :], k_raw[L - SUBLANES:, :]

    zeros = lambda *shape: jnp.zeros(shape, F32)
    init = (zeros(hd, hd), zeros(1, hd), zeros(1, 1), zeros(SUBLANES, hd), zeros(SUBLANES, hd))
    lax.fori_loop(0, nc, chunk, (init,) * group)


def _mlstm(pm, gates, gate_b, conv_w, conv_b, norm_g, n_heads, head_dim, group=2):
    bsz, s, _ = pm.shape
    nc = s // M_CHUNK
    gw = group * head_dim
    n_groups = n_heads // group
    col = lambda off: (lambda b, g: (b, 0, off * n_groups + g))
    wcol = lambda off: (lambda b, g: (0, off * n_groups + g))
    kern = functools.partial(_mlstm_kernel, n_heads=n_heads, head_dim=head_dim, group=group)
    return pl.pallas_call(
        kern,
        grid=(bsz, n_groups),
        in_specs=[pl.BlockSpec(memory_space=pltpu.SMEM),
                  pl.BlockSpec((1, s, gw), col(0)),
                  pl.BlockSpec((1, s, gw), col(1)),
                  pl.BlockSpec((1, s, gw), col(2)),
                  pl.BlockSpec((1, s, gw), col(3)),
                  pl.BlockSpec((1, 2 * n_heads, nc, M_CHUNK), lambda b, g: (b, 0, 0, 0)),
                  pl.BlockSpec((CONV_WIDTH, gw), wcol(0)),
                  pl.BlockSpec((CONV_WIDTH, gw), wcol(1)),
                  pl.BlockSpec((1, gw), wcol(0)),
                  pl.BlockSpec((1, gw), wcol(1)),
                  pl.BlockSpec((1, gw), wcol(0))],
        out_specs=pl.BlockSpec((1, s, gw), col(0)),
        out_shape=jax.ShapeDtypeStruct((bsz, s, n_heads * head_dim), BF16),
        scratch_shapes=[pltpu.VMEM((group, nc, M_CHUNK), F32), pltpu.VMEM((group, nc, M_CHUNK), F32)],
        compiler_params=_params("arbitrary", "arbitrary"),
        name="mlstm",
    )(gate_b, pm, pm, pm, pm, gates, conv_w, conv_w, conv_b, conv_b, norm_g)


def _diffattn_kernel(q_ref, k_ref, v_ref, lam_ref, ng_ref, o_ref, m_scr, l_scr, acc_scr,
                     *, dh, lam_init, group):
    i = pl.program_id(2)
    tq = q_ref.shape[1]
    hw = 2 * dh
    scale = dh ** -0.5
    exact_scale = math.frexp(scale)[0] == 0.5
    lane = lax.broadcasted_iota(jnp.int32, (tq, hw), 1)
    qs = []
    for u in range(group):
        q = q_ref[0, :, u * hw:(u + 1) * hw]
        if exact_scale:
            q = q * scale
        zero = jnp.zeros_like(q)
        qs += [jnp.where(lane < dh, q, zero), jnp.where(lane < dh, zero, q)]
    lp = lam_ref[...]
    lam = (jnp.exp(jnp.sum(lp[0:1] * lp[1:2], axis=1, keepdims=True))
           - jnp.exp(jnp.sum(lp[2:3] * lp[3:4], axis=1, keepdims=True)) + lam_init)
    on_or_below_diag = (lax.broadcasted_iota(jnp.int32, (tq, tq), 1)
                        <= lax.broadcasted_iota(jnp.int32, (tq, tq), 0))

    m_scr[...] = jnp.full(m_scr.shape, -jnp.inf, F32)
    l_scr[...] = jnp.zeros(l_scr.shape, F32)
    acc_scr[...] = jnp.zeros(acc_scr.shape, F32)

    def block(j, diagonal):
        k0 = pl.multiple_of(j * tq, tq)
        for st in range(2 * group):
            cols = slice((st // 2) * hw, (st // 2 + 1) * hw)
            kb = k_ref[0, pl.ds(k0, tq), cols]
            vb = v_ref[0, pl.ds(k0, tq), cols]
            s = lax.dot_general(qs[st], kb, NT, preferred_element_type=F32)
            if not exact_scale:
                s = s * scale
            if diagonal:
                s = jnp.where(on_or_below_diag, s, -jnp.inf)
            m_old = m_scr[st]
            m_new = jnp.maximum(m_old, jnp.broadcast_to(jnp.max(s, axis=1, keepdims=True), m_old.shape))
            alpha = jnp.exp(m_old - m_new)
            e = [jnp.exp(s[:, c:c + LANES] - m_new) for c in range(0, tq, LANES)]
            l_scr[st] = alpha * l_scr[st] + functools.reduce(jnp.add, e)
            eb = jnp.concatenate([x.astype(BF16) for x in e], axis=1)
            acc_scr[st] = alpha * acc_scr[st] + jnp.dot(eb, vb, preferred_element_type=F32)
            m_scr[st] = m_new

    def below(j, carry):
        block(j, False)
        return carry

    lax.fori_loop(0, i, below, 0)
    block(i, True)
    for u in range(group):
        o1 = acc_scr[2 * u] / jnp.sum(l_scr[2 * u], axis=1, keepdims=True)
        o2 = acc_scr[2 * u + 1] / jnp.sum(l_scr[2 * u + 1], axis=1, keepdims=True)
        o = o1 - lam * o2
        y = o * lax.rsqrt(jnp.mean(o * o, axis=-1, keepdims=True) + EPS) * ng_ref[...] * (1.0 - lam_init)
        o_ref[0, :, u * hw:(u + 1) * hw] = y.astype(o_ref.dtype)


def _diffattn(pd, lam_p, norm_g, n_heads, dh, lam_init, tq=256, group=4):
    bsz, s, _ = pd.shape
    hw = 2 * dh
    gw = group * hw
    n_groups = n_heads // group
    kern = functools.partial(_diffattn_kernel, dh=dh, lam_init=lam_init, group=group)
    return pl.pallas_call(
        kern,
        grid=(bsz, n_groups, s // tq),
        in_specs=[pl.BlockSpec((1, tq, gw), lambda b, g, i: (b, i, g)),
                  pl.BlockSpec((1, s, gw), lambda b, g, i: (b, 0, n_groups + g)),
                  pl.BlockSpec((1, s, gw), lambda b, g, i: (b, 0, 2 * n_groups + g)),
                  pl.BlockSpec((4, dh), lambda b, g, i: (0, 0)),
                  pl.BlockSpec((1, hw), lambda b, g, i: (0, 0))],
        out_specs=pl.BlockSpec((1, tq, gw), lambda b, g, i: (b, i, g)),
        out_shape=jax.ShapeDtypeStruct((bsz, s, n_heads * hw), BF16),
        scratch_shapes=[pltpu.VMEM((2 * group, tq, LANES), F32), pltpu.VMEM((2 * group, tq, LANES), F32),
                        pltpu.VMEM((2 * group, tq, hw), F32)],
        compiler_params=_params("arbitrary", "arbitrary", "arbitrary"),
        name="diffattn",
    )(pd, pd, pd, lam_p, norm_g)


def _outproj_kernel(hm_ref, od_ref, x_ref, g1_ref, sh_ref, sc_ref, ng_ref, wm_ref, wd_ref, wq_ref,
                    x1_ref, h2_ref, q_ref):
    y = (jnp.dot(hm_ref[0], wm_ref[...], preferred_element_type=F32)
         + jnp.dot(od_ref[0], wd_ref[...], preferred_element_type=F32))
    x1 = x_ref[0] + g1_ref[0] * y
    x1_ref[0] = x1
    ms = jnp.mean(x1 * x1, axis=-1, keepdims=True)
    h2 = x1 * lax.rsqrt(ms + EPS) * ng_ref[...]
    h2 = (h2 * (1.0 + sc_ref[0]) + sh_ref[0]).astype(BF16)
    h2_ref[0] = h2
    q_ref[0] = jnp.dot(h2, wq_ref[...], preferred_element_type=F32).astype(BF16)


def _outproj(hm, od, x, g1, sh2, sc2, ng, wm, wd, wq, tm=512):
    bsz, s, d = x.shape
    tm = min(tm, s)
    wm_w, wd_w, nq = hm.shape[2], od.shape[2], wq.shape[1]
    tok = lambda b, i: (b, i, 0)
    per_b = lambda b, i: (b, 0, 0)
    const = lambda b, i: (0, 0)
    return pl.pallas_call(
        _outproj_kernel,
        grid=(bsz, s // tm),
        in_specs=[pl.BlockSpec((1, tm, wm_w), tok),
                  pl.BlockSpec((1, tm, wd_w), tok),
                  pl.BlockSpec((1, tm, d), tok),
                  pl.BlockSpec((1, 1, d), per_b),
                  pl.BlockSpec((1, 1, d), per_b),
                  pl.BlockSpec((1, 1, d), per_b),
                  pl.BlockSpec((1, d), const),
                  pl.BlockSpec((wm_w, d), const),
                  pl.BlockSpec((wd_w, d), const),
                  pl.BlockSpec((d, nq), const)],
        out_specs=[pl.BlockSpec((1, tm, d), tok),
                   pl.BlockSpec((1, tm, d), tok),
                   pl.BlockSpec((1, tm, nq), tok)],
        out_shape=[jax.ShapeDtypeStruct((bsz, s, d), F32),
                   jax.ShapeDtypeStruct((bsz, s, d), BF16),
                   jax.ShapeDtypeStruct((bsz, s, nq), BF16)],
        compiler_params=_params("arbitrary", "arbitrary"),
        name="outproj",
    )(hm, od, x, g1, sh2, sc2, ng, wm, wd, wq)


def _topk_rows(s, k, payload=None):
    n_rows, t = s.shape
    nv = n_rows // SUBLANES
    slab = lambda a, v: a[SUBLANES * v:SUBLANES * (v + 1), :]
    slabs = [slab(s, v) for v in range(nv)]
    sub = lax.broadcasted_iota(jnp.int32, (SUBLANES, t), 0)
    out_row = lax.broadcasted_iota(jnp.int32, (k, t), 0)
    vals = jnp.zeros((k, t), s.dtype)
    rows = jnp.zeros((k, t), jnp.int32)
    pays = None if payload is None else jnp.zeros((k, t), payload.dtype)
    for r in range(k):
        nodes = [(slabs[v], v) for v in range(nv)]
        while len(nodes) > 1:
            nxt = []
            for a in range(0, len(nodes) - 1, 2):
                (va, ia), (vb, ib) = nodes[a], nodes[a + 1]
                c = va >= vb
                nxt.append((jnp.where(c, va, vb), jnp.where(c, ia, ib)))
            if len(nodes) % 2:
                nxt.append(nodes[-1])
            nodes = nxt
        m8, v8 = nodes[0]
        m = jnp.max(m8, axis=0, keepdims=True)
        idx = jnp.min(jnp.where(m8 == m, v8 * SUBLANES + sub, n_rows), axis=0, keepdims=True)
        hits = [sub == idx - SUBLANES * v for v in range(nv)]
        vals = jnp.where(out_row == r, m, vals)
        rows = jnp.where(out_row == r, idx, rows)
        if payload is not None:
            picked = [jnp.where(hits[v], slab(payload, v), 0) for v in range(nv)]
            p = jnp.sum(functools.reduce(jnp.add, picked), axis=0, keepdims=True)
            pays = jnp.where(out_row == r, p, pays)
        slabs = [jnp.where(hits[v], -jnp.inf, slabs[v]) for v in range(nv)]
    return vals, rows, pays


def _candidate_layout(k):
    slabs, cur = [], []

    def flush():
        n_valid = len(cur)
        while len(cur) < SUBLANES:
            r = len(cur)
            period = 1
            while period < max(b for _, b in cur[:n_valid]) + 1:
                period *= 2
            b = cur[r - period][1] if r >= period else r
            cur.append((cur[n_valid - 1][0], b))
        slabs.append(([a for a, _ in cur], [b for _, b in cur], n_valid))
        cur.clear()

    for a in range(k):
        nb = k // (a + 1)
        for b0 in range(0, nb, SUBLANES):
            group = [(a, b) for b in range(b0, min(nb, b0 + SUBLANES))]
            if cur and (len(cur) + len(group) > SUBLANES or (nb == 1 and a % SUBLANES == 0)):
                flush()
            cur.extend(group)
    if cur:
        flush()
    return slabs


def _rows_by_pattern(x, pattern, sub):
    p0 = pattern[0]
    if p0 % SUBLANES == 0 and pattern == list(range(p0, p0 + SUBLANES)):
        return x[p0:p0 + SUBLANES, :]
    period = SUBLANES
    while period > 1 and all(pattern[r] == pattern[r % (period // 2)] for r in range(SUBLANES)):
        period //= 2
    pos = sub if period == SUBLANES else sub & (period - 1)
    out = x[p0:p0 + 1, :]
    for r in range(1, period):
        if pattern[r] != pattern[r - 1]:
            out = jnp.where(pos >= r, x[pattern[r]:pattern[r] + 1, :], out)
    return jnp.broadcast_to(out, (SUBLANES, x.shape[1]))


def _route_kernel(q_ref, keys_ref, i_ref, j_ref, g_ref, sv_scr, si_scr, it_scr, jt_scr, gt_scr,
                  *, n_heads, n_keys):
    K = P_TOPK
    t = q_ref.shape[0]
    sub = lax.broadcasted_iota(jnp.int32, (SUBLANES, t), 0)
    layout = _candidate_layout(K)

    def stage1(h, _):
        for p in range(2):
            hp = 2 * h + p
            c0 = pl.multiple_of(hp * LANES, LANES)
            qb = q_ref[:, pl.ds(c0, LANES)]
            st = lax.dot_general(keys_ref[hp], qb, NT, preferred_element_type=F32)
            v, ix, _ = _topk_rows(st, K)
            sv_scr[hp] = v
            si_scr[hp] = ix
        return 0

    lax.fori_loop(0, n_heads, stage1, 0)

    def stage2(hh, _):
        for u in range(2):
            h = 2 * hh + u
            sv0, sv1 = sv_scr[2 * h], sv_scr[2 * h + 1]
            si0, si1 = si_scr[2 * h], si_scr[2 * h + 1]
            cand, cidx = [], []
            for a_pat, b_pat, n_valid in layout:
                c = _rows_by_pattern(sv0, a_pat, sub) + _rows_by_pattern(sv1, b_pat, sub)
                if n_valid < SUBLANES:
                    c = jnp.where(sub < n_valid, c, -jnp.inf)
                cand.append(c)
                cidx.append(_rows_by_pattern(si0, a_pat, sub) * n_keys + _rows_by_pattern(si1, b_pat, sub))
            fv, _, eidx = _topk_rows(jnp.concatenate(cand, axis=0), K, jnp.concatenate(cidx, axis=0))
            e = jnp.exp(fv - fv[0:1, :])
            g = e / jnp.sum(e, axis=0, keepdims=True)
            r0 = pl.multiple_of(h * K, K)
            it_scr[pl.ds(r0, K), :] = eidx // n_keys
            jt_scr[pl.ds(r0, K), :] = eidx % n_keys
            gt_scr[pl.ds(r0, K), :] = g
        return 0

    lax.fori_loop(0, n_heads // 2, stage2, 0)
    i_ref[...] = it_scr[...].T
    j_ref[...] = jt_scr[...].T
    g_ref[...] = gt_scr[...].T


def _route(q, keys, n_heads, n_keys):
    n = q.shape[0]
    t = LANES
    slots = n_heads * P_TOPK
    kern = functools.partial(_route_kernel, n_heads=n_heads, n_keys=n_keys)
    out = jax.ShapeDtypeStruct((n, slots), jnp.int32)
    return pl.pallas_call(
        kern,
        grid=(n // t,),
        in_specs=[pl.BlockSpec((t, q.shape[1]), lambda i: (i, 0)),
                  pl.BlockSpec(keys.shape, lambda i: (0, 0, 0))],
        out_specs=[pl.BlockSpec((t, slots), lambda i: (i, 0))] * 3,
        out_shape=[out, out, jax.ShapeDtypeStruct((n, slots), F32)],
        scratch_shapes=[pltpu.VMEM((2 * n_heads, P_TOPK, t), F32),
                        pltpu.VMEM((2 * n_heads, P_TOPK, t), jnp.int32),
                        pltpu.VMEM((slots, t), jnp.int32),
                        pltpu.VMEM((slots, t), jnp.int32),
                        pltpu.VMEM((slots, t), F32)],
        compiler_params=_params("arbitrary"),
        name="route",
    )(q, keys)


GATE_UNROLL = 16


def _gatemat_kernel(i_ref, j_ref, g_ref, o_ref, w_scr, *, n_keys, stride):
    tb, slots = i_ref.shape
    sub = lax.broadcasted_iota(jnp.int32, (n_keys, slots), 0)

    def tokens(k, _):
        t0 = pl.multiple_of(k * GATE_UNROLL, GATE_UNROLL)
        for u in range(GATE_UNROLL):
            t = t0 + u
            i_row = i_ref[pl.ds(t, 1), :]
            j_row = j_ref[pl.ds(t, 1), :]
            g_row = g_ref[pl.ds(t, 1), :]
            a_t = jnp.where(sub == i_row, 1.0, 0.0).astype(BF16)
            b_t = jnp.where(sub == j_row, g_row, 0.0).astype(BF16)
            m_t = lax.dot_general(a_t, b_t, NT, preferred_element_type=F32)
            w_scr[pl.ds(t, n_keys, stride=stride), :] = m_t
        return 0

    lax.fori_loop(0, tb // GATE_UNROLL, tokens, 0)

    def blocks(i, _):
        r0 = pl.multiple_of(i * stride, SUBLANES)
        o_ref[i] = w_scr[pl.ds(r0, tb), :].astype(o_ref.dtype)
        return 0

    lax.fori_loop(0, n_keys, blocks, 0)


def _gatemat(islot, jslot, gate, n_keys, tb=128):
    n, slots = islot.shape
    stride = tb + SUBLANES
    kern = functools.partial(_gatemat_kernel, n_keys=n_keys, stride=stride)
    return pl.pallas_call(
        kern,
        grid=(n // tb,),
        in_specs=[pl.BlockSpec((tb, slots), lambda i: (i, 0))] * 3,
        out_specs=pl.BlockSpec((n_keys, tb, n_keys), lambda i: (0, i, 0)),
        out_shape=jax.ShapeDtypeStruct((n_keys, n, n_keys), BF16),
        scratch_shapes=[pltpu.VMEM((n_keys * stride, n_keys), F32)],
        compiler_params=_params("arbitrary"),
        name="gatemat",
    )(islot, jslot, gate)


def _experts_kernel(h2_ref, u_ref, v_ref, m_ref, x1_ref, g2_ref, fg_ref, o_ref, acc_ref, *, final):
    j = pl.program_id(2)

    @pl.when(j == 0)
    def _():
        acc_ref[...] = jnp.zeros_like(acc_ref)

    s = lax.dot_general(h2_ref[0], u_ref[...], NT, preferred_element_type=F32)
    act = 0.5 * s * (1.0 + lax.erf(s * (2.0 ** -0.5)))
    gate = jnp.concatenate([m_ref[ib] for ib in range(m_ref.shape[0])], axis=1)
    w = (gate.astype(F32) * act).astype(BF16)
    acc_ref[...] += jnp.dot(w, v_ref[...], preferred_element_type=F32)

    @pl.when(j == pl.num_programs(2) - 1)
    def _():
        x2 = x1_ref[0] + g2_ref[0] * acc_ref[...]
        if final:
            x2 = x2 * lax.rsqrt(jnp.mean(x2 * x2, axis=-1, keepdims=True) + EPS) * fg_ref[...]
        o_ref[0] = x2


def _experts(h2, u, v, gmat, x1, g2, final_g, final, tm=1024, te=512):
    bsz, s, d = x1.shape
    tm = min(tm, s)
    ne = u.shape[0]
    n_keys = gmat.shape[2]
    n_tiles = s // tm
    kern = functools.partial(_experts_kernel, final=final)
    return pl.pallas_call(
        kern,
        grid=(bsz, n_tiles, ne // te),
        in_specs=[pl.BlockSpec((1, tm, d), lambda b, i, j: (b, i, 0)),
                  pl.BlockSpec((te, d), lambda b, i, j: (j, 0)),
                  pl.BlockSpec((te, d), lambda b, i, j: (j, 0)),
                  pl.BlockSpec((te // n_keys, tm, n_keys), lambda b, i, j: (j, b * n_tiles + i, 0)),
                  pl.BlockSpec((1, tm, d), lambda b, i, j: (b, i, 0)),
                  pl.BlockSpec((1, 1, d), lambda b, i, j: (b, 0, 0)),
                  pl.BlockSpec((1, d), lambda b, i, j: (0, 0))],
        out_specs=pl.BlockSpec((1, tm, d), lambda b, i, j: (b, i, 0)),
        out_shape=jax.ShapeDtypeStruct((bsz, s, d), F32),
        scratch_shapes=[pltpu.VMEM((tm, d), F32)],
        compiler_params=_params("arbitrary", "arbitrary", "arbitrary"),
        name="experts",
    )(h2, u, v, gmat, x1, g2, final_g)


def kernel(x, c, ada_w, ada_b, norm1_g, w_in, conv_w, conv_b, mlstm_gate_b, mlstm_norm_g, lambda_q1,
           lambda_k1, lambda_q2, lambda_k2, diff_norm_g, w_out, norm2_g, peer_w_query, peer_sub_keys,
           peer_u, peer_v, final_g):
    bsz, s, d = x.shape
    depth = ada_w.shape[0]
    m_heads = mlstm_gate_b.shape[1] // 2
    m_width = mlstm_norm_g.shape[1]
    m_hd = m_width // m_heads
    dh = lambda_q1.shape[1]
    d_width = w_out.shape[1] - m_width
    d_heads = d_width // (2 * dh)
    p_heads, _, n_keys, _ = peer_sub_keys.shape[1:]
    n_m = 4 * m_width
    n_g = 2 * m_heads
    assert s % M_CHUNK == 0 and m_hd == LANES and 2 * dh == LANES and n_keys == LANES
    assert peer_sub_keys.shape[-1] == LANES and p_heads % 2 == 0

    for l in range(depth):
        mod = _adaln(c, ada_w[l], ada_b[l])
        sh1, sc1, g1, sh2, sc2, g2 = (a.reshape(bsz, 1, d) for a in jnp.split(mod, 6, axis=-1))

        w = w_in[l].astype(BF16)
        w_m = w[:, :n_m]
        w_g = jnp.pad(w[:, n_m:n_m + n_g], ((0, 0), (0, LANES - n_g)))
        w_d = w[:, n_m + n_g:]
        pm, pg, pd = _inproj(x, sh1, sc1, norm1_g[l].reshape(1, d), w_m, w_g, w_d)

        gates = pg[:, :, :n_g].transpose(0, 2, 1).reshape(bsz, n_g, s // M_CHUNK, M_CHUNK)
        hm = _mlstm(pm, gates, mlstm_gate_b[l], conv_w[l], conv_b[l].reshape(1, -1),
                    mlstm_norm_g[l].reshape(1, -1), m_heads, m_hd)

        lam_init = 0.8 - 0.6 * math.exp(-0.3 * l)
        lam_p = jnp.stack([lambda_q1[l], lambda_k1[l], lambda_q2[l], lambda_k2[l]])
        od = _diffattn(pd, lam_p, diff_norm_g[l].reshape(1, -1), d_heads, dh, lam_init)

        wo = w_out[l].astype(BF16)
        x1, h2, q = _outproj(hm, od, x, g1, sh2, sc2, norm2_g[l].reshape(1, d), wo[:m_width], wo[m_width:],
                             peer_w_query[l].astype(BF16))

        keys = peer_sub_keys[l].reshape(2 * p_heads, n_keys, -1).astype(BF16)
        islot, jslot, gate = _route(q.reshape(bsz * s, -1), keys, p_heads, n_keys)
        gmat = _gatemat(islot, jslot, gate, n_keys)
        x = _experts(h2, peer_u[l].astype(BF16), peer_v[l].astype(BF16), gmat, x1, g2,
                     final_g.reshape(1, d), final=(l == depth - 1))
    return x
```

```python
import functools
import math

import jax
import jax.numpy as jnp
from jax import lax
from jax.experimental import pallas as pl
from jax.experimental.pallas import tpu as pltpu

F32 = jnp.float32
BF16 = jnp.bfloat16
EPS = 1e-6
LANES = 128
SUBLANES = 8
VMEM_LIMIT = 56 * 1024 * 1024
M_CHUNK = 128
CONV_WIDTH = 4
P_TOPK = 16

NT = (((1,), (1,)), ((), ()))
TN = (((0,), (0,)), ((), ()))


def _params(*sem):
    return pltpu.CompilerParams(dimension_semantics=sem, vmem_limit_bytes=VMEM_LIMIT)


def _adaln_kernel(c_ref, w_ref, b_ref, o_ref):
    c = c_ref[...]
    sc = (c * jax.nn.sigmoid(c)).astype(BF16)
    o_ref[...] = jnp.dot(sc, w_ref[...].astype(BF16), preferred_element_type=F32) + b_ref[...]


def _adaln(c, w, b):
    bsz, d = c.shape
    n = w.shape[1]
    tn = 1536
    return pl.pallas_call(
        _adaln_kernel,
        grid=(n // tn,),
        in_specs=[pl.BlockSpec((bsz, d), lambda j: (0, 0)),
                  pl.BlockSpec((d, tn), lambda j: (0, j)),
                  pl.BlockSpec((1, tn), lambda j: (0, j))],
        out_specs=pl.BlockSpec((bsz, tn), lambda j: (0, j)),
        out_shape=jax.ShapeDtypeStruct((bsz, n), F32),
        compiler_params=_params("arbitrary"),
        name="adaln",
    )(c, w, b.reshape(1, n))


def _inproj_kernel(x_ref, sh_ref, sc_ref, g_ref, wm_ref, wg_ref, wd_ref, pm_ref, pg_ref, pd_ref):
    x = x_ref[0]
    ms = jnp.mean(x * x, axis=-1, keepdims=True)
    h = x * lax.rsqrt(ms + EPS) * g_ref[...]
    h = h * (1.0 + sc_ref[0]) + sh_ref[0]
    hb = h.astype(BF16)
    pm_ref[0] = jnp.dot(hb, wm_ref[...], preferred_element_type=F32)
    pg_ref[0] = jnp.dot(hb, wg_ref[...], preferred_element_type=F32)
    pd_ref[0] = jnp.dot(hb, wd_ref[...], preferred_element_type=F32).astype(BF16)


def _inproj(x, sh, sc, g, wm, wg, wd, tm=512):
    bsz, s, d = x.shape
    tm = min(tm, s)
    nm, ng, nd = wm.shape[1], wg.shape[1], wd.shape[1]
    tok = lambda b, i: (b, i, 0)
    per_b = lambda b, i: (b, 0, 0)
    const = lambda b, i: (0, 0)
    return pl.pallas_call(
        _inproj_kernel,
        grid=(bsz, s // tm),
        in_specs=[pl.BlockSpec((1, tm, d), tok),
                  pl.BlockSpec((1, 1, d), per_b),
                  pl.BlockSpec((1, 1, d), per_b),
                  pl.BlockSpec((1, d), const),
                  pl.BlockSpec((d, nm), const),
                  pl.BlockSpec((d, ng), const),
                  pl.BlockSpec((d, nd), const)],
        out_specs=[pl.BlockSpec((1, tm, nm), tok),
                   pl.BlockSpec((1, tm, ng), tok),
                   pl.BlockSpec((1, tm, nd), tok)],
        out_shape=[jax.ShapeDtypeStruct((bsz, s, nm), F32),
                   jax.ShapeDtypeStruct((bsz, s, ng), F32),
                   jax.ShapeDtypeStruct((bsz, s, nd), BF16)],
        compiler_params=_params("arbitrary", "arbitrary"),
        name="inproj",
    )(x, sh, sc, g, wm, wg, wd)


def _causal_conv(u, tail, w, b):
    ext = jnp.concatenate([tail, u], axis=0)
    y = u * w[CONV_WIDTH - 1:CONV_WIDTH, :]
    for k in range(CONV_WIDTH - 1):
        shift = CONV_WIDTH - 1 - k
        y = y + pltpu.roll(ext, shift, axis=0)[SUBLANES:, :] * w[k:k + 1, :]
    return y + b


def _cumsum_lanes(x):
    lane = lax.broadcasted_iota(jnp.int32, x.shape, 1)
    s = 1
    while s < x.shape[1]:
        x = x + jnp.where(lane >= s, pltpu.roll(x, s, axis=1), 0.0)
        s *= 2
    return x


def _mlstm_kernel(gb_ref, mq_ref, mk_ref, mv_ref, mo_ref, g_ref, cwq_ref, cwk_ref, cbq_ref, cbk_ref,
                  ng_ref, o_ref, b_scr, i_scr, *, n_heads, head_dim, group):
    h0 = pl.program_id(1) * group
    L = M_CHUNK
    hd = head_dim
    nc = b_scr.shape[1]
    silu = lambda a: a * jax.nn.sigmoid(a)

    for u in range(group):
        i_scr[u] = g_ref[0, h0 + u] + gb_ref[h0 + u]
        fpre = g_ref[0, n_heads + h0 + u] + gb_ref[n_heads + h0 + u]
        lf = jnp.minimum(fpre, 0.0) - jnp.log1p(jnp.exp(-jnp.abs(fpre)))
        b_scr[u] = _cumsum_lanes(lf)

    r_i = lax.broadcasted_iota(jnp.int32, (L, L), 0)
    c_i = lax.broadcasted_iota(jnp.int32, (L, L), 1)
    eye = r_i == c_i
    causal = c_i <= r_i

    def to_col(row):
        return jnp.sum(jnp.where(eye, row, 0.0), axis=1, keepdims=True)

    def chunk(c, carry):
        t0 = pl.multiple_of(c * L, L)
        return tuple(head_chunk(c, t0, u, carry[u]) for u in range(group))

    def head_chunk(c, t0, u, carry):
        C, n, m, q_tail, k_tail = carry
        cols = slice(u * hd, (u + 1) * hd)
        q_raw = mq_ref[0, pl.ds(t0, L), cols]
        k_raw = mk_ref[0, pl.ds(t0, L), cols]
        qc = silu(_causal_conv(q_raw, q_tail, cwq_ref[:, cols], cbq_ref[:, cols])).astype(BF16)
        kc = (silu(_causal_conv(k_raw, k_tail, cwk_ref[:, cols], cbk_ref[:, cols])) * (hd ** -0.5)).astype(BF16)
        vc = mv_ref[0, pl.ds(t0, L), cols].astype(BF16)
        b_row = b_scr[u, pl.ds(c, 1), :]
        i_row = i_scr[u, pl.ds(c, 1), :]
        b_col = to_col(b_row)
        i_col = to_col(i_row)

        logD = jnp.where(causal, b_col - b_row + i_row, -jnp.inf)
        m_t = jnp.maximum(b_col + m, jnp.max(logD, axis=1, keepdims=True))
        Dw = jnp.exp(logD - m_t)
        inter = jnp.exp(b_col + m - m_t)
        sqk = lax.dot_general(qc, kc, NT, preferred_element_type=F32) * Dw
        num = (jnp.dot(sqk.astype(BF16), vc, preferred_element_type=F32)
               + inter * jnp.dot(qc, C.astype(BF16), preferred_element_type=F32))
        den = (jnp.sum(sqk, axis=1, keepdims=True)
               + inter * jnp.sum(qc.astype(F32) * n, axis=1, keepdims=True))
        hh = num / jnp.maximum(jnp.abs(den), jnp.exp(-m_t))

        bL = b_row[:, L - 1:L]
        m_new = jnp.maximum(bL + m, jnp.max(bL - b_row + i_row, axis=1, keepdims=True))
        w_col = jnp.exp(bL - b_col + i_col - m_new)
        decay = jnp.exp(bL + m - m_new)
        kw = kc.astype(F32) * w_col
        C_new = decay * C + lax.dot_general(kw.astype(BF16), vc, TN, preferred_element_type=F32)
        n_new = decay * n + jnp.sum(kw, axis=0, keepdims=True)

        y = hh * lax.rsqrt(jnp.mean(hh * hh, axis=-1, keepdims=True) + EPS) * ng_ref[:, cols]
        y = y * jax.nn.sigmoid(mo_ref[0, pl.ds(t0, L), cols])
        o_ref[0, pl.ds(t0, L), cols] = y.astype(o_ref.dtype)
        return C_new, n_new, m_new, q_raw[L - SUBLANES:, :], k_raw[L - SUBLANES:, :]

    zeros = lambda *shape: jnp.zeros(shape, F32)
    init = (zeros(hd, hd), zeros(1, hd), zeros(1, 1), zeros(SUBLANES, hd), zeros(SUBLANES, hd))
    lax.fori_loop(0, nc, chunk, (init,) * group)


def _mlstm(pm, gates, gate_b, conv_w, conv_b, norm_g, n_heads, head_dim, group=2):
    bsz, s, _ = pm.shape
    nc = s // M_CHUNK
    gw = group * head_dim
    n_groups = n_heads // group
    col = lambda off: (lambda b, g: (b, 0, off * n_groups + g))
    wcol = lambda off: (lambda b, g: (0, off * n_groups + g))
    kern = functools.partial(_mlstm_kernel, n_heads=n_heads, head_dim=head_dim, group=group)
    return pl.pallas_call(
        kern,
        grid=(bsz, n_groups),
        in_specs=[pl.BlockSpec(memory_space=pltpu.SMEM),
                  pl.BlockSpec((1, s, gw), col(0)),
                  pl.BlockSpec((1, s, gw), col(1)),
                  pl.BlockSpec((1, s, gw), col(2)),
                  pl.BlockSpec((1, s, gw), col(3)),
                  pl.BlockSpec((1, 2 * n_heads, nc, M_CHUNK), lambda b, g: (b, 0, 0, 0)),
                  pl.BlockSpec((CONV_WIDTH, gw), wcol(0)),
                  pl.BlockSpec((CONV_WIDTH, gw), wcol(1)),
                  pl.BlockSpec((1, gw), wcol(0)),
                  pl.BlockSpec((1, gw), wcol(1)),
                  pl.BlockSpec((1, gw), wcol(0))],
        out_specs=pl.BlockSpec((1, s, gw), col(0)),
        out_shape=jax.ShapeDtypeStruct((bsz, s, n_heads * head_dim), BF16),
        scratch_shapes=[pltpu.VMEM((group, nc, M_CHUNK), F32), pltpu.VMEM((group, nc, M_CHUNK), F32)],
        compiler_params=_params("arbitrary", "arbitrary"),
        name="mlstm",
    )(gate_b, pm, pm, pm, pm, gates, conv_w, conv_w, conv_b, conv_b, norm_g)


def _diffattn_kernel(q_ref, k_ref, v_ref, lam_ref, ng_ref, o_ref, m_scr, l_scr, acc_scr,
                     *, dh, lam_init, group):
    i = pl.program_id(2)
    tq = q_ref.shape[1]
    hw = 2 * dh
    scale = dh ** -0.5
    exact_scale = math.frexp(scale)[0] == 0.5
    lane = lax.broadcasted_iota(jnp.int32, (tq, hw), 1)
    qs = []
    for u in range(group):
        q = q_ref[0, :, u * hw:(u + 1) * hw]
        if exact_scale:
            q = q * scale
        zero = jnp.zeros_like(q)
        qs += [jnp.where(lane < dh, q, zero), jnp.where(lane < dh, zero, q)]
    lp = lam_ref[...]
    lam = (jnp.exp(jnp.sum(lp[0:1] * lp[1:2], axis=1, keepdims=True))
           - jnp.exp(jnp.sum(lp[2:3] * lp[3:4], axis=1, keepdims=True)) + lam_init)
    on_or_below_diag = (lax.broadcasted_iota(jnp.int32, (tq, tq), 1)
                        <= lax.broadcasted_iota(jnp.int32, (tq, tq), 0))

    m_scr[...] = jnp.full(m_scr.shape, -jnp.inf, F32)
    l_scr[...] = jnp.zeros(l_scr.shape, F32)
    acc_scr[...] = jnp.zeros(acc_scr.shape, F32)

    def block(j, diagonal):
        k0 = pl.multiple_of(j * tq, tq)
        for st in range(2 * group):
            cols = slice((st // 2) * hw, (st // 2 + 1) * hw)
            kb = k_ref[0, pl.ds(k0, tq), cols]
            vb = v_ref[0, pl.ds(k0, tq), cols]
            s = lax.dot_general(qs[st], kb, NT, preferred_element_type=F32)
            if not exact_scale:
                s = s * scale
            if diagonal:
                s = jnp.where(on_or_below_diag, s, -jnp.inf)
            m_old = m_scr[st]
            m_new = jnp.maximum(m_old, jnp.broadcast_to(jnp.max(s, axis=1, keepdims=True), m_old.shape))
            alpha = jnp.exp(m_old - m_new)
            e = [jnp.exp(s[:, c:c + LANES] - m_new) for c in range(0, tq, LANES)]
            l_scr[st] = alpha * l_scr[st] + functools.reduce(jnp.add, e)
            eb = jnp.concatenate([x.astype(BF16) for x in e], axis=1)
            acc_scr[st] = alpha * acc_scr[st] + jnp.dot(eb, vb, preferred_element_type=F32)
            m_scr[st] = m_new

    def below(j, carry):
        block(j, False)
        return carry

    lax.fori_loop(0, i, below, 0)
    block(i, True)
    for u in range(group):
        o1 = acc_scr[2 * u] / jnp.sum(l_scr[2 * u], axis=1, keepdims=True)
        o2 = acc_scr[2 * u + 1] / jnp.sum(l_scr[2 * u + 1], axis=1, keepdims=True)
        o = o1 - lam * o2
        y = o * lax.rsqrt(jnp.mean(o * o, axis=-1, keepdims=True) + EPS) * ng_ref[...] * (1.0 - lam_init)
        o_ref[0, :, u * hw:(u + 1) * hw] = y.astype(o_ref.dtype)


def _diffattn(pd, lam_p, norm_g, n_heads, dh, lam_init, tq=256, group=4):
    bsz, s, _ = pd.shape
    hw = 2 * dh
    gw = group * hw
    n_groups = n_heads // group
    kern = functools.partial(_diffattn_kernel, dh=dh, lam_init=lam_init, group=group)
    return pl.pallas_call(
        kern,
        grid=(bsz, n_groups, s // tq),
        in_specs=[pl.BlockSpec((1, tq, gw), lambda b, g, i: (b, i, g)),
                  pl.BlockSpec((1, s, gw), lambda b, g, i: (b, 0, n_groups + g)),
                  pl.BlockSpec((1, s, gw), lambda b, g, i: (b, 0, 2 * n_groups + g)),
                  pl.BlockSpec((4, dh), lambda b, g, i: (0, 0)),
                  pl.BlockSpec((1, hw), lambda b, g, i: (0, 0))],
        out_specs=pl.BlockSpec((1, tq, gw), lambda b, g, i: (b, i, g)),
        out_shape=jax.ShapeDtypeStruct((bsz, s, n_heads * hw), BF16),
        scratch_shapes=[pltpu.VMEM((2 * group, tq, LANES), F32), pltpu.VMEM((2 * group, tq, LANES), F32),
                        pltpu.VMEM((2 * group, tq, hw), F32)],
        compiler_params=_params("arbitrary", "arbitrary", "arbitrary"),
        name="diffattn",
    )(pd, pd, pd, lam_p, norm_g)


def _outproj_kernel(hm_ref, od_ref, x_ref, g1_ref, sh_ref, sc_ref, ng_ref, wm_ref, wd_ref, wq_ref,
                    x1_ref, h2_ref, q_ref):
    y = (jnp.dot(hm_ref[0], wm_ref[...], preferred_element_type=F32)
         + jnp.dot(od_ref[0], wd_ref[...], preferred_element_type=F32))
    x1 = x_ref[0] + g1_ref[0] * y
    x1_ref[0] = x1
    ms = jnp.mean(x1 * x1, axis=-1, keepdims=True)
    h2 = x1 * lax.rsqrt(ms + EPS) * ng_ref[...]
    h2 = (h2 * (1.0 + sc_ref[0]) + sh_ref[0]).astype(BF16)
    h2_ref[0] = h2
    q_ref[0] = jnp.dot(h2, wq_ref[...], preferred_element_type=F32).astype(BF16)


def _outproj(hm, od, x, g1, sh2, sc2, ng, wm, wd, wq, tm=512):
    bsz, s, d = x.shape
    tm = min(tm, s)
    wm_w, wd_w, nq = hm.shape[2], od.shape[2], wq.shape[1]
    tok = lambda b, i: (b, i, 0)
    per_b = lambda b, i: (b, 0, 0)
    const = lambda b, i: (0, 0)
    return pl.pallas_call(
        _outproj_kernel,
        grid=(bsz, s // tm),
        in_specs=[pl.BlockSpec((1, tm, wm_w), tok),
                  pl.BlockSpec((1, tm, wd_w), tok),
                  pl.BlockSpec((1, tm, d), tok),
                  pl.BlockSpec((1, 1, d), per_b),
                  pl.BlockSpec((1, 1, d), per_b),
                  pl.BlockSpec((1, 1, d), per_b),
                  pl.BlockSpec((1, d), const),
                  pl.BlockSpec((wm_w, d), const),
                  pl.BlockSpec((wd_w, d), const),
                  pl.BlockSpec((d, nq), const)],
        out_specs=[pl.BlockSpec((1, tm, d), tok),
                   pl.BlockSpec((1, tm, d), tok),
                   pl.BlockSpec((1, tm, nq), tok)],
        out_shape=[jax.ShapeDtypeStruct((bsz, s, d), F32),
                   jax.ShapeDtypeStruct((bsz, s, d), BF16),
                   jax.ShapeDtypeStruct((bsz, s, nq), BF16)],
        compiler_params=_params("arbitrary", "arbitrary"),
        name="outproj",
    )(hm, od, x, g1, sh2, sc2, ng, wm, wd, wq)


def _topk_rows(s, k, payload=None):
    n_rows, t = s.shape
    nv = n_rows // SUBLANES
    slab = lambda a, v: a[SUBLANES * v:SUBLANES * (v + 1), :]
    slabs = [slab(s, v) for v in range(nv)]
    sub = lax.broadcasted_iota(jnp.int32, (SUBLANES, t), 0)
    out_row = lax.broadcasted_iota(jnp.int32, (k, t), 0)
    vals = jnp.zeros((k, t), s.dtype)
    rows = jnp.zeros((k, t), jnp.int32)
    pays = None if payload is None else jnp.zeros((k, t), payload.dtype)
    for r in range(k):
        nodes = [(slabs[v], v) for v in range(nv)]
        while len(nodes) > 1:
            nxt = []
            for a in range(0, len(nodes) - 1, 2):
                (va, ia), (vb, ib) = nodes[a], nodes[a + 1]
                c = va >= vb
                nxt.append((jnp.where(c, va, vb), jnp.where(c, ia, ib)))
            if len(nodes) % 2:
                nxt.append(nodes[-1])
            nodes = nxt
        m8, v8 = nodes[0]
        m = jnp.max(m8, axis=0, keepdims=True)
        idx = jnp.min(jnp.where(m8 == m, v8 * SUBLANES + sub, n_rows), axis=0, keepdims=True)
        hits = [sub == idx - SUBLANES * v for v in range(nv)]
        vals = jnp.where(out_row == r, m, vals)
        rows = jnp.where(out_row == r, idx, rows)
        if payload is not None:
            picked = [jnp.where(hits[v], slab(payload, v), 0) for v in range(nv)]
            p = jnp.sum(functools.reduce(jnp.add, picked), axis=0, keepdims=True)
            pays = jnp.where(out_row == r, p, pays)
        slabs = [jnp.where(hits[v], -jnp.inf, slabs[v]) for v in range(nv)]
    return vals, rows, pays


def _candidate_layout(k):
    slabs, cur = [], []

    def flush():
        n_valid = len(cur)
        while len(cur) < SUBLANES:
            r = len(cur)
            period = 1
            while period < max(b for _, b in cur[:n_valid]) + 1:
                period *= 2
            b = cur[r - period][1] if r >= period else r
            cur.append((cur[n_valid - 1][0], b))
        slabs.append(([a for a, _ in cur], [b for _, b in cur], n_valid))
        cur.clear()

    for a in range(k):
        nb = k // (a + 1)
        for b0 in range(0, nb, SUBLANES):
            group = [(a, b) for b in range(b0, min(nb, b0 + SUBLANES))]
            if cur and (len(cur) + len(group) > SUBLANES or (nb == 1 and a % SUBLANES == 0)):
                flush()
            cur.extend(group)
    if cur:
        flush()
    return slabs


def _rows_by_pattern(x, pattern, sub):
    p0 = pattern[0]
    if p0 % SUBLANES == 0 and pattern == list(range(p0, p0 + SUBLANES)):
        return x[p0:p0 + SUBLANES, :]
    period = SUBLANES
    while period > 1 and all(pattern[r] == pattern[r % (period // 2)] for r in range(SUBLANES)):
        period //= 2
    pos = sub if period == SUBLANES else sub & (period - 1)
    out = x[p0:p0 + 1, :]
    for r in range(1, period):
        if pattern[r] != pattern[r - 1]:
            out = jnp.where(pos >= r, x[pattern[r]:pattern[r] + 1, :], out)
    return jnp.broadcast_to(out, (SUBLANES, x.shape[1]))


def _route_kernel(q_ref, keys_ref, gm_ref, sv_scr, si_scr, it_scr, jt_scr, gt_scr, il_scr, jl_scr, gl_scr,
                  w_scr, *, n_heads, n_keys, stride):
    K = P_TOPK
    t = q_ref.shape[0]
    slots = n_heads * K
    step = pl.program_id(0)
    cur = step % 2
    prev = 1 - cur
    sub = lax.broadcasted_iota(jnp.int32, (SUBLANES, t), 0)
    key_row = lax.broadcasted_iota(jnp.int32, (n_keys, slots), 0)
    layout = _candidate_layout(K)
    n_loop = n_heads // 2
    tok_per_iter = t // n_loop
    blk_per_iter = n_keys // n_loop

    @pl.when(step == 0)
    def _():
        il_scr[1] = jnp.zeros((t, slots), jnp.int32)
        jl_scr[1] = jnp.zeros((t, slots), jnp.int32)
        gl_scr[1] = jnp.zeros((t, slots), F32)

    def gate_rows(tok):
        i_row = il_scr[prev, pl.ds(tok, 1), :]
        j_row = jl_scr[prev, pl.ds(tok, 1), :]
        g_row = gl_scr[prev, pl.ds(tok, 1), :]
        a_t = jnp.where(key_row == i_row, 1.0, 0.0).astype(BF16)
        b_t = jnp.where(key_row == j_row, g_row, 0.0).astype(BF16)
        m_t = lax.dot_general(a_t, b_t, NT, preferred_element_type=F32)
        w_scr[pl.ds(tok, n_keys, stride=stride), :] = m_t

    def stage1(hh, _):
        for p in range(4):
            hp = 4 * hh + p
            c0 = pl.multiple_of(hp * LANES, LANES)
            qb = q_ref[:, pl.ds(c0, LANES)]
            st = lax.dot_general(keys_ref[hp], qb, NT, preferred_element_type=F32)
            v, ix, _ = _topk_rows(st, K)
            sv_scr[hp] = v
            si_scr[hp] = ix
        t0 = pl.multiple_of(hh * tok_per_iter, tok_per_iter)
        for u in range(tok_per_iter):
            gate_rows(t0 + u)
        return 0

    lax.fori_loop(0, n_loop, stage1, 0)

    def stage2(hh, _):
        b0 = hh * blk_per_iter
        for u in range(blk_per_iter):
            r0 = pl.multiple_of((b0 + u) * stride, SUBLANES)
            gm_ref[b0 + u] = w_scr[pl.ds(r0, t), :].astype(gm_ref.dtype)
        for u in range(2):
            h = 2 * hh + u
            sv0, sv1 = sv_scr[2 * h], sv_scr[2 * h + 1]
            si0, si1 = si_scr[2 * h], si_scr[2 * h + 1]
            cand, cidx = [], []
            for a_pat, b_pat, n_valid in layout:
                c = _rows_by_pattern(sv0, a_pat, sub) + _rows_by_pattern(sv1, b_pat, sub)
                if n_valid < SUBLANES:
                    c = jnp.where(sub < n_valid, c, -jnp.inf)
                cand.append(c)
                cidx.append(_rows_by_pattern(si0, a_pat, sub) * n_keys + _rows_by_pattern(si1, b_pat, sub))
            fv, _, eidx = _topk_rows(jnp.concatenate(cand, axis=0), K, jnp.concatenate(cidx, axis=0))
            e = jnp.exp(fv - fv[0:1, :])
            g = e / jnp.sum(e, axis=0, keepdims=True)
            r0 = pl.multiple_of(h * K, K)
            it_scr[pl.ds(r0, K), :] = eidx // n_keys
            jt_scr[pl.ds(r0, K), :] = eidx % n_keys
            gt_scr[pl.ds(r0, K), :] = g
        return 0

    lax.fori_loop(0, n_loop, stage2, 0)
    il_scr[cur] = it_scr[...].T
    jl_scr[cur] = jt_scr[...].T
    gl_scr[cur] = gt_scr[...].T


def _route(q, keys, n_heads, n_keys):
    n = q.shape[0]
    t = LANES
    n_tiles = n // t
    slots = n_heads * P_TOPK
    stride = t + SUBLANES
    kern = functools.partial(_route_kernel, n_heads=n_heads, n_keys=n_keys, stride=stride)
    return pl.pallas_call(
        kern,
        grid=(n_tiles + 1,),
        in_specs=[pl.BlockSpec((t, q.shape[1]), lambda s: (jnp.minimum(s, n_tiles - 1), 0)),
                  pl.BlockSpec(keys.shape, lambda s: (0, 0, 0))],
        out_specs=pl.BlockSpec((n_keys, t, n_keys), lambda s: (0, jnp.maximum(s - 1, 0), 0)),
        out_shape=jax.ShapeDtypeStruct((n_keys, n, n_keys), BF16),
        scratch_shapes=[pltpu.VMEM((2 * n_heads, P_TOPK, t), F32),
                        pltpu.VMEM((2 * n_heads, P_TOPK, t), jnp.int32),
                        pltpu.VMEM((slots, t), jnp.int32),
                        pltpu.VMEM((slots, t), jnp.int32),
                        pltpu.VMEM((slots, t), F32),
                        pltpu.VMEM((2, t, slots), jnp.int32),
                        pltpu.VMEM((2, t, slots), jnp.int32),
                        pltpu.VMEM((2, t, slots), F32),
                        pltpu.VMEM((n_keys * stride, n_keys), F32)],
        compiler_params=_params("arbitrary"),
        name="route",
    )(q, keys)


def _experts_kernel(h2_ref, u_ref, v_ref, m_ref, x1_ref, g2_ref, fg_ref, o_ref, acc_ref, *, final):
    j = pl.program_id(2)

    @pl.when(j == 0)
    def _():
        acc_ref[...] = jnp.zeros_like(acc_ref)

    s = lax.dot_general(h2_ref[0], u_ref[...], NT, preferred_element_type=F32)
    act = 0.5 * s * (1.0 + lax.erf(s * (2.0 ** -0.5)))
    gate = jnp.concatenate([m_ref[ib] for ib in range(m_ref.shape[0])], axis=1)
    w = (gate.astype(F32) * act).astype(BF16)
    acc_ref[...] += jnp.dot(w, v_ref[...], preferred_element_type=F32)

    @pl.when(j == pl.num_programs(2) - 1)
    def _():
        x2 = x1_ref[0] + g2_ref[0] * acc_ref[...]
        if final:
            x2 = x2 * lax.rsqrt(jnp.mean(x2 * x2, axis=-1, keepdims=True) + EPS) * fg_ref[...]
        o_ref[0] = x2


def _experts(h2, u, v, gmat, x1, g2, final_g, final, tm=1024, te=1024):
    bsz, s, d = x1.shape
    tm = min(tm, s)
    ne = u.shape[0]
    n_keys = gmat.shape[2]
    n_tiles = s // tm
    kern = functools.partial(_experts_kernel, final=final)
    return pl.pallas_call(
        kern,
        grid=(bsz, n_tiles, ne // te),
        in_specs=[pl.BlockSpec((1, tm, d), lambda b, i, j: (b, i, 0)),
                  pl.BlockSpec((te, d), lambda b, i, j: (j, 0)),
                  pl.BlockSpec((te, d), lambda b, i, j: (j, 0)),
                  pl.BlockSpec((te // n_keys, tm, n_keys), lambda b, i, j: (j, b * n_tiles + i, 0)),
                  pl.BlockSpec((1, tm, d), lambda b, i, j: (b, i, 0)),
                  pl.BlockSpec((1, 1, d), lambda b, i, j: (b, 0, 0)),
                  pl.BlockSpec((1, d), lambda b, i, j: (0, 0))],
        out_specs=pl.BlockSpec((1, tm, d), lambda b, i, j: (b, i, 0)),
        out_shape=jax.ShapeDtypeStruct((bsz, s, d), F32),
        scratch_shapes=[pltpu.VMEM((tm, d), F32)],
        compiler_params=_params("arbitrary", "arbitrary", "arbitrary"),
        name="experts",
    )(h2, u, v, gmat, x1, g2, final_g)


def kernel(x, c, ada_w, ada_b, norm1_g, w_in, conv_w, conv_b, mlstm_gate_b, mlstm_norm_g, lambda_q1,
           lambda_k1, lambda_q2, lambda_k2, diff_norm_g, w_out, norm2_g, peer_w_query, peer_sub_keys,
           peer_u, peer_v, final_g):
    bsz, s, d = x.shape
    depth = ada_w.shape[0]
    m_heads = mlstm_gate_b.shape[1] // 2
    m_width = mlstm_norm_g.shape[1]
    m_hd = m_width // m_heads
    dh = lambda_q1.shape[1]
    d_width = w_out.shape[1] - m_width
    d_heads = d_width // (2 * dh)
    p_heads, _, n_keys, _ = peer_sub_keys.shape[1:]
    n_m = 4 * m_width
    n_g = 2 * m_heads
    assert s % M_CHUNK == 0 and m_hd == LANES and 2 * dh == LANES and n_keys == LANES
    assert peer_sub_keys.shape[-1] == LANES and p_heads % 2 == 0

    for l in range(depth):
        mod = _adaln(c, ada_w[l], ada_b[l])
        sh1, sc1, g1, sh2, sc2, g2 = (a.reshape(bsz, 1, d) for a in jnp.split(mod, 6, axis=-1))

        w = w_in[l].astype(BF16)
        w_m = w[:, :n_m]
        w_g = jnp.pad(w[:, n_m:n_m + n_g], ((0, 0), (0, LANES - n_g)))
        w_d = w[:, n_m + n_g:]
        pm, pg, pd = _inproj(x, sh1, sc1, norm1_g[l].reshape(1, d), w_m, w_g, w_d)

        gates = pg[:, :, :n_g].transpose(0, 2, 1).reshape(bsz, n_g, s // M_CHUNK, M_CHUNK)
        hm = _mlstm(pm, gates, mlstm_gate_b[l], conv_w[l], conv_b[l].reshape(1, -1),
                    mlstm_norm_g[l].reshape(1, -1), m_heads, m_hd)

        lam_init = 0.8 - 0.6 * math.exp(-0.3 * l)
        lam_p = jnp.stack([lambda_q1[l], lambda_k1[l], lambda_q2[l], lambda_k2[l]])
        od = _diffattn(pd, lam_p, diff_norm_g[l].reshape(1, -1), d_heads, dh, lam_init)

        wo = w_out[l].astype(BF16)
        x1, h2, q = _outproj(hm, od, x, g1, sh2, sc2, norm2_g[l].reshape(1, d), wo[:m_width], wo[m_width:],
                             peer_w_query[l].astype(BF16))

        keys = peer_sub_keys[l].reshape(2 * p_heads, n_keys, -1).astype(BF16)
        gmat = _route(q.reshape(bsz * s, -1), keys, p_heads, n_keys)
        x = _experts(h2, peer_u[l].astype(BF16), peer_v[l].astype(BF16), gmat, x1, g2,
                     final_g.reshape(1, d), final=(l == depth - 1))
    return x
```

```python
import functools
import math

import jax
import jax.numpy as jnp
from jax import lax
from jax.experimental import pallas as pl
from jax.experimental.pallas import tpu as pltpu

F32 = jnp.float32
BF16 = jnp.bfloat16
EPS = 1e-6
LANES = 128
SUBLANES = 8
VMEM_LIMIT = 56 * 1024 * 1024
M_CHUNK = 128
CONV_WIDTH = 4
P_TOPK = 16

NT = (((1,), (1,)), ((), ()))
TN = (((0,), (0,)), ((), ()))


def _params(*sem):
    return pltpu.CompilerParams(dimension_semantics=sem, vmem_limit_bytes=VMEM_LIMIT)


def _adaln_kernel(c_ref, w_ref, b_ref, o_ref):
    c = c_ref[...]
    sc = (c * jax.nn.sigmoid(c)).astype(BF16)
    o_ref[...] = jnp.dot(sc, w_ref[...].astype(BF16), preferred_element_type=F32) + b_ref[...]


def _adaln(c, w, b):
    bsz, d = c.shape
    n = w.shape[1]
    tn = 1536
    return pl.pallas_call(
        _adaln_kernel,
        grid=(n // tn,),
        in_specs=[pl.BlockSpec((bsz, d), lambda j: (0, 0)),
                  pl.BlockSpec((d, tn), lambda j: (0, j)),
                  pl.BlockSpec((1, tn), lambda j: (0, j))],
        out_specs=pl.BlockSpec((bsz, tn), lambda j: (0, j)),
        out_shape=jax.ShapeDtypeStruct((bsz, n), F32),
        compiler_params=_params("arbitrary"),
        name="adaln",
    )(c, w, b.reshape(1, n))


def _inproj_kernel(x_ref, sh_ref, sc_ref, g_ref, wm_ref, wg_ref, wd_ref, pm_ref, pg_ref, pd_ref):
    x = x_ref[0]
    ms = jnp.mean(x * x, axis=-1, keepdims=True)
    h = x * lax.rsqrt(ms + EPS) * g_ref[...]
    h = h * (1.0 + sc_ref[0]) + sh_ref[0]
    hb = h.astype(BF16)
    pm_ref[0] = jnp.dot(hb, wm_ref[...], preferred_element_type=F32)
    pg_ref[0] = jnp.dot(hb, wg_ref[...], preferred_element_type=F32)
    pd_ref[0] = jnp.dot(hb, wd_ref[...], preferred_element_type=F32).astype(BF16)


def _inproj(x, sh, sc, g, wm, wg, wd, tm=512):
    bsz, s, d = x.shape
    tm = min(tm, s)
    nm, ng, nd = wm.shape[1], wg.shape[1], wd.shape[1]
    tok = lambda b, i: (b, i, 0)
    per_b = lambda b, i: (b, 0, 0)
    const = lambda b, i: (0, 0)
    return pl.pallas_call(
        _inproj_kernel,
        grid=(bsz, s // tm),
        in_specs=[pl.BlockSpec((1, tm, d), tok),
                  pl.BlockSpec((1, 1, d), per_b),
                  pl.BlockSpec((1, 1, d), per_b),
                  pl.BlockSpec((1, d), const),
                  pl.BlockSpec((d, nm), const),
                  pl.BlockSpec((d, ng), const),
                  pl.BlockSpec((d, nd), const)],
        out_specs=[pl.BlockSpec((1, tm, nm), tok),
                   pl.BlockSpec((1, tm, ng), tok),
                   pl.BlockSpec((1, tm, nd), tok)],
        out_shape=[jax.ShapeDtypeStruct((bsz, s, nm), F32),
                   jax.ShapeDtypeStruct((bsz, s, ng), F32),
                   jax.ShapeDtypeStruct((bsz, s, nd), BF16)],
        compiler_params=_params("arbitrary", "arbitrary"),
        name="inproj",
    )(x, sh, sc, g, wm, wg, wd)


def _causal_conv(u, tail, w, b):
    ext = jnp.concatenate([tail, u], axis=0)
    y = u * w[CONV_WIDTH - 1:CONV_WIDTH, :]
    for k in range(CONV_WIDTH - 1):
        shift = CONV_WIDTH - 1 - k
        y = y + pltpu.roll(ext, shift, axis=0)[SUBLANES:, :] * w[k:k + 1, :]
    return y + b


def _cumsum_lanes(x):
    lane = lax.broadcasted_iota(jnp.int32, x.shape, 1)
    s = 1
    while s < x.shape[1]:
        x = x + jnp.where(lane >= s, pltpu.roll(x, s, axis=1), 0.0)
        s *= 2
    return x


def _mlstm_kernel(gb_ref, mq_ref, mk_ref, mv_ref, mo_ref, g_ref, cwq_ref, cwk_ref, cbq_ref, cbk_ref,
                  ng_ref, o_ref, b_scr, i_scr, *, n_heads, head_dim, group):
    h0 = pl.program_id(1) * group
    L = M_CHUNK
    hd = head_dim
    nc = b_scr.shape[1]
    silu = lambda a: a * jax.nn.sigmoid(a)

    for u in range(group):
        i_scr[u] = g_ref[0, h0 + u] + gb_ref[h0 + u]
        fpre = g_ref[0, n_heads + h0 + u] + gb_ref[n_heads + h0 + u]
        lf = jnp.minimum(fpre, 0.0) - jnp.log1p(jnp.exp(-jnp.abs(fpre)))
        b_scr[u] = _cumsum_lanes(lf)

    r_i = lax.broadcasted_iota(jnp.int32, (L, L), 0)
    c_i = lax.broadcasted_iota(jnp.int32, (L, L), 1)
    eye = r_i == c_i
    causal = c_i <= r_i

    def to_col(row):
        return jnp.sum(jnp.where(eye, row, 0.0), axis=1, keepdims=True)

    def chunk(c, carry):
        t0 = pl.multiple_of(c * L, L)
        return tuple(head_chunk(c, t0, u, carry[u]) for u in range(group))

    def head_chunk(c, t0, u, carry):
        C, n, m, q_tail, k_tail = carry
        cols = slice(u * hd, (u + 1) * hd)
        q_raw = mq_ref[0, pl.ds(t0, L), cols]
        k_raw = mk_ref[0, pl.ds(t0, L), cols]
        qc = silu(_causal_conv(q_raw, q_tail, cwq_ref[:, cols], cbq_ref[:, cols])).astype(BF16)
        kc = (silu(_causal_conv(k_raw, k_tail, cwk_ref[:, cols], cbk_ref[:, cols])) * (hd ** -0.5)).astype(BF16)
        vc = mv_ref[0, pl.ds(t0, L), cols].astype(BF16)
        b_row = b_scr[u, pl.ds(c, 1), :]
        i_row = i_scr[u, pl.ds(c, 1), :]
        b_col = to_col(b_row)
        i_col = to_col(i_row)

        logD = jnp.where(causal, b_col - b_row + i_row, -jnp.inf)
        m_t = jnp.maximum(b_col + m, jnp.max(logD, axis=1, keepdims=True))
        Dw = jnp.exp(logD - m_t)
        inter = jnp.exp(b_col + m - m_t)
        sqk = lax.dot_general(qc, kc, NT, preferred_element_type=F32) * Dw
        num = (jnp.dot(sqk.astype(BF16), vc, preferred_element_type=F32)
               + inter * jnp.dot(qc, C.astype(BF16), preferred_element_type=F32))
        den = (jnp.sum(sqk, axis=1, keepdims=True)
               + inter * jnp.sum(qc.astype(F32) * n, axis=1, keepdims=True))
        hh = num / jnp.maximum(jnp.abs(den), jnp.exp(-m_t))

        bL = b_row[:, L - 1:L]
        m_new = jnp.maximum(bL + m, jnp.max(bL - b_row + i_row, axis=1, keepdims=True))
        w_col = jnp.exp(bL - b_col + i_col - m_new)
        decay = jnp.exp(bL + m - m_new)
        kw = kc.astype(F32) * w_col
        C_new = decay * C + lax.dot_general(kw.astype(BF16), vc, TN, preferred_element_type=F32)
        n_new = decay * n + jnp.sum(kw, axis=0, keepdims=True)

        y = hh * lax.rsqrt(jnp.mean(hh * hh, axis=-1, keepdims=True) + EPS) * ng_ref[:, cols]
        y = y * jax.nn.sigmoid(mo_ref[0, pl.ds(t0, L), cols])
        o_ref[0, pl.ds(t0, L), cols] = y.astype(o_ref.dtype)
        return C_new, n_new, m_new, q_raw[L - SUBLANES:, :], k_raw[L - SUBLANES:, :]

    zeros = lambda *shape: jnp.zeros(shape, F32)
    init = (zeros(hd, hd), zeros(1, hd), zeros(1, 1), zeros(SUBLANES, hd), zeros(SUBLANES, hd))
    lax.fori_loop(0, nc, chunk, (init,) * group)


def _mlstm(pm, gates, gate_b, conv_w, conv_b, norm_g, n_heads, head_dim, group=2):
    bsz, s, _ = pm.shape
    nc = s // M_CHUNK
    gw = group * head_dim
    n_groups = n_heads // group
    col = lambda off: (lambda b, g: (b, 0, off * n_groups + g))
    wcol = lambda off: (lambda b, g: (0, off * n_groups + g))
    kern = functools.partial(_mlstm_kernel, n_heads=n_heads, head_dim=head_dim, group=group)
    return pl.pallas_call(
        kern,
        grid=(bsz, n_groups),
        in_specs=[pl.BlockSpec(memory_space=pltpu.SMEM),
                  pl.BlockSpec((1, s, gw), col(0)),
                  pl.BlockSpec((1, s, gw), col(1)),
                  pl.BlockSpec((1, s, gw), col(2)),
                  pl.BlockSpec((1, s, gw), col(3)),
                  pl.BlockSpec((1, 2 * n_heads, nc, M_CHUNK), lambda b, g: (b, 0, 0, 0)),
                  pl.BlockSpec((CONV_WIDTH, gw), wcol(0)),
                  pl.BlockSpec((CONV_WIDTH, gw), wcol(1)),
                  pl.BlockSpec((1, gw), wcol(0)),
                  pl.BlockSpec((1, gw), wcol(1)),
                  pl.BlockSpec((1, gw), wcol(0))],
        out_specs=pl.BlockSpec((1, s, gw), col(0)),
        out_shape=jax.ShapeDtypeStruct((bsz, s, n_heads * head_dim), BF16),
        scratch_shapes=[pltpu.VMEM((group, nc, M_CHUNK), F32), pltpu.VMEM((group, nc, M_CHUNK), F32)],
        compiler_params=_params("arbitrary", "arbitrary"),
        name="mlstm",
    )(gate_b, pm, pm, pm, pm, gates, conv_w, conv_w, conv_b, conv_b, norm_g)


def _diffattn_kernel(q_ref, k_ref, v_ref, lam_ref, ng_ref, o_ref, m_scr, l_scr, acc_scr,
                     *, dh, lam_init, group):
    i = pl.program_id(2)
    tq = q_ref.shape[1]
    hw = 2 * dh
    scale = dh ** -0.5
    exact_scale = math.frexp(scale)[0] == 0.5
    lane = lax.broadcasted_iota(jnp.int32, (tq, hw), 1)
    qs = []
    for u in range(group):
        q = q_ref[0, :, u * hw:(u + 1) * hw]
        if exact_scale:
            q = q * scale
        zero = jnp.zeros_like(q)
        qs += [jnp.where(lane < dh, q, zero), jnp.where(lane < dh, zero, q)]
    lp = lam_ref[...]
    lam = (jnp.exp(jnp.sum(lp[0:1] * lp[1:2], axis=1, keepdims=True))
           - jnp.exp(jnp.sum(lp[2:3] * lp[3:4], axis=1, keepdims=True)) + lam_init)
    on_or_below_diag = (lax.broadcasted_iota(jnp.int32, (tq, tq), 1)
                        <= lax.broadcasted_iota(jnp.int32, (tq, tq), 0))

    m_scr[...] = jnp.full(m_scr.shape, -jnp.inf, F32)
    l_scr[...] = jnp.zeros(l_scr.shape, F32)
    acc_scr[...] = jnp.zeros(acc_scr.shape, F32)

    def block(j, diagonal):
        k0 = pl.multiple_of(j * tq, tq)
        for st in range(2 * group):
            cols = slice((st // 2) * hw, (st // 2 + 1) * hw)
            kb = k_ref[0, pl.ds(k0, tq), cols]
            vb = v_ref[0, pl.ds(k0, tq), cols]
            s = lax.dot_general(qs[st], kb, NT, preferred_element_type=F32)
            if not exact_scale:
                s = s * scale
            if diagonal:
                s = jnp.where(on_or_below_diag, s, -jnp.inf)
            m_old = m_scr[st]
            m_new = jnp.maximum(m_old, jnp.broadcast_to(jnp.max(s, axis=1, keepdims=True), m_old.shape))
            alpha = jnp.exp(m_old - m_new)
            e = [jnp.exp(s[:, c:c + LANES] - m_new) for c in range(0, tq, LANES)]
            l_scr[st] = alpha * l_scr[st] + functools.reduce(jnp.add, e)
            eb = jnp.concatenate([x.astype(BF16) for x in e], axis=1)
            acc_scr[st] = alpha * acc_scr[st] + jnp.dot(eb, vb, preferred_element_type=F32)
            m_scr[st] = m_new

    def below(j, carry):
        block(j, False)
        return carry

    lax.fori_loop(0, i, below, 0)
    block(i, True)
    for u in range(group):
        o1 = acc_scr[2 * u] / jnp.sum(l_scr[2 * u], axis=1, keepdims=True)
        o2 = acc_scr[2 * u + 1] / jnp.sum(l_scr[2 * u + 1], axis=1, keepdims=True)
        o = o1 - lam * o2
        y = o * lax.rsqrt(jnp.mean(o * o, axis=-1, keepdims=True) + EPS) * ng_ref[...] * (1.0 - lam_init)
        o_ref[0, :, u * hw:(u + 1) * hw] = y.astype(o_ref.dtype)


def _diffattn(pd, lam_p, norm_g, n_heads, dh, lam_init, tq=256, group=4):
    bsz, s, _ = pd.shape
    hw = 2 * dh
    gw = group * hw
    n_groups = n_heads // group
    kern = functools.partial(_diffattn_kernel, dh=dh, lam_init=lam_init, group=group)
    return pl.pallas_call(
        kern,
        grid=(bsz, n_groups, s // tq),
        in_specs=[pl.BlockSpec((1, tq, gw), lambda b, g, i: (b, i, g)),
                  pl.BlockSpec((1, s, gw), lambda b, g, i: (b, 0, n_groups + g)),
                  pl.BlockSpec((1, s, gw), lambda b, g, i: (b, 0, 2 * n_groups + g)),
                  pl.BlockSpec((4, dh), lambda b, g, i: (0, 0)),
                  pl.BlockSpec((1, hw), lambda b, g, i: (0, 0))],
        out_specs=pl.BlockSpec((1, tq, gw), lambda b, g, i: (b, i, g)),
        out_shape=jax.ShapeDtypeStruct((bsz, s, n_heads * hw), BF16),
        scratch_shapes=[pltpu.VMEM((2 * group, tq, LANES), F32), pltpu.VMEM((2 * group, tq, LANES), F32),
                        pltpu.VMEM((2 * group, tq, hw), F32)],
        compiler_params=_params("arbitrary", "arbitrary", "arbitrary"),
        name="diffattn",
    )(pd, pd, pd, lam_p, norm_g)


def _outproj_kernel(hm_ref, od_ref, x_ref, g1_ref, sh_ref, sc_ref, ng_ref, wm_ref, wd_ref, wq_ref,
                    x1_ref, h2_ref, q_ref):
    y = (jnp.dot(hm_ref[0], wm_ref[...], preferred_element_type=F32)
         + jnp.dot(od_ref[0], wd_ref[...], preferred_element_type=F32))
    x1 = x_ref[0] + g1_ref[0] * y
    x1_ref[0] = x1
    ms = jnp.mean(x1 * x1, axis=-1, keepdims=True)
    h2 = x1 * lax.rsqrt(ms + EPS) * ng_ref[...]
    h2 = (h2 * (1.0 + sc_ref[0]) + sh_ref[0]).astype(BF16)
    h2_ref[0] = h2
    q_ref[0] = jnp.dot(h2, wq_ref[...], preferred_element_type=F32).astype(BF16)


def _outproj(hm, od, x, g1, sh2, sc2, ng, wm, wd, wq, tm=512):
    bsz, s, d = x.shape
    tm = min(tm, s)
    wm_w, wd_w, nq = hm.shape[2], od.shape[2], wq.shape[1]
    tok = lambda b, i: (b, i, 0)
    per_b = lambda b, i: (b, 0, 0)
    const = lambda b, i: (0, 0)
    return pl.pallas_call(
        _outproj_kernel,
        grid=(bsz, s // tm),
        in_specs=[pl.BlockSpec((1, tm, wm_w), tok),
                  pl.BlockSpec((1, tm, wd_w), tok),
                  pl.BlockSpec((1, tm, d), tok),
                  pl.BlockSpec((1, 1, d), per_b),
                  pl.BlockSpec((1, 1, d), per_b),
                  pl.BlockSpec((1, 1, d), per_b),
                  pl.BlockSpec((1, d), const),
                  pl.BlockSpec((wm_w, d), const),
                  pl.BlockSpec((wd_w, d), const),
                  pl.BlockSpec((d, nq), const)],
        out_specs=[pl.BlockSpec((1, tm, d), tok),
                   pl.BlockSpec((1, tm, d), tok),
                   pl.BlockSpec((1, tm, nq), tok)],
        out_shape=[jax.ShapeDtypeStruct((bsz, s, d), F32),
                   jax.ShapeDtypeStruct((bsz, s, d), BF16),
                   jax.ShapeDtypeStruct((bsz, s, nq), BF16)],
        compiler_params=_params("arbitrary", "arbitrary"),
        name="outproj",
    )(hm, od, x, g1, sh2, sc2, ng, wm, wd, wq)


def _topk_rows(s, k, payload=None):
    n_rows, t = s.shape
    nv = n_rows // SUBLANES
    slab = lambda a, v: a[SUBLANES * v:SUBLANES * (v + 1), :]
    slabs = [slab(s, v) for v in range(nv)]
    sub = lax.broadcasted_iota(jnp.int32, (SUBLANES, t), 0)
    out_row = lax.broadcasted_iota(jnp.int32, (k, t), 0)
    vals = jnp.zeros((k, t), s.dtype)
    rows = jnp.zeros((k, t), jnp.int32)
    pays = None if payload is None else jnp.zeros((k, t), payload.dtype)
    for r in range(k):
        nodes = [(slabs[v], v) for v in range(nv)]
        while len(nodes) > 1:
            nxt = []
            for a in range(0, len(nodes) - 1, 2):
                (va, ia), (vb, ib) = nodes[a], nodes[a + 1]
                c = va >= vb
                nxt.append((jnp.where(c, va, vb), jnp.where(c, ia, ib)))
            if len(nodes) % 2:
                nxt.append(nodes[-1])
            nodes = nxt
        m8, v8 = nodes[0]
        m = jnp.max(m8, axis=0, keepdims=True)
        row8 = v8 * SUBLANES + sub
        idx = jnp.min(jnp.where(m8 == m, row8, n_rows), axis=0, keepdims=True)
        won = jnp.where(row8 == idx, v8, -1)
        hits = [won == v for v in range(nv)]
        vals = jnp.where(out_row == r, m, vals)
        rows = jnp.where(out_row == r, idx, rows)
        if payload is not None:
            picked = [jnp.where(hits[v], slab(payload, v), 0) for v in range(nv)]
            p = jnp.sum(functools.reduce(jnp.add, picked), axis=0, keepdims=True)
            pays = jnp.where(out_row == r, p, pays)
        slabs = [jnp.where(hits[v], -jnp.inf, slabs[v]) for v in range(nv)]
    return vals, rows, pays


def _oddeven_merge_sort(n):
    def merge(lo, hi, r):
        step = 2 * r
        if step < hi - lo:
            yield from merge(lo, hi, step)
            yield from merge(lo + r, hi, step)
            yield from ((i, i + r) for i in range(lo + r, hi - r, step))
        else:
            yield (lo, lo + r)

    def sort(lo, hi):
        if hi > lo:
            mid = lo + (hi - lo) // 2
            yield from sort(lo, mid)
            yield from sort(mid + 1, hi)
            yield from merge(lo, hi, 1)

    return list(sort(0, n - 1))


def _topk_rows_sorted(s, k):
    n_rows, t = s.shape
    nv = n_rows // SUBLANES
    assert nv == k and nv & (nv - 1) == 0
    sub = lax.broadcasted_iota(jnp.int32, (SUBLANES, t), 0)
    out_row = lax.broadcasted_iota(jnp.int32, (k, t), 0)
    val = [s[SUBLANES * v:SUBLANES * (v + 1), :] for v in range(nv)]
    sid = list(range(nv))
    for i, j in _oddeven_merge_sort(nv):
        va, vb, ia, ib = val[i], val[j], sid[i], sid[j]
        if isinstance(ia, int) and isinstance(ib, int):
            first = va >= vb if ia < ib else va > vb
        else:
            first = (va > vb) | ((va == vb) & (ia < ib))
        val[i], val[j] = jnp.where(first, va, vb), jnp.where(first, vb, va)
        sid[i], sid[j] = jnp.where(first, ia, ib), jnp.where(first, ib, ia)
    row = [x * SUBLANES + sub for x in sid]
    vals = jnp.zeros((k, t), s.dtype)
    rows = jnp.zeros((k, t), jnp.int32)
    for r in range(k):
        m = jnp.max(val[0], axis=0, keepdims=True)
        idx = jnp.min(jnp.where(val[0] == m, row[0], n_rows), axis=0, keepdims=True)
        hit = row[0] == idx
        vals = jnp.where(out_row == r, m, vals)
        rows = jnp.where(out_row == r, idx, rows)
        for q in range(k - 1 - r):
            val[q] = jnp.where(hit, val[q + 1], val[q])
            row[q] = jnp.where(hit, row[q + 1], row[q])
    return vals, rows


def _candidate_layout(k):
    slabs, cur = [], []

    def flush():
        n_valid = len(cur)
        while len(cur) < SUBLANES:
            r = len(cur)
            period = 1
            while period < max(b for _, b in cur[:n_valid]) + 1:
                period *= 2
            b = cur[r - period][1] if r >= period else r
            cur.append((cur[n_valid - 1][0], b))
        slabs.append(([a for a, _ in cur], [b for _, b in cur], n_valid))
        cur.clear()

    for a in range(k):
        nb = k // (a + 1)
        for b0 in range(0, nb, SUBLANES):
            group = [(a, b) for b in range(b0, min(nb, b0 + SUBLANES))]
            if cur and (len(cur) + len(group) > SUBLANES or (nb == 1 and a % SUBLANES == 0)):
                flush()
            cur.extend(group)
    if cur:
        flush()
    return slabs


def _rows_by_pattern(x, pattern, sub):
    p0 = pattern[0]
    if p0 % SUBLANES == 0 and pattern == list(range(p0, p0 + SUBLANES)):
        return x[p0:p0 + SUBLANES, :]
    period = SUBLANES
    while period > 1 and all(pattern[r] == pattern[r % (period // 2)] for r in range(SUBLANES)):
        period //= 2
    pos = sub if period == SUBLANES else sub & (period - 1)
    out = x[p0:p0 + 1, :]
    for r in range(1, period):
        if pattern[r] != pattern[r - 1]:
            out = jnp.where(pos >= r, x[pattern[r]:pattern[r] + 1, :], out)
    return jnp.broadcast_to(out, (SUBLANES, x.shape[1]))


def _route_kernel(q_ref, keys_ref, gm_ref, sv_scr, si_scr, it_scr, jt_scr, gt_scr, il_scr, jl_scr, gl_scr,
                  w_scr, *, n_heads, n_keys, stride):
    K = P_TOPK
    t = q_ref.shape[0]
    slots = n_heads * K
    step = pl.program_id(0)
    cur = step % 2
    prev = 1 - cur
    sub = lax.broadcasted_iota(jnp.int32, (SUBLANES, t), 0)
    key_row = lax.broadcasted_iota(jnp.int32, (n_keys, slots), 0)
    layout = _candidate_layout(K)
    n_loop = n_heads // 2
    blk_per_iter = n_keys // n_loop

    @pl.when(step == 0)
    def _():
        il_scr[1] = jnp.zeros((t, slots), jnp.int32)
        jl_scr[1] = jnp.zeros((t, slots), jnp.int32)
        gl_scr[1] = jnp.zeros((t, slots), F32)

    def gate_rows(tok):
        i_row = il_scr[prev, pl.ds(tok, 1), :]
        j_row = jl_scr[prev, pl.ds(tok, 1), :]
        g_row = gl_scr[prev, pl.ds(tok, 1), :]
        a_t = jnp.where(key_row == i_row, 1.0, 0.0).astype(BF16)
        b_t = jnp.where(key_row == j_row, g_row, 0.0).astype(BF16)
        m_t = lax.dot_general(a_t, b_t, NT, preferred_element_type=F32)
        w_scr[pl.ds(tok, n_keys, stride=stride), :] = m_t

    ways = 4
    n_loop1 = 2 * n_heads // ways
    tok_per_iter = t // n_loop1

    def stage1(hh, _):
        for p in range(ways):
            hp = ways * hh + p
            c0 = pl.multiple_of(hp * LANES, LANES)
            qb = q_ref[:, pl.ds(c0, LANES)]
            st = lax.dot_general(keys_ref[hp], qb, NT, preferred_element_type=F32)
            v, ix = _topk_rows_sorted(st, K)
            sv_scr[hp] = v
            si_scr[hp] = ix
        t0 = pl.multiple_of(hh * tok_per_iter, tok_per_iter)
        for u in range(tok_per_iter):
            gate_rows(t0 + u)
        return 0

    lax.fori_loop(0, n_loop1, stage1, 0)

    def stage2(hh, _):
        b0 = hh * blk_per_iter
        for u in range(blk_per_iter):
            r0 = pl.multiple_of((b0 + u) * stride, SUBLANES)
            gm_ref[b0 + u] = w_scr[pl.ds(r0, t), :].astype(gm_ref.dtype)
        for u in range(2):
            h = 2 * hh + u
            sv0, sv1 = sv_scr[2 * h], sv_scr[2 * h + 1]
            si0, si1 = si_scr[2 * h], si_scr[2 * h + 1]
            cand, cidx = [], []
            for a_pat, b_pat, n_valid in layout:
                c = _rows_by_pattern(sv0, a_pat, sub) + _rows_by_pattern(sv1, b_pat, sub)
                if n_valid < SUBLANES:
                    c = jnp.where(sub < n_valid, c, -jnp.inf)
                cand.append(c)
                cidx.append(_rows_by_pattern(si0, a_pat, sub) * n_keys + _rows_by_pattern(si1, b_pat, sub))
            fv, _, eidx = _topk_rows(jnp.concatenate(cand, axis=0), K, jnp.concatenate(cidx, axis=0))
            e = jnp.exp(fv - fv[0:1, :])
            g = e / jnp.sum(e, axis=0, keepdims=True)
            r0 = pl.multiple_of(h * K, K)
            it_scr[pl.ds(r0, K), :] = eidx // n_keys
            jt_scr[pl.ds(r0, K), :] = eidx % n_keys
            gt_scr[pl.ds(r0, K), :] = g
        return 0

    lax.fori_loop(0, n_loop, stage2, 0)
    il_scr[cur] = it_scr[...].T
    jl_scr[cur] = jt_scr[...].T
    gl_scr[cur] = gt_scr[...].T


def _route(q, keys, n_heads, n_keys):
    n = q.shape[0]
    t = LANES
    n_tiles = n // t
    slots = n_heads * P_TOPK
    stride = t + SUBLANES
    kern = functools.partial(_route_kernel, n_heads=n_heads, n_keys=n_keys, stride=stride)
    return pl.pallas_call(
        kern,
        grid=(n_tiles + 1,),
        in_specs=[pl.BlockSpec((t, q.shape[1]), lambda s: (jnp.minimum(s, n_tiles - 1), 0)),
                  pl.BlockSpec(keys.shape, lambda s: (0, 0, 0))],
        out_specs=pl.BlockSpec((n_keys, t, n_keys), lambda s: (0, jnp.maximum(s - 1, 0), 0)),
        out_shape=jax.ShapeDtypeStruct((n_keys, n, n_keys), BF16),
        scratch_shapes=[pltpu.VMEM((2 * n_heads, P_TOPK, t), F32),
                        pltpu.VMEM((2 * n_heads, P_TOPK, t), jnp.int32),
                        pltpu.VMEM((slots, t), jnp.int32),
                        pltpu.VMEM((slots, t), jnp.int32),
                        pltpu.VMEM((slots, t), F32),
                        pltpu.VMEM((2, t, slots), jnp.int32),
                        pltpu.VMEM((2, t, slots), jnp.int32),
                        pltpu.VMEM((2, t, slots), F32),
                        pltpu.VMEM((n_keys * stride, n_keys), F32)],
        compiler_params=_params("arbitrary"),
        name="route",
    )(q, keys)


def _experts_kernel(h2_ref, u_ref, v_ref, m_ref, x1_ref, g2_ref, fg_ref, o_ref, acc_ref, *, final):
    j = pl.program_id(2)

    @pl.when(j == 0)
    def _():
        acc_ref[...] = jnp.zeros_like(acc_ref)

    s = lax.dot_general(h2_ref[0], u_ref[...], NT, preferred_element_type=F32)
    act = 0.5 * s * (1.0 + lax.erf(s * (2.0 ** -0.5)))
    gate = jnp.concatenate([m_ref[ib] for ib in range(m_ref.shape[0])], axis=1)
    w = (gate.astype(F32) * act).astype(BF16)
    acc_ref[...] += jnp.dot(w, v_ref[...], preferred_element_type=F32)

    @pl.when(j == pl.num_programs(2) - 1)
    def _():
        x2 = x1_ref[0] + g2_ref[0] * acc_ref[...]
        if final:
            x2 = x2 * lax.rsqrt(jnp.mean(x2 * x2, axis=-1, keepdims=True) + EPS) * fg_ref[...]
        o_ref[0] = x2


def _experts(h2, u, v, gmat, x1, g2, final_g, final, tm=1024, te=1024):
    bsz, s, d = x1.shape
    tm = min(tm, s)
    ne = u.shape[0]
    n_keys = gmat.shape[2]
    n_tiles = s // tm
    kern = functools.partial(_experts_kernel, final=final)
    return pl.pallas_call(
        kern,
        grid=(bsz, n_tiles, ne // te),
        in_specs=[pl.BlockSpec((1, tm, d), lambda b, i, j: (b, i, 0)),
                  pl.BlockSpec((te, d), lambda b, i, j: (j, 0)),
                  pl.BlockSpec((te, d), lambda b, i, j: (j, 0)),
                  pl.BlockSpec((te // n_keys, tm, n_keys), lambda b, i, j: (j, b * n_tiles + i, 0)),
                  pl.BlockSpec((1, tm, d), lambda b, i, j: (b, i, 0)),
                  pl.BlockSpec((1, 1, d), lambda b, i, j: (b, 0, 0)),
                  pl.BlockSpec((1, d), lambda b, i, j: (0, 0))],
        out_specs=pl.BlockSpec((1, tm, d), lambda b, i, j: (b, i, 0)),
        out_shape=jax.ShapeDtypeStruct((bsz, s, d), F32),
        scratch_shapes=[pltpu.VMEM((tm, d), F32)],
        compiler_params=_params("arbitrary", "arbitrary", "arbitrary"),
        name="experts",
    )(h2, u, v, gmat, x1, g2, final_g)


def kernel(x, c, ada_w, ada_b, norm1_g, w_in, conv_w, conv_b, mlstm_gate_b, mlstm_norm_g, lambda_q1,
           lambda_k1, lambda_q2, lambda_k2, diff_norm_g, w_out, norm2_g, peer_w_query, peer_sub_keys,
           peer_u, peer_v, final_g):
    bsz, s, d = x.shape
    depth = ada_w.shape[0]
    m_heads = mlstm_gate_b.shape[1] // 2
    m_width = mlstm_norm_g.shape[1]
    m_hd = m_width // m_heads
    dh = lambda_q1.shape[1]
    d_width = w_out.shape[1] - m_width
    d_heads = d_width // (2 * dh)
    p_heads, _, n_keys, _ = peer_sub_keys.shape[1:]
    n_m = 4 * m_width
    n_g = 2 * m_heads
    assert s % M_CHUNK == 0 and m_hd == LANES and 2 * dh == LANES and n_keys == LANES
    assert peer_sub_keys.shape[-1] == LANES and p_heads % 2 == 0 and n_keys == SUBLANES * P_TOPK

    for l in range(depth):
        mod = _adaln(c, ada_w[l], ada_b[l])
        sh1, sc1, g1, sh2, sc2, g2 = (a.reshape(bsz, 1, d) for a in jnp.split(mod, 6, axis=-1))

        w = w_in[l].astype(BF16)
        w_m = w[:, :n_m]
        w_g = jnp.pad(w[:, n_m:n_m + n_g], ((0, 0), (0, LANES - n_g)))
        w_d = w[:, n_m + n_g:]
        pm, pg, pd = _inproj(x, sh1, sc1, norm1_g[l].reshape(1, d), w_m, w_g, w_d)

        gates = pg[:, :, :n_g].transpose(0, 2, 1).reshape(bsz, n_g, s // M_CHUNK, M_CHUNK)
        hm = _mlstm(pm, gates, mlstm_gate_b[l], conv_w[l], conv_b[l].reshape(1, -1),
                    mlstm_norm_g[l].reshape(1, -1), m_heads, m_hd)

        lam_init = 0.8 - 0.6 * math.exp(-0.3 * l)
        lam_p = jnp.stack([lambda_q1[l], lambda_k1[l], lambda_q2[l], lambda_k2[l]])
        od = _diffattn(pd, lam_p, diff_norm_g[l].reshape(1, -1), d_heads, dh, lam_init)

        wo = w_out[l].astype(BF16)
        x1, h2, q = _outproj(hm, od, x, g1, sh2, sc2, norm2_g[l].reshape(1, d), wo[:m_width], wo[m_width:],
                             peer_w_query[l].astype(BF16))

        keys = peer_sub_keys[l].reshape(2 * p_heads, n_keys, -1).astype(BF16)
        gmat = _route(q.reshape(bsz * s, -1), keys, p_heads, n_keys)
        x = _experts(h2, peer_u[l].astype(BF16), peer_v[l].astype(BF16), gmat, x1, g2,
                     final_g.reshape(1, d), final=(l == depth - 1))
    return x
```

```python
import functools
import math

import jax
import jax.numpy as jnp
from jax import lax
from jax.experimental import pallas as pl
from jax.experimental.pallas import tpu as pltpu

F32 = jnp.float32
BF16 = jnp.bfloat16
EPS = 1e-6
LANES = 128
SUBLANES = 8
VMEM_LIMIT = 56 * 1024 * 1024
M_CHUNK = 128
CONV_WIDTH = 4
P_TOPK = 16

NT = (((1,), (1,)), ((), ()))
TN = (((0,), (0,)), ((), ()))


def _params(*sem):
    return pltpu.CompilerParams(dimension_semantics=sem, vmem_limit_bytes=VMEM_LIMIT)


def _adaln_kernel(c_ref, w_ref, b_ref, o_ref):
    c = c_ref[...]
    sc = (c * jax.nn.sigmoid(c)).astype(BF16)
    o_ref[...] = jnp.dot(sc, w_ref[...].astype(BF16), preferred_element_type=F32) + b_ref[...]


def _adaln(c, w, b):
    bsz, d = c.shape
    n = w.shape[1]
    tn = 1536
    return pl.pallas_call(
        _adaln_kernel,
        grid=(n // tn,),
        in_specs=[pl.BlockSpec((bsz, d), lambda j: (0, 0)),
                  pl.BlockSpec((d, tn), lambda j: (0, j)),
                  pl.BlockSpec((1, tn), lambda j: (0, j))],
        out_specs=pl.BlockSpec((bsz, tn), lambda j: (0, j)),
        out_shape=jax.ShapeDtypeStruct((bsz, n), F32),
        compiler_params=_params("arbitrary"),
        name="adaln",
    )(c, w, b.reshape(1, n))


def _inproj_kernel(x_ref, sh_ref, sc_ref, g_ref, wm_ref, wg_ref, wd_ref, pm_ref, pg_ref, pd_ref):
    x = x_ref[0]
    ms = jnp.mean(x * x, axis=-1, keepdims=True)
    h = x * lax.rsqrt(ms + EPS) * g_ref[...]
    h = h * (1.0 + sc_ref[0]) + sh_ref[0]
    hb = h.astype(BF16)
    pm_ref[0] = jnp.dot(hb, wm_ref[...], preferred_element_type=F32)
    pg_ref[0] = jnp.dot(hb, wg_ref[...], preferred_element_type=F32)
    pd_ref[0] = jnp.dot(hb, wd_ref[...], preferred_element_type=F32).astype(BF16)


def _inproj(x, sh, sc, g, wm, wg, wd, tm=512):
    bsz, s, d = x.shape
    tm = min(tm, s)
    nm, ng, nd = wm.shape[1], wg.shape[1], wd.shape[1]
    tok = lambda b, i: (b, i, 0)
    per_b = lambda b, i: (b, 0, 0)
    const = lambda b, i: (0, 0)
    return pl.pallas_call(
        _inproj_kernel,
        grid=(bsz, s // tm),
        in_specs=[pl.BlockSpec((1, tm, d), tok),
                  pl.BlockSpec((1, 1, d), per_b),
                  pl.BlockSpec((1, 1, d), per_b),
                  pl.BlockSpec((1, d), const),
                  pl.BlockSpec((d, nm), const),
                  pl.BlockSpec((d, ng), const),
                  pl.BlockSpec((d, nd), const)],
        out_specs=[pl.BlockSpec((1, tm, nm), tok),
                   pl.BlockSpec((1, tm, ng), tok),
                   pl.BlockSpec((1, tm, nd), tok)],
        out_shape=[jax.ShapeDtypeStruct((bsz, s, nm), F32),
                   jax.ShapeDtypeStruct((bsz, s, ng), F32),
                   jax.ShapeDtypeStruct((bsz, s, nd), BF16)],
        compiler_params=_params("arbitrary", "arbitrary"),
        name="inproj",
    )(x, sh, sc, g, wm, wg, wd)


def _causal_conv(u, tail, w, b):
    ext = jnp.concatenate([tail, u], axis=0)
    y = u * w[CONV_WIDTH - 1:CONV_WIDTH, :]
    for k in range(CONV_WIDTH - 1):
        shift = CONV_WIDTH - 1 - k
        y = y + pltpu.roll(ext, shift, axis=0)[SUBLANES:, :] * w[k:k + 1, :]
    return y + b


def _cumsum_lanes(x):
    lane = lax.broadcasted_iota(jnp.int32, x.shape, 1)
    s = 1
    while s < x.shape[1]:
        x = x + jnp.where(lane >= s, pltpu.roll(x, s, axis=1), 0.0)
        s *= 2
    return x


def _mlstm_kernel(gb_ref, mq_ref, mk_ref, mv_ref, mo_ref, g_ref, cwq_ref, cwk_ref, cbq_ref, cbk_ref,
                  ng_ref, o_ref, b_scr, i_scr, *, n_heads, head_dim, group):
    h0 = pl.program_id(1) * group
    L = M_CHUNK
    hd = head_dim
    nc = b_scr.shape[1]
    silu = lambda a: a * jax.nn.sigmoid(a)

    for u in range(group):
        i_scr[u] = g_ref[0, h0 + u] + gb_ref[h0 + u]
        fpre = g_ref[0, n_heads + h0 + u] + gb_ref[n_heads + h0 + u]
        lf = jnp.minimum(fpre, 0.0) - jnp.log1p(jnp.exp(-jnp.abs(fpre)))
        b_scr[u] = _cumsum_lanes(lf)

    r_i = lax.broadcasted_iota(jnp.int32, (L, L), 0)
    c_i = lax.broadcasted_iota(jnp.int32, (L, L), 1)
    eye = r_i == c_i
    causal = c_i <= r_i

    def to_col(row):
        return jnp.sum(jnp.where(eye, row, 0.0), axis=1, keepdims=True)

    def chunk(c, carry):
        t0 = pl.multiple_of(c * L, L)
        return tuple(head_chunk(c, t0, u, carry[u]) for u in range(group))

    def head_chunk(c, t0, u, carry):
        C, n, m, q_tail, k_tail = carry
        cols = slice(u * hd, (u + 1) * hd)
        q_raw = mq_ref[0, pl.ds(t0, L), cols]
        k_raw = mk_ref[0, pl.ds(t0, L), cols]
        qc = silu(_causal_conv(q_raw, q_tail, cwq_ref[:, cols], cbq_ref[:, cols])).astype(BF16)
        kc = (silu(_causal_conv(k_raw, k_tail, cwk_ref[:, cols], cbk_ref[:, cols])) * (hd ** -0.5)).astype(BF16)
        vc = mv_ref[0, pl.ds(t0, L), cols].astype(BF16)
        b_row = b_scr[u, pl.ds(c, 1), :]
        i_row = i_scr[u, pl.ds(c, 1), :]
        b_col = to_col(b_row)
        i_col = to_col(i_row)

        logD = jnp.where(causal, b_col - b_row + i_row, -jnp.inf)
        m_t = jnp.maximum(b_col + m, jnp.max(logD, axis=1, keepdims=True))
        Dw = jnp.exp(logD - m_t)
        inter = jnp.exp(b_col + m - m_t)
        sqk = lax.dot_general(qc, kc, NT, preferred_element_type=F32) * Dw
        num = (jnp.dot(sqk.astype(BF16), vc, preferred_element_type=F32)
               + inter * jnp.dot(qc, C.astype(BF16), preferred_element_type=F32))
        den = (jnp.sum(sqk, axis=1, keepdims=True)
               + inter * jnp.sum(qc.astype(F32) * n, axis=1, keepdims=True))
        hh = num / jnp.maximum(jnp.abs(den), jnp.exp(-m_t))

        bL = b_row[:, L - 1:L]
        m_new = jnp.maximum(bL + m, jnp.max(bL - b_row + i_row, axis=1, keepdims=True))
        w_col = jnp.exp(bL - b_col + i_col - m_new)
        decay = jnp.exp(bL + m - m_new)
        kw = kc.astype(F32) * w_col
        C_new = decay * C + lax.dot_general(kw.astype(BF16), vc, TN, preferred_element_type=F32)
        n_new = decay * n + jnp.sum(kw, axis=0, keepdims=True)

        y = hh * lax.rsqrt(jnp.mean(hh * hh, axis=-1, keepdims=True) + EPS) * ng_ref[:, cols]
        y = y * jax.nn.sigmoid(mo_ref[0, pl.ds(t0, L), cols])
        o_ref[0, pl.ds(t0, L), cols] = y.astype(o_ref.dtype)
        return C_new, n_new, m_new, q_raw[L - SUBLANES:, :], k_raw[L - SUBLANES:, :]

    zeros = lambda *shape: jnp.zeros(shape, F32)
    init = (zeros(hd, hd), zeros(1, hd), zeros(1, 1), zeros(SUBLANES, hd), zeros(SUBLANES, hd))
    lax.fori_loop(0, nc, chunk, (init,) * group)


def _mlstm(pm, gates, gate_b, conv_w, conv_b, norm_g, n_heads, head_dim, group=2):
    bsz, s, _ = pm.shape
    nc = s // M_CHUNK
    gw = group * head_dim
    n_groups = n_heads // group
    col = lambda off: (lambda b, g: (b, 0, off * n_groups + g))
    wcol = lambda off: (lambda b, g: (0, off * n_groups + g))
    kern = functools.partial(_mlstm_kernel, n_heads=n_heads, head_dim=head_dim, group=group)
    return pl.pallas_call(
        kern,
        grid=(bsz, n_groups),
        in_specs=[pl.BlockSpec(memory_space=pltpu.SMEM),
                  pl.BlockSpec((1, s, gw), col(0)),
                  pl.BlockSpec((1, s, gw), col(1)),
                  pl.BlockSpec((1, s, gw), col(2)),
                  pl.BlockSpec((1, s, gw), col(3)),
                  pl.BlockSpec((1, 2 * n_heads, nc, M_CHUNK), lambda b, g: (b, 0, 0, 0)),
                  pl.BlockSpec((CONV_WIDTH, gw), wcol(0)),
                  pl.BlockSpec((CONV_WIDTH, gw), wcol(1)),
                  pl.BlockSpec((1, gw), wcol(0)),
                  pl.BlockSpec((1, gw), wcol(1)),
                  pl.BlockSpec((1, gw), wcol(0))],
        out_specs=pl.BlockSpec((1, s, gw), col(0)),
        out_shape=jax.ShapeDtypeStruct((bsz, s, n_heads * head_dim), BF16),
        scratch_shapes=[pltpu.VMEM((group, nc, M_CHUNK), F32), pltpu.VMEM((group, nc, M_CHUNK), F32)],
        compiler_params=_params("arbitrary", "arbitrary"),
        name="mlstm",
    )(gate_b, pm, pm, pm, pm, gates, conv_w, conv_w, conv_b, conv_b, norm_g)


def _diffattn_kernel(q_ref, k_ref, v_ref, lam_ref, ng_ref, o_ref, m_scr, l_scr, acc_scr,
                     *, dh, lam_init, group):
    i = pl.program_id(2)
    tq = q_ref.shape[1]
    hw = 2 * dh
    scale = dh ** -0.5
    exact_scale = math.frexp(scale)[0] == 0.5
    lane = lax.broadcasted_iota(jnp.int32, (tq, hw), 1)
    qs = []
    for u in range(group):
        q = q_ref[0, :, u * hw:(u + 1) * hw]
        if exact_scale:
            q = q * scale
        zero = jnp.zeros_like(q)
        qs += [jnp.where(lane < dh, q, zero), jnp.where(lane < dh, zero, q)]
    lp = lam_ref[...]
    lam = (jnp.exp(jnp.sum(lp[0:1] * lp[1:2], axis=1, keepdims=True))
           - jnp.exp(jnp.sum(lp[2:3] * lp[3:4], axis=1, keepdims=True)) + lam_init)
    on_or_below_diag = (lax.broadcasted_iota(jnp.int32, (tq, tq), 1)
                        <= lax.broadcasted_iota(jnp.int32, (tq, tq), 0))

    m_scr[...] = jnp.full(m_scr.shape, -jnp.inf, F32)
    l_scr[...] = jnp.zeros(l_scr.shape, F32)
    acc_scr[...] = jnp.zeros(acc_scr.shape, F32)

    def block(k0, width, diagonal):
        for st in range(2 * group):
            cols = slice((st // 2) * hw, (st // 2 + 1) * hw)
            kb = k_ref[0, pl.ds(k0, width), cols]
            vb = v_ref[0, pl.ds(k0, width), cols]
            s = lax.dot_general(qs[st], kb, NT, preferred_element_type=F32)
            if not exact_scale:
                s = s * scale
            if diagonal:
                s = jnp.where(on_or_below_diag, s, -jnp.inf)
            m_old = m_scr[st]
            m_new = jnp.maximum(m_old, jnp.broadcast_to(jnp.max(s, axis=1, keepdims=True), m_old.shape))
            alpha = jnp.exp(m_old - m_new)
            e = [jnp.exp(s[:, c:c + LANES] - m_new) for c in range(0, width, LANES)]
            l_scr[st] = alpha * l_scr[st] + functools.reduce(jnp.add, e)
            eb = jnp.concatenate([x.astype(BF16) for x in e], axis=1)
            acc_scr[st] = alpha * acc_scr[st] + jnp.dot(eb, vb, preferred_element_type=F32)
            m_scr[st] = m_new

    def below(j, carry):
        block(pl.multiple_of(j * tq, tq), tq, False)
        return carry

    lax.fori_loop(0, i, below, 0)
    block(pl.multiple_of(i * tq, tq), tq, True)
    for u in range(group):
        o1 = acc_scr[2 * u] / jnp.sum(l_scr[2 * u], axis=1, keepdims=True)
        o2 = acc_scr[2 * u + 1] / jnp.sum(l_scr[2 * u + 1], axis=1, keepdims=True)
        o = o1 - lam * o2
        y = o * lax.rsqrt(jnp.mean(o * o, axis=-1, keepdims=True) + EPS) * ng_ref[...] * (1.0 - lam_init)
        o_ref[0, :, u * hw:(u + 1) * hw] = y.astype(o_ref.dtype)


def _diffattn(pd, lam_p, norm_g, n_heads, dh, lam_init, tq=256, group=4):
    bsz, s, _ = pd.shape
    hw = 2 * dh
    gw = group * hw
    n_groups = n_heads // group
    kern = functools.partial(_diffattn_kernel, dh=dh, lam_init=lam_init, group=group)
    return pl.pallas_call(
        kern,
        grid=(bsz, n_groups, s // tq),
        in_specs=[pl.BlockSpec((1, tq, gw), lambda b, g, i: (b, i, g)),
                  pl.BlockSpec((1, s, gw), lambda b, g, i: (b, 0, n_groups + g)),
                  pl.BlockSpec((1, s, gw), lambda b, g, i: (b, 0, 2 * n_groups + g)),
                  pl.BlockSpec((4, dh), lambda b, g, i: (0, 0)),
                  pl.BlockSpec((1, hw), lambda b, g, i: (0, 0))],
        out_specs=pl.BlockSpec((1, tq, gw), lambda b, g, i: (b, i, g)),
        out_shape=jax.ShapeDtypeStruct((bsz, s, n_heads * hw), BF16),
        scratch_shapes=[pltpu.VMEM((2 * group, tq, LANES), F32), pltpu.VMEM((2 * group, tq, LANES), F32),
                        pltpu.VMEM((2 * group, tq, hw), F32)],
        compiler_params=_params("arbitrary", "arbitrary", "arbitrary"),
        name="diffattn",
    )(pd, pd, pd, lam_p, norm_g)


def _outproj_kernel(hm_ref, od_ref, x_ref, g1_ref, sh_ref, sc_ref, ng_ref, wm_ref, wd_ref, wq_ref,
                    x1_ref, h2_ref, q_ref):
    y = (jnp.dot(hm_ref[0], wm_ref[...], preferred_element_type=F32)
         + jnp.dot(od_ref[0], wd_ref[...], preferred_element_type=F32))
    x1 = x_ref[0] + g1_ref[0] * y
    x1_ref[0] = x1
    ms = jnp.mean(x1 * x1, axis=-1, keepdims=True)
    h2 = x1 * lax.rsqrt(ms + EPS) * ng_ref[...]
    h2 = (h2 * (1.0 + sc_ref[0]) + sh_ref[0]).astype(BF16)
    h2_ref[0] = h2
    q_ref[0] = jnp.dot(h2, wq_ref[...], preferred_element_type=F32).astype(BF16)


def _outproj(hm, od, x, g1, sh2, sc2, ng, wm, wd, wq, tm=512):
    bsz, s, d = x.shape
    tm = min(tm, s)
    wm_w, wd_w, nq = hm.shape[2], od.shape[2], wq.shape[1]
    tok = lambda b, i: (b, i, 0)
    per_b = lambda b, i: (b, 0, 0)
    const = lambda b, i: (0, 0)
    return pl.pallas_call(
        _outproj_kernel,
        grid=(bsz, s // tm),
        in_specs=[pl.BlockSpec((1, tm, wm_w), tok),
                  pl.BlockSpec((1, tm, wd_w), tok),
                  pl.BlockSpec((1, tm, d), tok),
                  pl.BlockSpec((1, 1, d), per_b),
                  pl.BlockSpec((1, 1, d), per_b),
                  pl.BlockSpec((1, 1, d), per_b),
                  pl.BlockSpec((1, d), const),
                  pl.BlockSpec((wm_w, d), const),
                  pl.BlockSpec((wd_w, d), const),
                  pl.BlockSpec((d, nq), const)],
        out_specs=[pl.BlockSpec((1, tm, d), tok),
                   pl.BlockSpec((1, tm, d), tok),
                   pl.BlockSpec((1, tm, nq), tok)],
        out_shape=[jax.ShapeDtypeStruct((bsz, s, d), F32),
                   jax.ShapeDtypeStruct((bsz, s, d), BF16),
                   jax.ShapeDtypeStruct((bsz, s, nq), BF16)],
        compiler_params=_params("arbitrary", "arbitrary"),
        name="outproj",
    )(hm, od, x, g1, sh2, sc2, ng, wm, wd, wq)


def _topk_rows(s, k, payload=None):
    n_rows, t = s.shape
    nv = n_rows // SUBLANES
    slab = lambda a, v: a[SUBLANES * v:SUBLANES * (v + 1), :]
    slabs = [slab(s, v) for v in range(nv)]
    sub = lax.broadcasted_iota(jnp.int32, (SUBLANES, t), 0)
    out_row = lax.broadcasted_iota(jnp.int32, (k, t), 0)
    vals = jnp.zeros((k, t), s.dtype)
    rows = jnp.zeros((k, t), jnp.int32)
    pays = None if payload is None else jnp.zeros((k, t), payload.dtype)
    for r in range(k):
        nodes = [(slabs[v], v) for v in range(nv)]
        while len(nodes) > 1:
            nxt = []
            for a in range(0, len(nodes) - 1, 2):
                (va, ia), (vb, ib) = nodes[a], nodes[a + 1]
                c = va >= vb
                nxt.append((jnp.where(c, va, vb), jnp.where(c, ia, ib)))
            if len(nodes) % 2:
                nxt.append(nodes[-1])
            nodes = nxt
        m8, v8 = nodes[0]
        m = jnp.max(m8, axis=0, keepdims=True)
        row8 = v8 * SUBLANES + sub
        idx = jnp.min(jnp.where(m8 == m, row8, n_rows), axis=0, keepdims=True)
        won = jnp.where(row8 == idx, v8, -1)
        hits = [won == v for v in range(nv)]
        vals = jnp.where(out_row == r, m, vals)
        rows = jnp.where(out_row == r, idx, rows)
        if payload is not None:
            picked = [jnp.where(hits[v], slab(payload, v), 0) for v in range(nv)]
            p = jnp.sum(functools.reduce(jnp.add, picked), axis=0, keepdims=True)
            pays = jnp.where(out_row == r, p, pays)
        slabs = [jnp.where(hits[v], -jnp.inf, slabs[v]) for v in range(nv)]
    return vals, rows, pays


def _oddeven_merge_sort(n):
    def merge(lo, hi, r):
        step = 2 * r
        if step < hi - lo:
            yield from merge(lo, hi, step)
            yield from merge(lo + r, hi, step)
            yield from ((i, i + r) for i in range(lo + r, hi - r, step))
        else:
            yield (lo, lo + r)

    def sort(lo, hi):
        if hi > lo:
            mid = lo + (hi - lo) // 2
            yield from sort(lo, mid)
            yield from sort(mid + 1, hi)
            yield from merge(lo, hi, 1)

    return list(sort(0, n - 1))


def _topk_rows_sorted(s, k):
    n_rows, t = s.shape
    nv = n_rows // SUBLANES
    assert nv == k and nv & (nv - 1) == 0
    sub = lax.broadcasted_iota(jnp.int32, (SUBLANES, t), 0)
    out_row = lax.broadcasted_iota(jnp.int32, (k, t), 0)
    val = [s[SUBLANES * v:SUBLANES * (v + 1), :] for v in range(nv)]
    sid = list(range(nv))
    for i, j in _oddeven_merge_sort(nv):
        va, vb, ia, ib = val[i], val[j], sid[i], sid[j]
        if isinstance(ia, int) and isinstance(ib, int):
            first = va >= vb if ia < ib else va > vb
        else:
            first = (va > vb) | ((va == vb) & (ia < ib))
        val[i], val[j] = jnp.where(first, va, vb), jnp.where(first, vb, va)
        sid[i], sid[j] = jnp.where(first, ia, ib), jnp.where(first, ib, ia)
    row = [x * SUBLANES + sub for x in sid]
    vals = jnp.zeros((k, t), s.dtype)
    rows = jnp.zeros((k, t), jnp.int32)
    for r in range(k):
        m = jnp.max(val[0], axis=0, keepdims=True)
        idx = jnp.min(jnp.where(val[0] == m, row[0], n_rows), axis=0, keepdims=True)
        hit = row[0] == idx
        vals = jnp.where(out_row == r, m, vals)
        rows = jnp.where(out_row == r, idx, rows)
        for q in range(k - 1 - r):
            val[q] = jnp.where(hit, val[q + 1], val[q])
            row[q] = jnp.where(hit, row[q + 1], row[q])
    return vals, rows


def _candidate_layout(k):
    slabs, cur = [], []

    def flush():
        n_valid = len(cur)
        while len(cur) < SUBLANES:
            r = len(cur)
            period = 1
            while period < max(b for _, b in cur[:n_valid]) + 1:
                period *= 2
            b = cur[r - period][1] if r >= period else r
            cur.append((cur[n_valid - 1][0], b))
        slabs.append(([a for a, _ in cur], [b for _, b in cur], n_valid))
        cur.clear()

    for a in range(k):
        nb = k // (a + 1)
        for b0 in range(0, nb, SUBLANES):
            group = [(a, b) for b in range(b0, min(nb, b0 + SUBLANES))]
            if cur and (len(cur) + len(group) > SUBLANES or (nb == 1 and a % SUBLANES == 0)):
                flush()
            cur.extend(group)
    if cur:
        flush()
    return slabs


def _rows_by_pattern(x, pattern, sub):
    p0 = pattern[0]
    if p0 % SUBLANES == 0 and pattern == list(range(p0, p0 + SUBLANES)):
        return x[p0:p0 + SUBLANES, :]
    period = SUBLANES
    while period > 1 and all(pattern[r] == pattern[r % (period // 2)] for r in range(SUBLANES)):
        period //= 2
    pos = sub if period == SUBLANES else sub & (period - 1)
    out = x[p0:p0 + 1, :]
    for r in range(1, period):
        if pattern[r] != pattern[r - 1]:
            out = jnp.where(pos >= r, x[pattern[r]:pattern[r] + 1, :], out)
    return jnp.broadcast_to(out, (SUBLANES, x.shape[1]))


def _route_kernel(q_ref, keys_ref, gm_ref, sv_scr, si_scr, it_scr, jt_scr, gt_scr, il_scr, jl_scr, gl_scr,
                  w_scr, *, n_heads, n_keys, stride):
    K = P_TOPK
    t = q_ref.shape[0]
    slots = n_heads * K
    step = pl.program_id(0)
    cur = step % 2
    prev = 1 - cur
    sub = lax.broadcasted_iota(jnp.int32, (SUBLANES, t), 0)
    key_row = lax.broadcasted_iota(jnp.int32, (n_keys, slots), 0)
    layout = _candidate_layout(K)
    ways2 = 4
    n_loop = n_heads // ways2
    blk_per_iter = n_keys // n_loop

    @pl.when(step == 0)
    def _():
        il_scr[1] = jnp.zeros((t, slots), jnp.int32)
        jl_scr[1] = jnp.zeros((t, slots), jnp.int32)
        gl_scr[1] = jnp.zeros((t, slots), F32)

    def gate_rows(tok):
        i_row = il_scr[prev, pl.ds(tok, 1), :]
        j_row = jl_scr[prev, pl.ds(tok, 1), :]
        g_row = gl_scr[prev, pl.ds(tok, 1), :]
        a_t = jnp.where(key_row == i_row, 1.0, 0.0).astype(BF16)
        b_t = jnp.where(key_row == j_row, g_row, 0.0).astype(BF16)
        m_t = lax.dot_general(a_t, b_t, NT, preferred_element_type=F32)
        w_scr[pl.ds(tok, n_keys, stride=stride), :] = m_t

    ways = 4
    n_loop1 = 2 * n_heads // ways
    tok_per_iter = t // n_loop1

    def stage1(hh, _):
        for p in range(ways):
            hp = ways * hh + p
            c0 = pl.multiple_of(hp * LANES, LANES)
            qb = q_ref[:, pl.ds(c0, LANES)]
            st = lax.dot_general(keys_ref[hp], qb, NT, preferred_element_type=F32)
            v, ix = _topk_rows_sorted(st, K)
            sv_scr[hp] = v
            si_scr[hp] = ix
        t0 = pl.multiple_of(hh * tok_per_iter, tok_per_iter)
        for u in range(tok_per_iter):
            gate_rows(t0 + u)
        return 0

    lax.fori_loop(0, n_loop1, stage1, 0)

    def stage2(hh, _):
        b0 = hh * blk_per_iter
        for u in range(blk_per_iter):
            r0 = pl.multiple_of((b0 + u) * stride, SUBLANES)
            gm_ref[b0 + u] = w_scr[pl.ds(r0, t), :].astype(gm_ref.dtype)
        for u in range(ways2):
            h = ways2 * hh + u
            sv0, sv1 = sv_scr[2 * h], sv_scr[2 * h + 1]
            si0, si1 = si_scr[2 * h], si_scr[2 * h + 1]
            cand, cidx = [], []
            for a_pat, b_pat, n_valid in layout:
                c = _rows_by_pattern(sv0, a_pat, sub) + _rows_by_pattern(sv1, b_pat, sub)
                if n_valid < SUBLANES:
                    c = jnp.where(sub < n_valid, c, -jnp.inf)
                cand.append(c)
                cidx.append(_rows_by_pattern(si0, a_pat, sub) * n_keys + _rows_by_pattern(si1, b_pat, sub))
            fv, _, eidx = _topk_rows(jnp.concatenate(cand, axis=0), K, jnp.concatenate(cidx, axis=0))
            e = jnp.exp(fv - fv[0:1, :])
            g = e / jnp.sum(e, axis=0, keepdims=True)
            r0 = pl.multiple_of(h * K, K)
            it_scr[pl.ds(r0, K), :] = eidx // n_keys
            jt_scr[pl.ds(r0, K), :] = eidx % n_keys
            gt_scr[pl.ds(r0, K), :] = g
        return 0

    lax.fori_loop(0, n_loop, stage2, 0)
    il_scr[cur] = it_scr[...].T
    jl_scr[cur] = jt_scr[...].T
    gl_scr[cur] = gt_scr[...].T


def _route(q, keys, n_heads, n_keys):
    n = q.shape[0]
    t = LANES
    n_tiles = n // t
    slots = n_heads * P_TOPK
    stride = t + SUBLANES
    kern = functools.partial(_route_kernel, n_heads=n_heads, n_keys=n_keys, stride=stride)
    return pl.pallas_call(
        kern,
        grid=(n_tiles + 1,),
        in_specs=[pl.BlockSpec((t, q.shape[1]), lambda s: (jnp.minimum(s, n_tiles - 1), 0)),
                  pl.BlockSpec(keys.shape, lambda s: (0, 0, 0))],
        out_specs=pl.BlockSpec((n_keys, t, n_keys), lambda s: (0, jnp.maximum(s - 1, 0), 0)),
        out_shape=jax.ShapeDtypeStruct((n_keys, n, n_keys), BF16),
        scratch_shapes=[pltpu.VMEM((2 * n_heads, P_TOPK, t), F32),
                        pltpu.VMEM((2 * n_heads, P_TOPK, t), jnp.int32),
                        pltpu.VMEM((slots, t), jnp.int32),
                        pltpu.VMEM((slots, t), jnp.int32),
                        pltpu.VMEM((slots, t), F32),
                        pltpu.VMEM((2, t, slots), jnp.int32),
                        pltpu.VMEM((2, t, slots), jnp.int32),
                        pltpu.VMEM((2, t, slots), F32),
                        pltpu.VMEM((n_keys * stride, n_keys), F32)],
        compiler_params=_params("arbitrary"),
        name="route",
    )(q, keys)


def _experts_kernel(h2_ref, u_ref, v_ref, m_ref, x1_ref, g2_ref, fg_ref, o_ref, acc_ref, *, final):
    j = pl.program_id(2)

    @pl.when(j == 0)
    def _():
        acc_ref[...] = jnp.zeros_like(acc_ref)

    s = lax.dot_general(h2_ref[0], u_ref[...].astype(BF16), NT, preferred_element_type=F32)
    act = 0.5 * s * (1.0 + lax.erf(s * (2.0 ** -0.5)))
    gate = jnp.concatenate([m_ref[ib] for ib in range(m_ref.shape[0])], axis=1)
    w = (gate.astype(F32) * act).astype(BF16)
    acc_ref[...] += jnp.dot(w, v_ref[...].astype(BF16), preferred_element_type=F32)

    @pl.when(j == pl.num_programs(2) - 1)
    def _():
        x2 = x1_ref[0] + g2_ref[0] * acc_ref[...]
        if final:
            x2 = x2 * lax.rsqrt(jnp.mean(x2 * x2, axis=-1, keepdims=True) + EPS) * fg_ref[...]
        o_ref[0] = x2


def _experts(h2, u, v, gmat, x1, g2, final_g, final, tm=1024, te=1024):
    bsz, s, d = x1.shape
    tm = min(tm, s)
    ne = u.shape[0]
    n_keys = gmat.shape[2]
    n_tiles = s // tm
    kern = functools.partial(_experts_kernel, final=final)
    return pl.pallas_call(
        kern,
        grid=(bsz, n_tiles, ne // te),
        in_specs=[pl.BlockSpec((1, tm, d), lambda b, i, j: (b, i, 0)),
                  pl.BlockSpec((te, d), lambda b, i, j: (j, 0)),
                  pl.BlockSpec((te, d), lambda b, i, j: (j, 0)),
                  pl.BlockSpec((te // n_keys, tm, n_keys), lambda b, i, j: (j, b * n_tiles + i, 0)),
                  pl.BlockSpec((1, tm, d), lambda b, i, j: (b, i, 0)),
                  pl.BlockSpec((1, 1, d), lambda b, i, j: (b, 0, 0)),
                  pl.BlockSpec((1, d), lambda b, i, j: (0, 0))],
        out_specs=pl.BlockSpec((1, tm, d), lambda b, i, j: (b, i, 0)),
        out_shape=jax.ShapeDtypeStruct((bsz, s, d), F32),
        scratch_shapes=[pltpu.VMEM((tm, d), F32)],
        compiler_params=_params("arbitrary", "arbitrary", "arbitrary"),
        name="experts",
    )(h2, u, v, gmat, x1, g2, final_g)


def kernel(x, c, ada_w, ada_b, norm1_g, w_in, conv_w, conv_b, mlstm_gate_b, mlstm_norm_g, lambda_q1,
           lambda_k1, lambda_q2, lambda_k2, diff_norm_g, w_out, norm2_g, peer_w_query, peer_sub_keys,
           peer_u, peer_v, final_g):
    bsz, s, d = x.shape
    depth = ada_w.shape[0]
    m_heads = mlstm_gate_b.shape[1] // 2
    m_width = mlstm_norm_g.shape[1]
    m_hd = m_width // m_heads
    dh = lambda_q1.shape[1]
    d_width = w_out.shape[1] - m_width
    d_heads = d_width // (2 * dh)
    p_heads, _, n_keys, _ = peer_sub_keys.shape[1:]
    n_m = 4 * m_width
    n_g = 2 * m_heads
    assert s % M_CHUNK == 0 and m_hd == LANES and 2 * dh == LANES and n_keys == LANES
    assert peer_sub_keys.shape[-1] == LANES and p_heads % 2 == 0 and n_keys == SUBLANES * P_TOPK

    for l in range(depth):
        mod = _adaln(c, ada_w[l], ada_b[l])
        sh1, sc1, g1, sh2, sc2, g2 = (a.reshape(bsz, 1, d) for a in jnp.split(mod, 6, axis=-1))

        w = w_in[l].astype(BF16)
        w_m = w[:, :n_m]
        w_g = jnp.pad(w[:, n_m:n_m + n_g], ((0, 0), (0, LANES - n_g)))
        w_d = w[:, n_m + n_g:]
        pm, pg, pd = _inproj(x, sh1, sc1, norm1_g[l].reshape(1, d), w_m, w_g, w_d)

        gates = pg[:, :, :n_g].transpose(0, 2, 1).reshape(bsz, n_g, s // M_CHUNK, M_CHUNK)
        hm = _mlstm(pm, gates, mlstm_gate_b[l], conv_w[l], conv_b[l].reshape(1, -1),
                    mlstm_norm_g[l].reshape(1, -1), m_heads, m_hd)

        lam_init = 0.8 - 0.6 * math.exp(-0.3 * l)
        lam_p = jnp.stack([lambda_q1[l], lambda_k1[l], lambda_q2[l], lambda_k2[l]])
        od = _diffattn(pd, lam_p, diff_norm_g[l].reshape(1, -1), d_heads, dh, lam_init)

        wo = w_out[l].astype(BF16)
        x1, h2, q = _outproj(hm, od, x, g1, sh2, sc2, norm2_g[l].reshape(1, d), wo[:m_width], wo[m_width:],
                             peer_w_query[l].astype(BF16))

        keys = peer_sub_keys[l].reshape(2 * p_heads, n_keys, -1).astype(BF16)
        gmat = _route(q.reshape(bsz * s, -1), keys, p_heads, n_keys)
        x = _experts(h2, peer_u[l], peer_v[l], gmat, x1, g2,
                     final_g.reshape(1, d), final=(l == depth - 1))
    return x
```

```python
import functools
import math

import jax
import jax.numpy as jnp
from jax import lax
from jax.experimental import pallas as pl
from jax.experimental.pallas import tpu as pltpu

F32 = jnp.float32
BF16 = jnp.bfloat16
EPS = 1e-6
LANES = 128
SUBLANES = 8
VMEM_LIMIT = 56 * 1024 * 1024
M_CHUNK = 128
CONV_WIDTH = 4
P_TOPK = 16

NT = (((1,), (1,)), ((), ()))
TN = (((0,), (0,)), ((), ()))


def _params(*sem):
    return pltpu.CompilerParams(dimension_semantics=sem, vmem_limit_bytes=VMEM_LIMIT)


def _adaln_kernel(c_ref, w_ref, b_ref, o_ref):
    c = c_ref[...]
    sc = (c * jax.nn.sigmoid(c)).astype(BF16)
    o_ref[...] = jnp.dot(sc, w_ref[...].astype(BF16), preferred_element_type=F32) + b_ref[...]


def _adaln(c, w, b):
    bsz, d = c.shape
    n = w.shape[1]
    tn = 1536
    return pl.pallas_call(
        _adaln_kernel,
        grid=(n // tn,),
        in_specs=[pl.BlockSpec((bsz, d), lambda j: (0, 0)),
                  pl.BlockSpec((d, tn), lambda j: (0, j)),
                  pl.BlockSpec((1, tn), lambda j: (0, j))],
        out_specs=pl.BlockSpec((bsz, tn), lambda j: (0, j)),
        out_shape=jax.ShapeDtypeStruct((bsz, n), F32),
        compiler_params=_params("arbitrary"),
        name="adaln",
    )(c, w, b.reshape(1, n))


def _inproj_kernel(x_ref, sh_ref, sc_ref, g_ref, wm_ref, wg_ref, wd_ref, pm_ref, pg_ref, pd_ref):
    x = x_ref[0]
    ms = jnp.mean(x * x, axis=-1, keepdims=True)
    h = x * lax.rsqrt(ms + EPS) * g_ref[...]
    h = h * (1.0 + sc_ref[0]) + sh_ref[0]
    hb = h.astype(BF16)
    pm_ref[0] = jnp.dot(hb, wm_ref[...], preferred_element_type=F32)
    pg_ref[0] = jnp.dot(hb, wg_ref[...], preferred_element_type=F32)
    pd_ref[0] = jnp.dot(hb, wd_ref[...], preferred_element_type=F32).astype(BF16)


def _inproj(x, sh, sc, g, wm, wg, wd, tm=512):
    bsz, s, d = x.shape
    tm = min(tm, s)
    nm, ng, nd = wm.shape[1], wg.shape[1], wd.shape[1]
    tok = lambda b, i: (b, i, 0)
    per_b = lambda b, i: (b, 0, 0)
    const = lambda b, i: (0, 0)
    return pl.pallas_call(
        _inproj_kernel,
        grid=(bsz, s // tm),
        in_specs=[pl.BlockSpec((1, tm, d), tok),
                  pl.BlockSpec((1, 1, d), per_b),
                  pl.BlockSpec((1, 1, d), per_b),
                  pl.BlockSpec((1, d), const),
                  pl.BlockSpec((d, nm), const),
                  pl.BlockSpec((d, ng), const),
                  pl.BlockSpec((d, nd), const)],
        out_specs=[pl.BlockSpec((1, tm, nm), tok),
                   pl.BlockSpec((1, tm, ng), tok),
                   pl.BlockSpec((1, tm, nd), tok)],
        out_shape=[jax.ShapeDtypeStruct((bsz, s, nm), F32),
                   jax.ShapeDtypeStruct((bsz, s, ng), F32),
                   jax.ShapeDtypeStruct((bsz, s, nd), BF16)],
        compiler_params=_params("arbitrary", "arbitrary"),
        name="inproj",
    )(x, sh, sc, g, wm, wg, wd)


def _causal_conv(u, tail, w, b):
    ext = jnp.concatenate([tail, u], axis=0)
    y = u * w[CONV_WIDTH - 1:CONV_WIDTH, :]
    for k in range(CONV_WIDTH - 1):
        shift = CONV_WIDTH - 1 - k
        y = y + pltpu.roll(ext, shift, axis=0)[SUBLANES:, :] * w[k:k + 1, :]
    return y + b


def _cumsum_lanes(x):
    lane = lax.broadcasted_iota(jnp.int32, x.shape, 1)
    s = 1
    while s < x.shape[1]:
        x = x + jnp.where(lane >= s, pltpu.roll(x, s, axis=1), 0.0)
        s *= 2
    return x


def _mlstm_kernel(gb_ref, mq_ref, mk_ref, mv_ref, mo_ref, g_ref, cwq_ref, cwk_ref, cbq_ref, cbk_ref,
                  ng_ref, o_ref, b_scr, i_scr, c_scr, *, n_heads, head_dim, group):
    h0 = pl.program_id(1) * group
    L = M_CHUNK
    hd = head_dim
    nc = b_scr.shape[1]
    silu = lambda a: a * jax.nn.sigmoid(a)

    for u in range(group):
        i_scr[u] = g_ref[0, h0 + u] + gb_ref[h0 + u]
        fpre = g_ref[0, n_heads + h0 + u] + gb_ref[n_heads + h0 + u]
        lf = jnp.minimum(fpre, 0.0) - jnp.log1p(jnp.exp(-jnp.abs(fpre)))
        b_scr[u] = _cumsum_lanes(lf)

    r_i = lax.broadcasted_iota(jnp.int32, (L, L), 0)
    c_i = lax.broadcasted_iota(jnp.int32, (L, L), 1)
    eye = r_i == c_i
    causal = c_i <= r_i

    def to_col(row):
        return jnp.sum(jnp.where(eye, row, 0.0), axis=1, keepdims=True)

    def chunk(c, carry):
        t0 = pl.multiple_of(c * L, L)
        return tuple(head_chunk(c, t0, u, carry[u]) for u in range(group))

    def head_chunk(c, t0, u, carry):
        n, m, q_tail, k_tail = carry
        cols = slice(u * hd, (u + 1) * hd)
        q_raw = mq_ref[0, pl.ds(t0, L), cols]
        k_raw = mk_ref[0, pl.ds(t0, L), cols]
        qc = silu(_causal_conv(q_raw, q_tail, cwq_ref[:, cols], cbq_ref[:, cols])).astype(BF16)
        kc = (silu(_causal_conv(k_raw, k_tail, cwk_ref[:, cols], cbk_ref[:, cols])) * (hd ** -0.5)).astype(BF16)
        vc = mv_ref[0, pl.ds(t0, L), cols].astype(BF16)
        b_row = b_scr[u, pl.ds(c, 1), :]
        i_row = i_scr[u, pl.ds(c, 1), :]
        b_col = to_col(b_row)
        i_col = to_col(i_row)

        logD = jnp.where(causal, b_col - b_row + i_row, -jnp.inf)
        m_t = jnp.maximum(b_col + m, jnp.max(logD, axis=1, keepdims=True))
        Dw = jnp.exp(logD - m_t)
        inter = jnp.exp(b_col + m - m_t)
        sqk = lax.dot_general(qc, kc, NT, preferred_element_type=F32) * Dw
        num = (jnp.dot(sqk.astype(BF16), vc, preferred_element_type=F32)
               + inter * jnp.dot(qc, c_scr[u].astype(BF16), preferred_element_type=F32))
        den = (jnp.sum(sqk, axis=1, keepdims=True)
               + inter * jnp.sum(qc.astype(F32) * n, axis=1, keepdims=True))
        hh = num / jnp.maximum(jnp.abs(den), jnp.exp(-m_t))

        bL = b_row[:, L - 1:L]
        m_new = jnp.maximum(bL + m, jnp.max(bL - b_row + i_row, axis=1, keepdims=True))
        w_col = jnp.exp(bL - b_col + i_col - m_new)
        decay = jnp.exp(bL + m - m_new)
        kw = kc.astype(F32) * w_col
        c_scr[u] = decay * c_scr[u] + lax.dot_general(kw.astype(BF16), vc, TN, preferred_element_type=F32)
        n_new = decay * n + jnp.sum(kw, axis=0, keepdims=True)

        y = hh * lax.rsqrt(jnp.mean(hh * hh, axis=-1, keepdims=True) + EPS) * ng_ref[:, cols]
        y = y * jax.nn.sigmoid(mo_ref[0, pl.ds(t0, L), cols])
        o_ref[0, pl.ds(t0, L), cols] = y.astype(o_ref.dtype)
        return n_new, m_new, q_raw[L - SUBLANES:, :], k_raw[L - SUBLANES:, :]

    zeros = lambda *shape: jnp.zeros(shape, F32)
    c_scr[...] = zeros(*c_scr.shape)
    init = (zeros(1, hd), zeros(1, 1), zeros(SUBLANES, hd), zeros(SUBLANES, hd))
    lax.fori_loop(0, nc, chunk, (init,) * group)


def _mlstm(pm, gates, gate_b, conv_w, conv_b, norm_g, n_heads, head_dim, group=2):
    bsz, s, _ = pm.shape
    nc = s // M_CHUNK
    gw = group * head_dim
    n_groups = n_heads // group
    col = lambda off: (lambda b, g: (b, 0, off * n_groups + g))
    wcol = lambda off: (lambda b, g: (0, off * n_groups + g))
    kern = functools.partial(_mlstm_kernel, n_heads=n_heads, head_dim=head_dim, group=group)
    return pl.pallas_call(
        kern,
        grid=(bsz, n_groups),
        in_specs=[pl.BlockSpec(memory_space=pltpu.SMEM),
                  pl.BlockSpec((1, s, gw), col(0)),
                  pl.BlockSpec((1, s, gw), col(1)),
                  pl.BlockSpec((1, s, gw), col(2)),
                  pl.BlockSpec((1, s, gw), col(3)),
                  pl.BlockSpec((1, 2 * n_heads, nc, M_CHUNK), lambda b, g: (b, 0, 0, 0)),
                  pl.BlockSpec((CONV_WIDTH, gw), wcol(0)),
                  pl.BlockSpec((CONV_WIDTH, gw), wcol(1)),
                  pl.BlockSpec((1, gw), wcol(0)),
                  pl.BlockSpec((1, gw), wcol(1)),
                  pl.BlockSpec((1, gw), wcol(0))],
        out_specs=pl.BlockSpec((1, s, gw), col(0)),
        out_shape=jax.ShapeDtypeStruct((bsz, s, n_heads * head_dim), BF16),
        scratch_shapes=[pltpu.VMEM((group, nc, M_CHUNK), F32), pltpu.VMEM((group, nc, M_CHUNK), F32),
                        pltpu.VMEM((group, head_dim, head_dim), F32)],
        compiler_params=_params("arbitrary", "arbitrary"),
        name="mlstm",
    )(gate_b, pm, pm, pm, pm, gates, conv_w, conv_w, conv_b, conv_b, norm_g)


def _diffattn_kernel(q_ref, k_ref, v_ref, lam_ref, ng_ref, o_ref, m_scr, l_scr, acc_scr,
                     *, dh, lam_init, group):
    i = pl.program_id(2)
    tq = q_ref.shape[1]
    hw = 2 * dh
    scale = dh ** -0.5
    exact_scale = math.frexp(scale)[0] == 0.5
    lane = lax.broadcasted_iota(jnp.int32, (tq, hw), 1)
    qs = []
    for u in range(group):
        q = q_ref[0, :, u * hw:(u + 1) * hw]
        if exact_scale:
            q = q * scale
        zero = jnp.zeros_like(q)
        qs += [jnp.where(lane < dh, q, zero), jnp.where(lane < dh, zero, q)]
    lp = lam_ref[...]
    lam = (jnp.exp(jnp.sum(lp[0:1] * lp[1:2], axis=1, keepdims=True))
           - jnp.exp(jnp.sum(lp[2:3] * lp[3:4], axis=1, keepdims=True)) + lam_init)
    on_or_below_diag = (lax.broadcasted_iota(jnp.int32, (tq, tq), 1)
                        <= lax.broadcasted_iota(jnp.int32, (tq, tq), 0))

    def block(k0, width, diagonal):
        for st in range(2 * group):
            cols = slice((st // 2) * hw, (st // 2 + 1) * hw)
            kb = k_ref[0, pl.ds(k0, width), cols]
            vb = v_ref[0, pl.ds(k0, width), cols]
            s = lax.dot_general(qs[st], kb, NT, preferred_element_type=F32)
            if not exact_scale:
                s = s * scale
            row_max = lambda a: jnp.broadcast_to(jnp.max(a, axis=1, keepdims=True), (tq, LANES))
            if diagonal:
                s = jnp.where(on_or_below_diag, s, -jnp.inf)
                m_new = row_max(s)
            else:
                m_old = m_scr[st]
                m_new = jnp.maximum(m_old, row_max(s))
                alpha = jnp.exp(m_old - m_new)
            e = [jnp.exp(s[:, c:c + LANES] - m_new) for c in range(0, width, LANES)]
            part = functools.reduce(jnp.add, e)
            pv = jnp.dot(jnp.concatenate([x.astype(BF16) for x in e], axis=1), vb, preferred_element_type=F32)
            l_scr[st] = part if diagonal else alpha * l_scr[st] + part
            acc_scr[st] = pv if diagonal else alpha * acc_scr[st] + pv
            m_scr[st] = m_new

    def below(j, carry):
        block(pl.multiple_of(j * tq, tq), tq, False)
        return carry

    block(pl.multiple_of(i * tq, tq), tq, True)
    lax.fori_loop(0, i, below, 0)
    for u in range(group):
        o1 = acc_scr[2 * u] / jnp.sum(l_scr[2 * u], axis=1, keepdims=True)
        o2 = acc_scr[2 * u + 1] / jnp.sum(l_scr[2 * u + 1], axis=1, keepdims=True)
        o = o1 - lam * o2
        y = o * lax.rsqrt(jnp.mean(o * o, axis=-1, keepdims=True) + EPS) * ng_ref[...] * (1.0 - lam_init)
        o_ref[0, :, u * hw:(u + 1) * hw] = y.astype(o_ref.dtype)


def _diffattn(pd, lam_p, norm_g, n_heads, dh, lam_init, tq=512, group=4):
    bsz, s, _ = pd.shape
    hw = 2 * dh
    gw = group * hw
    n_groups = n_heads // group
    kern = functools.partial(_diffattn_kernel, dh=dh, lam_init=lam_init, group=group)
    return pl.pallas_call(
        kern,
        grid=(bsz, n_groups, s // tq),
        in_specs=[pl.BlockSpec((1, tq, gw), lambda b, g, i: (b, i, g)),
                  pl.BlockSpec((1, s, gw), lambda b, g, i: (b, 0, n_groups + g)),
                  pl.BlockSpec((1, s, gw), lambda b, g, i: (b, 0, 2 * n_groups + g)),
                  pl.BlockSpec((4, dh), lambda b, g, i: (0, 0)),
                  pl.BlockSpec((1, hw), lambda b, g, i: (0, 0))],
        out_specs=pl.BlockSpec((1, tq, gw), lambda b, g, i: (b, i, g)),
        out_shape=jax.ShapeDtypeStruct((bsz, s, n_heads * hw), BF16),
        scratch_shapes=[pltpu.VMEM((2 * group, tq, LANES), F32), pltpu.VMEM((2 * group, tq, LANES), F32),
                        pltpu.VMEM((2 * group, tq, hw), F32)],
        compiler_params=_params("arbitrary", "arbitrary", "arbitrary"),
        name="diffattn",
    )(pd, pd, pd, lam_p, norm_g)


def _outproj_kernel(hm_ref, od_ref, x_ref, g1_ref, sh_ref, sc_ref, ng_ref, wm_ref, wd_ref, wq_ref,
                    x1_ref, h2_ref, q_ref):
    y = (jnp.dot(hm_ref[0], wm_ref[...], preferred_element_type=F32)
         + jnp.dot(od_ref[0], wd_ref[...], preferred_element_type=F32))
    x1 = x_ref[0] + g1_ref[0] * y
    x1_ref[0] = x1
    ms = jnp.mean(x1 * x1, axis=-1, keepdims=True)
    h2 = x1 * lax.rsqrt(ms + EPS) * ng_ref[...]
    h2 = (h2 * (1.0 + sc_ref[0]) + sh_ref[0]).astype(BF16)
    h2_ref[0] = h2
    q_ref[0] = jnp.dot(h2, wq_ref[...], preferred_element_type=F32).astype(BF16)


def _outproj(hm, od, x, g1, sh2, sc2, ng, wm, wd, wq, tm=512):
    bsz, s, d = x.shape
    tm = min(tm, s)
    wm_w, wd_w, nq = hm.shape[2], od.shape[2], wq.shape[1]
    tok = lambda b, i: (b, i, 0)
    per_b = lambda b, i: (b, 0, 0)
    const = lambda b, i: (0, 0)
    return pl.pallas_call(
        _outproj_kernel,
        grid=(bsz, s // tm),
        in_specs=[pl.BlockSpec((1, tm, wm_w), tok),
                  pl.BlockSpec((1, tm, wd_w), tok),
                  pl.BlockSpec((1, tm, d), tok),
                  pl.BlockSpec((1, 1, d), per_b),
                  pl.BlockSpec((1, 1, d), per_b),
                  pl.BlockSpec((1, 1, d), per_b),
                  pl.BlockSpec((1, d), const),
                  pl.BlockSpec((wm_w, d), const),
                  pl.BlockSpec((wd_w, d), const),
                  pl.BlockSpec((d, nq), const)],
        out_specs=[pl.BlockSpec((1, tm, d), tok),
                   pl.BlockSpec((1, tm, d), tok),
                   pl.BlockSpec((1, tm, nq), tok)],
        out_shape=[jax.ShapeDtypeStruct((bsz, s, d), F32),
                   jax.ShapeDtypeStruct((bsz, s, d), BF16),
                   jax.ShapeDtypeStruct((bsz, s, nq), BF16)],
        compiler_params=_params("arbitrary", "arbitrary"),
        name="outproj",
    )(hm, od, x, g1, sh2, sc2, ng, wm, wd, wq)


def _topk_rows(s, k, payload=None):
    n_rows, t = s.shape
    nv = n_rows // SUBLANES
    slab = lambda a, v: a[SUBLANES * v:SUBLANES * (v + 1), :]
    slabs = [slab(s, v) for v in range(nv)]
    sub = lax.broadcasted_iota(jnp.int32, (SUBLANES, t), 0)
    out_row = lax.broadcasted_iota(jnp.int32, (k, t), 0)
    vals = jnp.zeros((k, t), s.dtype)
    rows = jnp.zeros((k, t), jnp.int32)
    pays = None if payload is None else jnp.zeros((k, t), payload.dtype)
    for r in range(k):
        nodes = [(slabs[v], v) for v in range(nv)]
        while len(nodes) > 1:
            nxt = []
            for a in range(0, len(nodes) - 1, 2):
                (va, ia), (vb, ib) = nodes[a], nodes[a + 1]
                c = va >= vb
                nxt.append((jnp.where(c, va, vb), jnp.where(c, ia, ib)))
            if len(nodes) % 2:
                nxt.append(nodes[-1])
            nodes = nxt
        m8, v8 = nodes[0]
        m = jnp.max(m8, axis=0, keepdims=True)
        row8 = v8 * SUBLANES + sub
        idx = jnp.min(jnp.where(m8 == m, row8, n_rows), axis=0, keepdims=True)
        won = jnp.where(row8 == idx, v8, -1)
        hits = [won == v for v in range(nv)]
        vals = jnp.where(out_row == r, m, vals)
        rows = jnp.where(out_row == r, idx, rows)
        if payload is not None:
            picked = [jnp.where(hits[v], slab(payload, v), 0) for v in range(nv)]
            p = jnp.sum(functools.reduce(jnp.add, picked), axis=0, keepdims=True)
            pays = jnp.where(out_row == r, p, pays)
        slabs = [jnp.where(hits[v], -jnp.inf, slabs[v]) for v in range(nv)]
    return vals, rows, pays


def _oddeven_merge_sort(n):
    def merge(lo, hi, r):
        step = 2 * r
        if step < hi - lo:
            yield from merge(lo, hi, step)
            yield from merge(lo + r, hi, step)
            yield from ((i, i + r) for i in range(lo + r, hi - r, step))
        else:
            yield (lo, lo + r)

    def sort(lo, hi):
        if hi > lo:
            mid = lo + (hi - lo) // 2
            yield from sort(lo, mid)
            yield from sort(mid + 1, hi)
            yield from merge(lo, hi, 1)

    return list(sort(0, n - 1))


def _topk_rows_sorted(s, k):
    n_rows, t = s.shape
    nv = n_rows // SUBLANES
    assert nv == k and nv & (nv - 1) == 0
    sub = lax.broadcasted_iota(jnp.int32, (SUBLANES, t), 0)
    out_row = lax.broadcasted_iota(jnp.int32, (k, t), 0)
    val = [s[SUBLANES * v:SUBLANES * (v + 1), :] for v in range(nv)]
    sid = list(range(nv))
    for i, j in _oddeven_merge_sort(nv):
        va, vb, ia, ib = val[i], val[j], sid[i], sid[j]
        if isinstance(ia, int) and isinstance(ib, int):
            first = va >= vb if ia < ib else va > vb
        else:
            first = (va > vb) | ((va == vb) & (ia < ib))
        val[i], val[j] = jnp.where(first, va, vb), jnp.where(first, vb, va)
        sid[i], sid[j] = jnp.where(first, ia, ib), jnp.where(first, ib, ia)
    row = [x * SUBLANES + sub for x in sid]
    vals = jnp.zeros((k, t), s.dtype)
    rows = jnp.zeros((k, t), jnp.int32)
    for r in range(k):
        m = jnp.max(val[0], axis=0, keepdims=True)
        idx = jnp.min(jnp.where(val[0] == m, row[0], n_rows), axis=0, keepdims=True)
        hit = row[0] == idx
        vals = jnp.where(out_row == r, m, vals)
        rows = jnp.where(out_row == r, idx, rows)
        for q in range(k - 1 - r):
            val[q] = jnp.where(hit, val[q + 1], val[q])
            row[q] = jnp.where(hit, row[q + 1], row[q])
    return vals, rows


def _candidate_layout(k):
    slabs, cur = [], []

    def flush():
        n_valid = len(cur)
        while len(cur) < SUBLANES:
            r = len(cur)
            period = 1
            while period < max(b for _, b in cur[:n_valid]) + 1:
                period *= 2
            b = cur[r - period][1] if r >= period else r
            cur.append((cur[n_valid - 1][0], b))
        slabs.append(([a for a, _ in cur], [b for _, b in cur], n_valid))
        cur.clear()

    for a in range(k):
        nb = k // (a + 1)
        for b0 in range(0, nb, SUBLANES):
            group = [(a, b) for b in range(b0, min(nb, b0 + SUBLANES))]
            if cur and (len(cur) + len(group) > SUBLANES or (nb == 1 and a % SUBLANES == 0)):
                flush()
            cur.extend(group)
    if cur:
        flush()
    return slabs


def _rows_by_pattern(x, pattern, sub):
    p0 = pattern[0]
    if p0 % SUBLANES == 0 and pattern == list(range(p0, p0 + SUBLANES)):
        return x[p0:p0 + SUBLANES, :]
    period = SUBLANES
    while period > 1 and all(pattern[r] == pattern[r % (period // 2)] for r in range(SUBLANES)):
        period //= 2
    pos = sub if period == SUBLANES else sub & (period - 1)
    out = x[p0:p0 + 1, :]
    for r in range(1, period):
        if pattern[r] != pattern[r - 1]:
            out = jnp.where(pos >= r, x[pattern[r]:pattern[r] + 1, :], out)
    return jnp.broadcast_to(out, (SUBLANES, x.shape[1]))


def _route_kernel(q_ref, keys_ref, gm_ref, sv_scr, si_scr, it_scr, jt_scr, gt_scr, il_scr, jl_scr, gl_scr,
                  w_scr, *, n_heads, n_keys, stride):
    K = P_TOPK
    t = q_ref.shape[0]
    slots = n_heads * K
    step = pl.program_id(0)
    cur = step % 2
    prev = 1 - cur
    sub = lax.broadcasted_iota(jnp.int32, (SUBLANES, t), 0)
    key_row = lax.broadcasted_iota(jnp.int32, (n_keys, slots), 0)
    layout = _candidate_layout(K)
    ways2 = 4
    n_loop = n_heads // ways2
    blk_per_iter = n_keys // n_loop

    @pl.when(step == 0)
    def _():
        il_scr[1] = jnp.zeros((t, slots), jnp.int32)
        jl_scr[1] = jnp.zeros((t, slots), jnp.int32)
        gl_scr[1] = jnp.zeros((t, slots), F32)

    def gate_rows(tok):
        i_row = il_scr[prev, pl.ds(tok, 1), :]
        j_row = jl_scr[prev, pl.ds(tok, 1), :]
        g_row = gl_scr[prev, pl.ds(tok, 1), :]
        a_t = jnp.where(key_row == i_row, 1.0, 0.0).astype(BF16)
        b_t = jnp.where(key_row == j_row, g_row, 0.0).astype(BF16)
        m_t = lax.dot_general(a_t, b_t, NT, preferred_element_type=F32)
        w_scr[pl.ds(tok, n_keys, stride=stride), :] = m_t

    ways = 4
    n_loop1 = 2 * n_heads // ways
    tok_per_iter = t // n_loop1

    def stage1(hh, _):
        for p in range(ways):
            hp = ways * hh + p
            c0 = pl.multiple_of(hp * LANES, LANES)
            qb = q_ref[:, pl.ds(c0, LANES)]
            st = lax.dot_general(keys_ref[hp], qb, NT, preferred_element_type=F32)
            v, ix = _topk_rows_sorted(st, K)
            sv_scr[hp] = v
            si_scr[hp] = ix
        t0 = pl.multiple_of(hh * tok_per_iter, tok_per_iter)
        for u in range(tok_per_iter):
            gate_rows(t0 + u)
        return 0

    lax.fori_loop(0, n_loop1, stage1, 0)

    def stage2(hh, _):
        b0 = hh * blk_per_iter
        for u in range(blk_per_iter):
            r0 = pl.multiple_of((b0 + u) * stride, SUBLANES)
            gm_ref[b0 + u] = w_scr[pl.ds(r0, t), :].astype(gm_ref.dtype)
        for u in range(ways2):
            h = ways2 * hh + u
            sv0, sv1 = sv_scr[2 * h], sv_scr[2 * h + 1]
            si0, si1 = si_scr[2 * h], si_scr[2 * h + 1]
            cand, cidx = [], []
            for a_pat, b_pat, n_valid in layout:
                c = _rows_by_pattern(sv0, a_pat, sub) + _rows_by_pattern(sv1, b_pat, sub)
                if n_valid < SUBLANES:
                    c = jnp.where(sub < n_valid, c, -jnp.inf)
                cand.append(c)
                cidx.append(_rows_by_pattern(si0, a_pat, sub) * n_keys + _rows_by_pattern(si1, b_pat, sub))
            fv, _, eidx = _topk_rows(jnp.concatenate(cand, axis=0), K, jnp.concatenate(cidx, axis=0))
            e = jnp.exp(fv - fv[0:1, :])
            g = e / jnp.sum(e, axis=0, keepdims=True)
            r0 = pl.multiple_of(h * K, K)
            it_scr[pl.ds(r0, K), :] = eidx // n_keys
            jt_scr[pl.ds(r0, K), :] = eidx % n_keys
            gt_scr[pl.ds(r0, K), :] = g
        return 0

    lax.fori_loop(0, n_loop, stage2, 0)
    il_scr[cur] = it_scr[...].T
    jl_scr[cur] = jt_scr[...].T
    gl_scr[cur] = gt_scr[...].T


def _route(q, keys, n_heads, n_keys):
    n = q.shape[0]
    t = LANES
    n_tiles = n // t
    slots = n_heads * P_TOPK
    stride = t + SUBLANES
    kern = functools.partial(_route_kernel, n_heads=n_heads, n_keys=n_keys, stride=stride)
    return pl.pallas_call(
        kern,
        grid=(n_tiles + 1,),
        in_specs=[pl.BlockSpec((t, q.shape[1]), lambda s: (jnp.minimum(s, n_tiles - 1), 0)),
                  pl.BlockSpec(keys.shape, lambda s: (0, 0, 0))],
        out_specs=pl.BlockSpec((n_keys, t, n_keys), lambda s: (0, jnp.maximum(s - 1, 0), 0)),
        out_shape=jax.ShapeDtypeStruct((n_keys, n, n_keys), BF16),
        scratch_shapes=[pltpu.VMEM((2 * n_heads, P_TOPK, t), F32),
                        pltpu.VMEM((2 * n_heads, P_TOPK, t), jnp.int32),
                        pltpu.VMEM((slots, t), jnp.int32),
                        pltpu.VMEM((slots, t), jnp.int32),
                        pltpu.VMEM((slots, t), F32),
                        pltpu.VMEM((2, t, slots), jnp.int32),
                        pltpu.VMEM((2, t, slots), jnp.int32),
                        pltpu.VMEM((2, t, slots), F32),
                        pltpu.VMEM((n_keys * stride, n_keys), F32)],
        compiler_params=_params("arbitrary"),
        name="route",
    )(q, keys)


def _experts_kernel(h2_ref, u_ref, v_ref, m_ref, x1_ref, g2_ref, fg_ref, o_ref, acc_ref, *, final):
    j = pl.program_id(2)

    @pl.when(j == 0)
    def _():
        acc_ref[...] = jnp.zeros_like(acc_ref)

    s = lax.dot_general(h2_ref[0], u_ref[...].astype(BF16), NT, preferred_element_type=F32)
    act = 0.5 * s * (1.0 + lax.erf(s * (2.0 ** -0.5)))
    gate = jnp.concatenate([m_ref[ib] for ib in range(m_ref.shape[0])], axis=1)
    w = (gate.astype(F32) * act).astype(BF16)
    acc_ref[...] += jnp.dot(w, v_ref[...].astype(BF16), preferred_element_type=F32)

    @pl.when(j == pl.num_programs(2) - 1)
    def _():
        x2 = x1_ref[0] + g2_ref[0] * acc_ref[...]
        if final:
            x2 = x2 * lax.rsqrt(jnp.mean(x2 * x2, axis=-1, keepdims=True) + EPS) * fg_ref[...]
        o_ref[0] = x2


def _experts(h2, u, v, gmat, x1, g2, final_g, final, tm=1024, te=1024):
    bsz, s, d = x1.shape
    tm = min(tm, s)
    ne = u.shape[0]
    n_keys = gmat.shape[2]
    n_tiles = s // tm
    kern = functools.partial(_experts_kernel, final=final)
    return pl.pallas_call(
        kern,
        grid=(bsz, n_tiles, ne // te),
        in_specs=[pl.BlockSpec((1, tm, d), lambda b, i, j: (b, i, 0)),
                  pl.BlockSpec((te, d), lambda b, i, j: (j, 0)),
                  pl.BlockSpec((te, d), lambda b, i, j: (j, 0)),
                  pl.BlockSpec((te // n_keys, tm, n_keys), lambda b, i, j: (j, b * n_tiles + i, 0)),
                  pl.BlockSpec((1, tm, d), lambda b, i, j: (b, i, 0)),
                  pl.BlockSpec((1, 1, d), lambda b, i, j: (b, 0, 0)),
                  pl.BlockSpec((1, d), lambda b, i, j: (0, 0))],
        out_specs=pl.BlockSpec((1, tm, d), lambda b, i, j: (b, i, 0)),
        out_shape=jax.ShapeDtypeStruct((bsz, s, d), F32),
        scratch_shapes=[pltpu.VMEM((tm, d), F32)],
        compiler_params=_params("arbitrary", "arbitrary", "arbitrary"),
        name="experts",
    )(h2, u, v, gmat, x1, g2, final_g)


def kernel(x, c, ada_w, ada_b, norm1_g, w_in, conv_w, conv_b, mlstm_gate_b, mlstm_norm_g, lambda_q1,
           lambda_k1, lambda_q2, lambda_k2, diff_norm_g, w_out, norm2_g, peer_w_query, peer_sub_keys,
           peer_u, peer_v, final_g):
    bsz, s, d = x.shape
    depth = ada_w.shape[0]
    m_heads = mlstm_gate_b.shape[1] // 2
    m_width = mlstm_norm_g.shape[1]
    m_hd = m_width // m_heads
    dh = lambda_q1.shape[1]
    d_width = w_out.shape[1] - m_width
    d_heads = d_width // (2 * dh)
    p_heads, _, n_keys, _ = peer_sub_keys.shape[1:]
    n_m = 4 * m_width
    n_g = 2 * m_heads
    assert s % M_CHUNK == 0 and m_hd == LANES and 2 * dh == LANES and n_keys == LANES
    assert peer_sub_keys.shape[-1] == LANES and p_heads % 2 == 0 and n_keys == SUBLANES * P_TOPK

    for l in range(depth):
        mod = _adaln(c, ada_w[l], ada_b[l])
        sh1, sc1, g1, sh2, sc2, g2 = (a.reshape(bsz, 1, d) for a in jnp.split(mod, 6, axis=-1))

        w = w_in[l].astype(BF16)
        w_m = w[:, :n_m]
        w_g = jnp.pad(w[:, n_m:n_m + n_g], ((0, 0), (0, LANES - n_g)))
        w_d = w[:, n_m + n_g:]
        pm, pg, pd = _inproj(x, sh1, sc1, norm1_g[l].reshape(1, d), w_m, w_g, w_d)

        gates = pg[:, :, :n_g].transpose(0, 2, 1).reshape(bsz, n_g, s // M_CHUNK, M_CHUNK)
        hm = _mlstm(pm, gates, mlstm_gate_b[l], conv_w[l], conv_b[l].reshape(1, -1),
                    mlstm_norm_g[l].reshape(1, -1), m_heads, m_hd)

        lam_init = 0.8 - 0.6 * math.exp(-0.3 * l)
        lam_p = jnp.stack([lambda_q1[l], lambda_k1[l], lambda_q2[l], lambda_k2[l]])
        od = _diffattn(pd, lam_p, diff_norm_g[l].reshape(1, -1), d_heads, dh, lam_init)

        wo = w_out[l].astype(BF16)
        x1, h2, q = _outproj(hm, od, x, g1, sh2, sc2, norm2_g[l].reshape(1, d), wo[:m_width], wo[m_width:],
                             peer_w_query[l].astype(BF16))

        keys = peer_sub_keys[l].reshape(2 * p_heads, n_keys, -1).astype(BF16)
        gmat = _route(q.reshape(bsz * s, -1), keys, p_heads, n_keys)
        x = _experts(h2, peer_u[l], peer_v[l], gmat, x1, g2,
                     final_g.reshape(1, d), final=(l == depth - 1))
    return x
```

```python
import functools
import math

import jax
import jax.numpy as jnp
from jax import lax
from jax.experimental import pallas as pl
from jax.experimental.pallas import tpu as pltpu

F32 = jnp.float32
BF16 = jnp.bfloat16
EPS = 1e-6
LANES = 128
SUBLANES = 8
VMEM_LIMIT = 56 * 1024 * 1024
M_CHUNK = 128
CONV_WIDTH = 4
P_TOPK = 16

NT = (((1,), (1,)), ((), ()))
TN = (((0,), (0,)), ((), ()))


def _params(*sem):
    return pltpu.CompilerParams(dimension_semantics=sem, vmem_limit_bytes=VMEM_LIMIT)


def _adaln_kernel(c_ref, w_ref, b_ref, o_ref):
    c = c_ref[...]
    sc = (c * jax.nn.sigmoid(c)).astype(BF16)
    o_ref[...] = jnp.dot(sc, w_ref[...].astype(BF16), preferred_element_type=F32) + b_ref[...]


def _adaln(c, w, b):
    bsz, d = c.shape
    n = w.shape[1]
    tn = 1536
    return pl.pallas_call(
        _adaln_kernel,
        grid=(n // tn,),
        in_specs=[pl.BlockSpec((bsz, d), lambda j: (0, 0)),
                  pl.BlockSpec((d, tn), lambda j: (0, j)),
                  pl.BlockSpec((1, tn), lambda j: (0, j))],
        out_specs=pl.BlockSpec((bsz, tn), lambda j: (0, j)),
        out_shape=jax.ShapeDtypeStruct((bsz, n), F32),
        compiler_params=_params("arbitrary"),
        name="adaln",
    )(c, w, b.reshape(1, n))


def _inproj_kernel(x_ref, sh_ref, sc_ref, g_ref, wm_ref, wg_ref, wd_ref, pm_ref, pg_ref, pd_ref):
    x = x_ref[0]
    ms = jnp.mean(x * x, axis=-1, keepdims=True)
    h = x * lax.rsqrt(ms + EPS) * g_ref[...]
    h = h * (1.0 + sc_ref[0]) + sh_ref[0]
    hb = h.astype(BF16)
    pm_ref[0] = jnp.dot(hb, wm_ref[...], preferred_element_type=F32)
    pg_ref[0] = jnp.dot(hb, wg_ref[...], preferred_element_type=F32)
    pd_ref[0] = jnp.dot(hb, wd_ref[...], preferred_element_type=F32).astype(BF16)


def _inproj(x, sh, sc, g, wm, wg, wd, tm=512):
    bsz, s, d = x.shape
    tm = min(tm, s)
    nm, ng, nd = wm.shape[1], wg.shape[1], wd.shape[1]
    tok = lambda b, i: (b, i, 0)
    per_b = lambda b, i: (b, 0, 0)
    const = lambda b, i: (0, 0)
    return pl.pallas_call(
        _inproj_kernel,
        grid=(bsz, s // tm),
        in_specs=[pl.BlockSpec((1, tm, d), tok),
                  pl.BlockSpec((1, 1, d), per_b),
                  pl.BlockSpec((1, 1, d), per_b),
                  pl.BlockSpec((1, d), const),
                  pl.BlockSpec((d, nm), const),
                  pl.BlockSpec((d, ng), const),
                  pl.BlockSpec((d, nd), const)],
        out_specs=[pl.BlockSpec((1, tm, nm), tok),
                   pl.BlockSpec((1, tm, ng), tok),
                   pl.BlockSpec((1, tm, nd), tok)],
        out_shape=[jax.ShapeDtypeStruct((bsz, s, nm), F32),
                   jax.ShapeDtypeStruct((bsz, s, ng), F32),
                   jax.ShapeDtypeStruct((bsz, s, nd), BF16)],
        compiler_params=_params("arbitrary", "arbitrary"),
        name="inproj",
    )(x, sh, sc, g, wm, wg, wd)


def _causal_conv(u, tail, w, b):
    ext = jnp.concatenate([tail, u], axis=0)
    y = u * w[CONV_WIDTH - 1:CONV_WIDTH, :]
    for k in range(CONV_WIDTH - 1):
        shift = CONV_WIDTH - 1 - k
        y = y + pltpu.roll(ext, shift, axis=0)[SUBLANES:, :] * w[k:k + 1, :]
    return y + b


def _cumsum_lanes(x):
    lane = lax.broadcasted_iota(jnp.int32, x.shape, 1)
    s = 1
    while s < x.shape[1]:
        x = x + jnp.where(lane >= s, pltpu.roll(x, s, axis=1), 0.0)
        s *= 2
    return x


def _mlstm_kernel(gb_ref, mq_ref, mk_ref, mv_ref, mo_ref, g_ref, cwq_ref, cwk_ref, cbq_ref, cbk_ref,
                  ng_ref, o_ref, b_scr, i_scr, c_scr, *, n_heads, head_dim, group):
    h0 = pl.program_id(1) * group
    L = M_CHUNK
    hd = head_dim
    nc = b_scr.shape[1]
    silu = lambda a: a * jax.nn.sigmoid(a)

    for u in range(group):
        i_scr[u] = g_ref[0, h0 + u] + gb_ref[h0 + u]
        fpre = g_ref[0, n_heads + h0 + u] + gb_ref[n_heads + h0 + u]
        lf = jnp.minimum(fpre, 0.0) - jnp.log1p(jnp.exp(-jnp.abs(fpre)))
        b_scr[u] = _cumsum_lanes(lf)

    r_i = lax.broadcasted_iota(jnp.int32, (L, L), 0)
    c_i = lax.broadcasted_iota(jnp.int32, (L, L), 1)
    eye = r_i == c_i
    causal = c_i <= r_i

    def to_col(row):
        return jnp.sum(jnp.where(eye, row, 0.0), axis=1, keepdims=True)

    def chunk(c, carry):
        t0 = pl.multiple_of(c * L, L)
        return tuple(head_chunk(c, t0, u, carry[u]) for u in range(group))

    def head_chunk(c, t0, u, carry):
        n, m, q_tail, k_tail = carry
        cols = slice(u * hd, (u + 1) * hd)
        q_raw = mq_ref[0, pl.ds(t0, L), cols]
        k_raw = mk_ref[0, pl.ds(t0, L), cols]
        qc = silu(_causal_conv(q_raw, q_tail, cwq_ref[:, cols], cbq_ref[:, cols])).astype(BF16)
        kc = (silu(_causal_conv(k_raw, k_tail, cwk_ref[:, cols], cbk_ref[:, cols])) * (hd ** -0.5)).astype(BF16)
        vc = mv_ref[0, pl.ds(t0, L), cols].astype(BF16)
        b_row = b_scr[u, pl.ds(c, 1), :]
        i_row = i_scr[u, pl.ds(c, 1), :]
        b_col = to_col(b_row)
        i_col = to_col(i_row)

        logD = jnp.where(causal, b_col - b_row + i_row, -jnp.inf)
        m_t = jnp.maximum(b_col + m, jnp.max(logD, axis=1, keepdims=True))
        Dw = jnp.exp(logD - m_t)
        inter = jnp.exp(b_col + m - m_t)
        sqk = lax.dot_general(qc, kc, NT, preferred_element_type=F32) * Dw
        num = (jnp.dot(sqk.astype(BF16), vc, preferred_element_type=F32)
               + inter * jnp.dot(qc, c_scr[u].astype(BF16), preferred_element_type=F32))
        den = (jnp.sum(sqk, axis=1, keepdims=True)
               + inter * jnp.sum(qc.astype(F32) * n, axis=1, keepdims=True))
        hh = num / jnp.maximum(jnp.abs(den), jnp.exp(-m_t))

        bL = b_row[:, L - 1:L]
        m_new = jnp.maximum(bL + m, jnp.max(bL - b_row + i_row, axis=1, keepdims=True))
        w_col = jnp.exp(bL - b_col + i_col - m_new)
        decay = jnp.exp(bL + m - m_new)
        kw = kc.astype(F32) * w_col
        c_scr[u] = decay * c_scr[u] + lax.dot_general(kw.astype(BF16), vc, TN, preferred_element_type=F32)
        n_new = decay * n + jnp.sum(kw, axis=0, keepdims=True)

        y = hh * lax.rsqrt(jnp.mean(hh * hh, axis=-1, keepdims=True) + EPS) * ng_ref[:, cols]
        y = y * jax.nn.sigmoid(mo_ref[0, pl.ds(t0, L), cols])
        o_ref[0, pl.ds(t0, L), cols] = y.astype(o_ref.dtype)
        return n_new, m_new, q_raw[L - SUBLANES:, :], k_raw[L - SUBLANES:, :]

    zeros = lambda *shape: jnp.zeros(shape, F32)
    c_scr[...] = zeros(*c_scr.shape)
    init = (zeros(1, hd), zeros(1, 1), zeros(SUBLANES, hd), zeros(SUBLANES, hd))
    lax.fori_loop(0, nc, chunk, (init,) * group)


def _mlstm(pm, gates, gate_b, conv_w, conv_b, norm_g, n_heads, head_dim, group=2):
    bsz, s, _ = pm.shape
    nc = s // M_CHUNK
    gw = group * head_dim
    n_groups = n_heads // group
    col = lambda off: (lambda b, g: (b, 0, off * n_groups + g))
    wcol = lambda off: (lambda b, g: (0, off * n_groups + g))
    kern = functools.partial(_mlstm_kernel, n_heads=n_heads, head_dim=head_dim, group=group)
    return pl.pallas_call(
        kern,
        grid=(bsz, n_groups),
        in_specs=[pl.BlockSpec(memory_space=pltpu.SMEM),
                  pl.BlockSpec((1, s, gw), col(0)),
                  pl.BlockSpec((1, s, gw), col(1)),
                  pl.BlockSpec((1, s, gw), col(2)),
                  pl.BlockSpec((1, s, gw), col(3)),
                  pl.BlockSpec((1, 2 * n_heads, nc, M_CHUNK), lambda b, g: (b, 0, 0, 0)),
                  pl.BlockSpec((CONV_WIDTH, gw), wcol(0)),
                  pl.BlockSpec((CONV_WIDTH, gw), wcol(1)),
                  pl.BlockSpec((1, gw), wcol(0)),
                  pl.BlockSpec((1, gw), wcol(1)),
                  pl.BlockSpec((1, gw), wcol(0))],
        out_specs=pl.BlockSpec((1, s, gw), col(0)),
        out_shape=jax.ShapeDtypeStruct((bsz, s, n_heads * head_dim), BF16),
        scratch_shapes=[pltpu.VMEM((group, nc, M_CHUNK), F32), pltpu.VMEM((group, nc, M_CHUNK), F32),
                        pltpu.VMEM((group, head_dim, head_dim), F32)],
        compiler_params=_params("arbitrary", "arbitrary"),
        name="mlstm",
    )(gate_b, pm, pm, pm, pm, gates, conv_w, conv_w, conv_b, conv_b, norm_g)


def _diffattn_kernel(q_ref, k_ref, v_ref, lam_ref, ng_ref, o_ref, m_scr, l_scr, acc_scr,
                     *, dh, lam_init, group):
    i = pl.program_id(2)
    tq = q_ref.shape[1]
    hw = 2 * dh
    scale = dh ** -0.5
    exact_scale = math.frexp(scale)[0] == 0.5
    lane = lax.broadcasted_iota(jnp.int32, (tq, hw), 1)
    qs = []
    for u in range(group):
        q = q_ref[0, :, u * hw:(u + 1) * hw]
        if exact_scale:
            q = q * scale
        zero = jnp.zeros_like(q)
        qs += [jnp.where(lane < dh, q, zero), jnp.where(lane < dh, zero, q)]
    lp = lam_ref[...]
    lam = (jnp.exp(jnp.sum(lp[0:1] * lp[1:2], axis=1, keepdims=True))
           - jnp.exp(jnp.sum(lp[2:3] * lp[3:4], axis=1, keepdims=True)) + lam_init)
    on_or_below_diag = (lax.broadcasted_iota(jnp.int32, (tq, tq), 1)
                        <= lax.broadcasted_iota(jnp.int32, (tq, tq), 0))

    def block(k0, width, diagonal):
        for st in range(2 * group):
            cols = slice((st // 2) * hw, (st // 2 + 1) * hw)
            kb = k_ref[0, pl.ds(k0, width), cols]
            vb = v_ref[0, pl.ds(k0, width), cols]
            s = lax.dot_general(qs[st], kb, NT, preferred_element_type=F32)
            if not exact_scale:
                s = s * scale
            row_max = lambda a: jnp.broadcast_to(jnp.max(a, axis=1, keepdims=True), (tq, LANES))
            if diagonal:
                s = jnp.where(on_or_below_diag, s, -jnp.inf)
                m_new = row_max(s)
            else:
                m_old = m_scr[st]
                m_new = jnp.maximum(m_old, row_max(s))
                alpha = jnp.exp(m_old - m_new)
            e = [jnp.exp(s[:, c:c + LANES] - m_new) for c in range(0, width, LANES)]
            part = functools.reduce(jnp.add, e)
            pv = jnp.dot(jnp.concatenate([x.astype(BF16) for x in e], axis=1), vb, preferred_element_type=F32)
            l_scr[st] = part if diagonal else alpha * l_scr[st] + part
            acc_scr[st] = pv if diagonal else alpha * acc_scr[st] + pv
            m_scr[st] = m_new

    def below(j, carry):
        block(pl.multiple_of(j * tq, tq), tq, False)
        return carry

    block(pl.multiple_of(i * tq, tq), tq, True)
    lax.fori_loop(0, i, below, 0)
    for u in range(group):
        o1 = acc_scr[2 * u] / jnp.sum(l_scr[2 * u], axis=1, keepdims=True)
        o2 = acc_scr[2 * u + 1] / jnp.sum(l_scr[2 * u + 1], axis=1, keepdims=True)
        o = o1 - lam * o2
        y = o * lax.rsqrt(jnp.mean(o * o, axis=-1, keepdims=True) + EPS) * ng_ref[...] * (1.0 - lam_init)
        o_ref[0, :, u * hw:(u + 1) * hw] = y.astype(o_ref.dtype)


def _diffattn(pd, lam_p, norm_g, n_heads, dh, lam_init, tq=512, group=4):
    bsz, s, _ = pd.shape
    tq = min(tq, s)
    hw = 2 * dh
    gw = group * hw
    n_groups = n_heads // group
    kern = functools.partial(_diffattn_kernel, dh=dh, lam_init=lam_init, group=group)
    return pl.pallas_call(
        kern,
        grid=(bsz, n_groups, s // tq),
        in_specs=[pl.BlockSpec((1, tq, gw), lambda b, g, i: (b, i, g)),
                  pl.BlockSpec((1, s, gw), lambda b, g, i: (b, 0, n_groups + g)),
                  pl.BlockSpec((1, s, gw), lambda b, g, i: (b, 0, 2 * n_groups + g)),
                  pl.BlockSpec((4, dh), lambda b, g, i: (0, 0)),
                  pl.BlockSpec((1, hw), lambda b, g, i: (0, 0))],
        out_specs=pl.BlockSpec((1, tq, gw), lambda b, g, i: (b, i, g)),
        out_shape=jax.ShapeDtypeStruct((bsz, s, n_heads * hw), BF16),
        scratch_shapes=[pltpu.VMEM((2 * group, tq, LANES), F32), pltpu.VMEM((2 * group, tq, LANES), F32),
                        pltpu.VMEM((2 * group, tq, hw), F32)],
        compiler_params=_params("arbitrary", "arbitrary", "arbitrary"),
        name="diffattn",
    )(pd, pd, pd, lam_p, norm_g)


def _outproj_kernel(hm_ref, od_ref, x_ref, g1_ref, sh_ref, sc_ref, ng_ref, wm_ref, wd_ref, wq_ref,
                    x1_ref, h2_ref, q_ref):
    y = (jnp.dot(hm_ref[0], wm_ref[...], preferred_element_type=F32)
         + jnp.dot(od_ref[0], wd_ref[...], preferred_element_type=F32))
    x1 = x_ref[0] + g1_ref[0] * y
    x1_ref[0] = x1
    ms = jnp.mean(x1 * x1, axis=-1, keepdims=True)
    h2 = x1 * lax.rsqrt(ms + EPS) * ng_ref[...]
    h2 = (h2 * (1.0 + sc_ref[0]) + sh_ref[0]).astype(BF16)
    h2_ref[0] = h2
    q_ref[0] = jnp.dot(h2, wq_ref[...], preferred_element_type=F32).astype(BF16)


def _outproj(hm, od, x, g1, sh2, sc2, ng, wm, wd, wq, tm=512):
    bsz, s, d = x.shape
    tm = min(tm, s)
    wm_w, wd_w, nq = hm.shape[2], od.shape[2], wq.shape[1]
    tok = lambda b, i: (b, i, 0)
    per_b = lambda b, i: (b, 0, 0)
    const = lambda b, i: (0, 0)
    return pl.pallas_call(
        _outproj_kernel,
        grid=(bsz, s // tm),
        in_specs=[pl.BlockSpec((1, tm, wm_w), tok),
                  pl.BlockSpec((1, tm, wd_w), tok),
                  pl.BlockSpec((1, tm, d), tok),
                  pl.BlockSpec((1, 1, d), per_b),
                  pl.BlockSpec((1, 1, d), per_b),
                  pl.BlockSpec((1, 1, d), per_b),
                  pl.BlockSpec((1, d), const),
                  pl.BlockSpec((wm_w, d), const),
                  pl.BlockSpec((wd_w, d), const),
                  pl.BlockSpec((d, nq), const)],
        out_specs=[pl.BlockSpec((1, tm, d), tok),
                   pl.BlockSpec((1, tm, d), tok),
                   pl.BlockSpec((1, tm, nq), tok)],
        out_shape=[jax.ShapeDtypeStruct((bsz, s, d), F32),
                   jax.ShapeDtypeStruct((bsz, s, d), BF16),
                   jax.ShapeDtypeStruct((bsz, s, nq), BF16)],
        compiler_params=_params("arbitrary", "arbitrary"),
        name="outproj",
    )(hm, od, x, g1, sh2, sc2, ng, wm, wd, wq)


def _topk_rows(s, k, payload=None):
    n_rows, t = s.shape
    nv = n_rows // SUBLANES
    slab = lambda a, v: a[SUBLANES * v:SUBLANES * (v + 1), :]
    slabs = [slab(s, v) for v in range(nv)]
    sub = lax.broadcasted_iota(jnp.int32, (SUBLANES, t), 0)
    out_row = lax.broadcasted_iota(jnp.int32, (k, t), 0)
    vals = jnp.zeros((k, t), s.dtype)
    rows = jnp.zeros((k, t), jnp.int32)
    pays = None if payload is None else jnp.zeros((k, t), payload.dtype)
    for r in range(k):
        nodes = [(slabs[v], v) for v in range(nv)]
        while len(nodes) > 1:
            nxt = []
            for a in range(0, len(nodes) - 1, 2):
                (va, ia), (vb, ib) = nodes[a], nodes[a + 1]
                c = va >= vb
                nxt.append((jnp.where(c, va, vb), jnp.where(c, ia, ib)))
            if len(nodes) % 2:
                nxt.append(nodes[-1])
            nodes = nxt
        m8, v8 = nodes[0]
        m = jnp.max(m8, axis=0, keepdims=True)
        row8 = v8 * SUBLANES + sub
        idx = jnp.min(jnp.where(m8 == m, row8, n_rows), axis=0, keepdims=True)
        won = jnp.where(row8 == idx, v8, -1)
        hits = [won == v for v in range(nv)]
        vals = jnp.where(out_row == r, m, vals)
        rows = jnp.where(out_row == r, idx, rows)
        if payload is not None:
            picked = [jnp.where(hits[v], slab(payload, v), 0) for v in range(nv)]
            p = jnp.sum(functools.reduce(jnp.add, picked), axis=0, keepdims=True)
            pays = jnp.where(out_row == r, p, pays)
        slabs = [jnp.where(hits[v], -jnp.inf, slabs[v]) for v in range(nv)]
    return vals, rows, pays


def _oddeven_merge_sort(n):
    def merge(lo, hi, r):
        step = 2 * r
        if step < hi - lo:
            yield from merge(lo, hi, step)
            yield from merge(lo + r, hi, step)
            yield from ((i, i + r) for i in range(lo + r, hi - r, step))
        else:
            yield (lo, lo + r)

    def sort(lo, hi):
        if hi > lo:
            mid = lo + (hi - lo) // 2
            yield from sort(lo, mid)
            yield from sort(mid + 1, hi)
            yield from merge(lo, hi, 1)

    return list(sort(0, n - 1))


def _topk_rows_sorted(s, k):
    n_rows, t = s.shape
    nv = n_rows // SUBLANES
    assert nv == k and nv & (nv - 1) == 0
    sub = lax.broadcasted_iota(jnp.int32, (SUBLANES, t), 0)
    out_row = lax.broadcasted_iota(jnp.int32, (k, t), 0)
    val = [s[SUBLANES * v:SUBLANES * (v + 1), :] for v in range(nv)]
    sid = list(range(nv))
    for i, j in _oddeven_merge_sort(nv):
        va, vb, ia, ib = val[i], val[j], sid[i], sid[j]
        if isinstance(ia, int) and isinstance(ib, int):
            first = va >= vb if ia < ib else va > vb
        else:
            first = (va > vb) | ((va == vb) & (ia < ib))
        val[i], val[j] = jnp.where(first, va, vb), jnp.where(first, vb, va)
        sid[i], sid[j] = jnp.where(first, ia, ib), jnp.where(first, ib, ia)
    row = [x * SUBLANES + sub for x in sid]
    vals = jnp.zeros((k, t), s.dtype)
    rows = jnp.zeros((k, t), jnp.int32)
    for r in range(k):
        m = jnp.max(val[0], axis=0, keepdims=True)
        idx = jnp.min(jnp.where(val[0] == m, row[0], n_rows), axis=0, keepdims=True)
        hit = row[0] == idx
        vals = jnp.where(out_row == r, m, vals)
        rows = jnp.where(out_row == r, idx, rows)
        for q in range(k - 1 - r):
            val[q] = jnp.where(hit, val[q + 1], val[q])
            row[q] = jnp.where(hit, row[q + 1], row[q])
    return vals, rows


def _candidate_layout(k):
    slabs, cur = [], []

    def flush():
        n_valid = len(cur)
        while len(cur) < SUBLANES:
            r = len(cur)
            period = 1
            while period < max(b for _, b in cur[:n_valid]) + 1:
                period *= 2
            b = cur[r - period][1] if r >= period else r
            cur.append((cur[n_valid - 1][0], b))
        slabs.append(([a for a, _ in cur], [b for _, b in cur], n_valid))
        cur.clear()

    for a in range(k):
        nb = k // (a + 1)
        for b0 in range(0, nb, SUBLANES):
            group = [(a, b) for b in range(b0, min(nb, b0 + SUBLANES))]
            if cur and (len(cur) + len(group) > SUBLANES or (nb == 1 and a % SUBLANES == 0)):
                flush()
            cur.extend(group)
    if cur:
        flush()
    return slabs


def _rows_by_pattern(x, pattern, sub):
    p0 = pattern[0]
    if p0 % SUBLANES == 0 and pattern == list(range(p0, p0 + SUBLANES)):
        return x[p0:p0 + SUBLANES, :]
    period = SUBLANES
    while period > 1 and all(pattern[r] == pattern[r % (period // 2)] for r in range(SUBLANES)):
        period //= 2
    pos = sub if period == SUBLANES else sub & (period - 1)
    out = x[p0:p0 + 1, :]
    for r in range(1, period):
        if pattern[r] != pattern[r - 1]:
            out = jnp.where(pos >= r, x[pattern[r]:pattern[r] + 1, :], out)
    return jnp.broadcast_to(out, (SUBLANES, x.shape[1]))


def _route_kernel(q_ref, keys_ref, gm_ref, sv_scr, si_scr, it_scr, jt_scr, gt_scr, il_scr, jl_scr, gl_scr,
                  w_scr, *, n_heads, n_keys, stride):
    K = P_TOPK
    t = q_ref.shape[0]
    slots = n_heads * K
    step = pl.program_id(0)
    cur = step % 2
    prev = 1 - cur
    sub = lax.broadcasted_iota(jnp.int32, (SUBLANES, t), 0)
    key_row = lax.broadcasted_iota(jnp.int32, (n_keys, slots), 0)
    layout = _candidate_layout(K)
    ways2 = 4
    n_loop = n_heads // ways2
    blk_per_iter = n_keys // n_loop

    @pl.when(step == 0)
    def _():
        il_scr[1] = jnp.zeros((t, slots), jnp.int32)
        jl_scr[1] = jnp.zeros((t, slots), jnp.int32)
        gl_scr[1] = jnp.zeros((t, slots), F32)

    def gate_rows(tok):
        i_row = il_scr[prev, pl.ds(tok, 1), :]
        j_row = jl_scr[prev, pl.ds(tok, 1), :]
        g_row = gl_scr[prev, pl.ds(tok, 1), :]
        a_t = jnp.where(key_row == i_row, 1.0, 0.0).astype(BF16)
        b_t = jnp.where(key_row == j_row, g_row, 0.0).astype(BF16)
        m_t = lax.dot_general(a_t, b_t, NT, preferred_element_type=F32)
        w_scr[pl.ds(tok, n_keys, stride=stride), :] = m_t

    ways = 4
    n_loop1 = 2 * n_heads // ways
    tok_per_iter = t // n_loop1

    def stage1(hh, _):
        for p in range(ways):
            hp = ways * hh + p
            c0 = pl.multiple_of(hp * LANES, LANES)
            qb = q_ref[:, pl.ds(c0, LANES)]
            st = lax.dot_general(keys_ref[hp], qb, NT, preferred_element_type=F32)
            v, ix = _topk_rows_sorted(st, K)
            sv_scr[hp] = v
            si_scr[hp] = ix
        t0 = pl.multiple_of(hh * tok_per_iter, tok_per_iter)
        for u in range(tok_per_iter):
            gate_rows(t0 + u)
        return 0

    lax.fori_loop(0, n_loop1, stage1, 0, unroll=True)

    def stage2(hh, _):
        b0 = hh * blk_per_iter
        for u in range(blk_per_iter):
            r0 = pl.multiple_of((b0 + u) * stride, SUBLANES)
            gm_ref[b0 + u] = w_scr[pl.ds(r0, t), :].astype(gm_ref.dtype)
        for u in range(ways2):
            h = ways2 * hh + u
            sv0, sv1 = sv_scr[2 * h], sv_scr[2 * h + 1]
            si0, si1 = si_scr[2 * h], si_scr[2 * h + 1]
            cand, cidx = [], []
            for a_pat, b_pat, n_valid in layout:
                c = _rows_by_pattern(sv0, a_pat, sub) + _rows_by_pattern(sv1, b_pat, sub)
                if n_valid < SUBLANES:
                    c = jnp.where(sub < n_valid, c, -jnp.inf)
                cand.append(c)
                cidx.append(_rows_by_pattern(si0, a_pat, sub) * n_keys + _rows_by_pattern(si1, b_pat, sub))
            fv, _, eidx = _topk_rows(jnp.concatenate(cand, axis=0), K, jnp.concatenate(cidx, axis=0))
            e = jnp.exp(fv - fv[0:1, :])
            g = e / jnp.sum(e, axis=0, keepdims=True)
            r0 = pl.multiple_of(h * K, K)
            it_scr[pl.ds(r0, K), :] = eidx // n_keys
            jt_scr[pl.ds(r0, K), :] = eidx % n_keys
            gt_scr[pl.ds(r0, K), :] = g
        return 0

    lax.fori_loop(0, n_loop, stage2, 0, unroll=True)
    il_scr[cur] = it_scr[...].T
    jl_scr[cur] = jt_scr[...].T
    gl_scr[cur] = gt_scr[...].T


def _route(q, keys, n_heads, n_keys):
    n = q.shape[0]
    t = LANES
    n_tiles = n // t
    slots = n_heads * P_TOPK
    stride = t + SUBLANES
    kern = functools.partial(_route_kernel, n_heads=n_heads, n_keys=n_keys, stride=stride)
    return pl.pallas_call(
        kern,
        grid=(n_tiles + 1,),
        in_specs=[pl.BlockSpec((t, q.shape[1]), lambda s: (jnp.minimum(s, n_tiles - 1), 0)),
                  pl.BlockSpec(keys.shape, lambda s: (0, 0, 0))],
        out_specs=pl.BlockSpec((n_keys, t, n_keys), lambda s: (0, jnp.maximum(s - 1, 0), 0)),
        out_shape=jax.ShapeDtypeStruct((n_keys, n, n_keys), BF16),
        scratch_shapes=[pltpu.VMEM((2 * n_heads, P_TOPK, t), F32),
                        pltpu.VMEM((2 * n_heads, P_TOPK, t), jnp.int32),
                        pltpu.VMEM((slots, t), jnp.int32),
                        pltpu.VMEM((slots, t), jnp.int32),
                        pltpu.VMEM((slots, t), F32),
                        pltpu.VMEM((2, t, slots), jnp.int32),
                        pltpu.VMEM((2, t, slots), jnp.int32),
                        pltpu.VMEM((2, t, slots), F32),
                        pltpu.VMEM((n_keys * stride, n_keys), F32)],
        compiler_params=_params("arbitrary"),
        name="route",
    )(q, keys)


def _experts_kernel(h2_ref, u_ref, v_ref, m_ref, x1_ref, g2_ref, fg_ref, o_ref, acc_ref, *, final):
    j = pl.program_id(2)

    @pl.when(j == 0)
    def _():
        acc_ref[...] = jnp.zeros_like(acc_ref)

    s = lax.dot_general(h2_ref[0], u_ref[...].astype(BF16), NT, preferred_element_type=F32)
    act = 0.5 * s * (1.0 + lax.erf(s * (2.0 ** -0.5)))
    gate = jnp.concatenate([m_ref[ib] for ib in range(m_ref.shape[0])], axis=1)
    w = (gate.astype(F32) * act).astype(BF16)
    acc_ref[...] += jnp.dot(w, v_ref[...].astype(BF16), preferred_element_type=F32)

    @pl.when(j == pl.num_programs(2) - 1)
    def _():
        x2 = x1_ref[0] + g2_ref[0] * acc_ref[...]
        if final:
            x2 = x2 * lax.rsqrt(jnp.mean(x2 * x2, axis=-1, keepdims=True) + EPS) * fg_ref[...]
        o_ref[0] = x2


def _experts(h2, u, v, gmat, x1, g2, final_g, final, tm=1024, te=1024):
    bsz, s, d = x1.shape
    tm = min(tm, s)
    ne = u.shape[0]
    n_keys = gmat.shape[2]
    n_tiles = s // tm
    kern = functools.partial(_experts_kernel, final=final)
    return pl.pallas_call(
        kern,
        grid=(bsz, n_tiles, ne // te),
        in_specs=[pl.BlockSpec((1, tm, d), lambda b, i, j: (b, i, 0)),
                  pl.BlockSpec((te, d), lambda b, i, j: (j, 0)),
                  pl.BlockSpec((te, d), lambda b, i, j: (j, 0)),
                  pl.BlockSpec((te // n_keys, tm, n_keys), lambda b, i, j: (j, b * n_tiles + i, 0)),
                  pl.BlockSpec((1, tm, d), lambda b, i, j: (b, i, 0)),
                  pl.BlockSpec((1, 1, d), lambda b, i, j: (b, 0, 0)),
                  pl.BlockSpec((1, d), lambda b, i, j: (0, 0))],
        out_specs=pl.BlockSpec((1, tm, d), lambda b, i, j: (b, i, 0)),
        out_shape=jax.ShapeDtypeStruct((bsz, s, d), F32),
        scratch_shapes=[pltpu.VMEM((tm, d), F32)],
        compiler_params=_params("arbitrary", "arbitrary", "arbitrary"),
        name="experts",
    )(h2, u, v, gmat, x1, g2, final_g)


def kernel(x, c, ada_w, ada_b, norm1_g, w_in, conv_w, conv_b, mlstm_gate_b, mlstm_norm_g, lambda_q1,
           lambda_k1, lambda_q2, lambda_k2, diff_norm_g, w_out, norm2_g, peer_w_query, peer_sub_keys,
           peer_u, peer_v, final_g):
    bsz, s, d = x.shape
    depth = ada_w.shape[0]
    m_heads = mlstm_gate_b.shape[1] // 2
    m_width = mlstm_norm_g.shape[1]
    m_hd = m_width // m_heads
    dh = lambda_q1.shape[1]
    d_width = w_out.shape[1] - m_width
    d_heads = d_width // (2 * dh)
    p_heads, _, n_keys, _ = peer_sub_keys.shape[1:]
    n_m = 4 * m_width
    n_g = 2 * m_heads
    assert s % M_CHUNK == 0 and m_hd == LANES and 2 * dh == LANES and n_keys == LANES
    assert peer_sub_keys.shape[-1] == LANES and p_heads % 2 == 0 and n_keys == SUBLANES * P_TOPK

    for l in range(depth):
        mod = _adaln(c, ada_w[l], ada_b[l])
        sh1, sc1, g1, sh2, sc2, g2 = (a.reshape(bsz, 1, d) for a in jnp.split(mod, 6, axis=-1))

        w = w_in[l].astype(BF16)
        w_m = w[:, :n_m]
        w_g = jnp.pad(w[:, n_m:n_m + n_g], ((0, 0), (0, LANES - n_g)))
        w_d = w[:, n_m + n_g:]
        pm, pg, pd = _inproj(x, sh1, sc1, norm1_g[l].reshape(1, d), w_m, w_g, w_d)

        gates = pg[:, :, :n_g].transpose(0, 2, 1).reshape(bsz, n_g, s // M_CHUNK, M_CHUNK)
        hm = _mlstm(pm, gates, mlstm_gate_b[l], conv_w[l], conv_b[l].reshape(1, -1),
                    mlstm_norm_g[l].reshape(1, -1), m_heads, m_hd)

        lam_init = 0.8 - 0.6 * math.exp(-0.3 * l)
        lam_p = jnp.stack([lambda_q1[l], lambda_k1[l], lambda_q2[l], lambda_k2[l]])
        od = _diffattn(pd, lam_p, diff_norm_g[l].reshape(1, -1), d_heads, dh, lam_init)

        wo = w_out[l].astype(BF16)
        x1, h2, q = _outproj(hm, od, x, g1, sh2, sc2, norm2_g[l].reshape(1, d), wo[:m_width], wo[m_width:],
                             peer_w_query[l].astype(BF16))

        keys = peer_sub_keys[l].reshape(2 * p_heads, n_keys, -1).astype(BF16)
        gmat = _route(q.reshape(bsz * s, -1), keys, p_heads, n_keys)
        x = _experts(h2, peer_u[l], peer_v[l], gmat, x1, g2,
                     final_g.reshape(1, d), final=(l == depth - 1))
    return x
```

```python
import functools
import math

import jax
import jax.numpy as jnp
from jax import lax
from jax.experimental import pallas as pl
from jax.experimental.pallas import tpu as pltpu

F32 = jnp.float32
BF16 = jnp.bfloat16
EPS = 1e-6
LANES = 128
SUBLANES = 8
VMEM_LIMIT = 56 * 1024 * 1024
M_CHUNK = 128
CONV_WIDTH = 4
P_TOPK = 16

NT = (((1,), (1,)), ((), ()))
TN = (((0,), (0,)), ((), ()))


def _params(*sem):
    return pltpu.CompilerParams(dimension_semantics=sem, vmem_limit_bytes=VMEM_LIMIT)


def _adaln_kernel(c_ref, w_ref, b_ref, o_ref):
    c = c_ref[...]
    sc = (c * jax.nn.sigmoid(c)).astype(BF16)
    o_ref[...] = jnp.dot(sc, w_ref[...].astype(BF16), preferred_element_type=F32) + b_ref[...]


def _adaln(c, w, b):
    bsz, d = c.shape
    n = w.shape[1]
    tn = 1536
    return pl.pallas_call(
        _adaln_kernel,
        grid=(n // tn,),
        in_specs=[pl.BlockSpec((bsz, d), lambda j: (0, 0)),
                  pl.BlockSpec((d, tn), lambda j: (0, j)),
                  pl.BlockSpec((1, tn), lambda j: (0, j))],
        out_specs=pl.BlockSpec((bsz, tn), lambda j: (0, j)),
        out_shape=jax.ShapeDtypeStruct((bsz, n), F32),
        compiler_params=_params("arbitrary"),
        name="adaln",
    )(c, w, b.reshape(1, n))


def _inproj_kernel(x_ref, sh_ref, sc_ref, g_ref, wm_ref, wg_ref, wd_ref, pm_ref, pg_ref, pd_ref):
    x = x_ref[0]
    ms = jnp.mean(x * x, axis=-1, keepdims=True)
    h = x * lax.rsqrt(ms + EPS) * g_ref[...]
    h = h * (1.0 + sc_ref[0]) + sh_ref[0]
    hb = h.astype(BF16)
    pm_ref[0] = jnp.dot(hb, wm_ref[...], preferred_element_type=F32)
    pg_ref[0] = jnp.dot(hb, wg_ref[...], preferred_element_type=F32)
    pd_ref[0] = jnp.dot(hb, wd_ref[...], preferred_element_type=F32).astype(BF16)


def _inproj(x, sh, sc, g, wm, wg, wd, tm=512):
    bsz, s, d = x.shape
    tm = min(tm, s)
    nm, ng, nd = wm.shape[1], wg.shape[1], wd.shape[1]
    tok = lambda b, i: (b, i, 0)
    per_b = lambda b, i: (b, 0, 0)
    const = lambda b, i: (0, 0)
    return pl.pallas_call(
        _inproj_kernel,
        grid=(bsz, s // tm),
        in_specs=[pl.BlockSpec((1, tm, d), tok),
                  pl.BlockSpec((1, 1, d), per_b),
                  pl.BlockSpec((1, 1, d), per_b),
                  pl.BlockSpec((1, d), const),
                  pl.BlockSpec((d, nm), const),
                  pl.BlockSpec((d, ng), const),
                  pl.BlockSpec((d, nd), const)],
        out_specs=[pl.BlockSpec((1, tm, nm), tok),
                   pl.BlockSpec((1, tm, ng), tok),
                   pl.BlockSpec((1, tm, nd), tok)],
        out_shape=[jax.ShapeDtypeStruct((bsz, s, nm), F32),
                   jax.ShapeDtypeStruct((bsz, s, ng), F32),
                   jax.ShapeDtypeStruct((bsz, s, nd), BF16)],
        compiler_params=_params("arbitrary", "arbitrary"),
        name="inproj",
    )(x, sh, sc, g, wm, wg, wd)


def _causal_conv(u, tail, w, b):
    ext = jnp.concatenate([tail, u], axis=0)
    y = u * w[CONV_WIDTH - 1:CONV_WIDTH, :]
    for k in range(CONV_WIDTH - 1):
        shift = CONV_WIDTH - 1 - k
        y = y + pltpu.roll(ext, shift, axis=0)[SUBLANES:, :] * w[k:k + 1, :]
    return y + b


def _cumsum_lanes(x):
    lane = lax.broadcasted_iota(jnp.int32, x.shape, 1)
    s = 1
    while s < x.shape[1]:
        x = x + jnp.where(lane >= s, pltpu.roll(x, s, axis=1), 0.0)
        s *= 2
    return x


def _mlstm_kernel(gb_ref, mq_ref, mk_ref, mv_ref, mo_ref, g_ref, cwq_ref, cwk_ref, cbq_ref, cbk_ref,
                  ng_ref, o_ref, b_scr, i_scr, c_scr, *, n_heads, head_dim, group):
    h0 = pl.program_id(1) * group
    L = M_CHUNK
    hd = head_dim
    nc = b_scr.shape[1]
    silu = lambda a: a * jax.nn.sigmoid(a)

    for u in range(group):
        i_scr[u] = g_ref[0, h0 + u] + gb_ref[h0 + u]
        fpre = g_ref[0, n_heads + h0 + u] + gb_ref[n_heads + h0 + u]
        lf = jnp.minimum(fpre, 0.0) - jnp.log1p(jnp.exp(-jnp.abs(fpre)))
        b_scr[u] = _cumsum_lanes(lf)

    r_i = lax.broadcasted_iota(jnp.int32, (L, L), 0)
    c_i = lax.broadcasted_iota(jnp.int32, (L, L), 1)
    eye = r_i == c_i
    causal = c_i <= r_i

    def to_col(row):
        return jnp.sum(jnp.where(eye, row, 0.0), axis=1, keepdims=True)

    def chunk(c, carry):
        t0 = pl.multiple_of(c * L, L)
        return tuple(head_chunk(c, t0, u, carry[u]) for u in range(group))

    def head_chunk(c, t0, u, carry):
        n, m, q_tail, k_tail = carry
        cols = slice(u * hd, (u + 1) * hd)
        q_raw = mq_ref[0, pl.ds(t0, L), cols]
        k_raw = mk_ref[0, pl.ds(t0, L), cols]
        qc = silu(_causal_conv(q_raw, q_tail, cwq_ref[:, cols], cbq_ref[:, cols])).astype(BF16)
        kc = (silu(_causal_conv(k_raw, k_tail, cwk_ref[:, cols], cbk_ref[:, cols])) * (hd ** -0.5)).astype(BF16)
        vc = mv_ref[0, pl.ds(t0, L), cols].astype(BF16)
        b_row = b_scr[u, pl.ds(c, 1), :]
        i_row = i_scr[u, pl.ds(c, 1), :]
        b_col = to_col(b_row)
        i_col = to_col(i_row)

        logD = jnp.where(causal, b_col - b_row + i_row, -jnp.inf)
        m_t = jnp.maximum(b_col + m, jnp.max(logD, axis=1, keepdims=True))
        Dw = jnp.exp(logD - m_t)
        inter = jnp.exp(b_col + m - m_t)
        sqk = lax.dot_general(qc, kc, NT, preferred_element_type=F32) * Dw
        num = (jnp.dot(sqk.astype(BF16), vc, preferred_element_type=F32)
               + inter * jnp.dot(qc, c_scr[u].astype(BF16), preferred_element_type=F32))
        den = (jnp.sum(sqk, axis=1, keepdims=True)
               + inter * jnp.sum(qc.astype(F32) * n, axis=1, keepdims=True))
        hh = num / jnp.maximum(jnp.abs(den), jnp.exp(-m_t))

        bL = b_row[:, L - 1:L]
        m_new = jnp.maximum(bL + m, jnp.max(bL - b_row + i_row, axis=1, keepdims=True))
        w_col = jnp.exp(bL - b_col + i_col - m_new)
        decay = jnp.exp(bL + m - m_new)
        kw = kc.astype(F32) * w_col
        c_scr[u] = decay * c_scr[u] + lax.dot_general(kw.astype(BF16), vc, TN, preferred_element_type=F32)
        n_new = decay * n + jnp.sum(kw, axis=0, keepdims=True)

        y = hh * lax.rsqrt(jnp.mean(hh * hh, axis=-1, keepdims=True) + EPS) * ng_ref[:, cols]
        y = y * jax.nn.sigmoid(mo_ref[0, pl.ds(t0, L), cols])
        o_ref[0, pl.ds(t0, L), cols] = y.astype(o_ref.dtype)
        return n_new, m_new, q_raw[L - SUBLANES:, :], k_raw[L - SUBLANES:, :]

    zeros = lambda *shape: jnp.zeros(shape, F32)
    c_scr[...] = zeros(*c_scr.shape)
    init = (zeros(1, hd), zeros(1, 1), zeros(SUBLANES, hd), zeros(SUBLANES, hd))
    lax.fori_loop(0, nc, chunk, (init,) * group)


def _mlstm(pm, gates, gate_b, conv_w, conv_b, norm_g, n_heads, head_dim, group=2):
    bsz, s, _ = pm.shape
    nc = s // M_CHUNK
    gw = group * head_dim
    n_groups = n_heads // group
    col = lambda off: (lambda b, g: (b, 0, off * n_groups + g))
    wcol = lambda off: (lambda b, g: (0, off * n_groups + g))
    kern = functools.partial(_mlstm_kernel, n_heads=n_heads, head_dim=head_dim, group=group)
    return pl.pallas_call(
        kern,
        grid=(bsz, n_groups),
        in_specs=[pl.BlockSpec(memory_space=pltpu.SMEM),
                  pl.BlockSpec((1, s, gw), col(0)),
                  pl.BlockSpec((1, s, gw), col(1)),
                  pl.BlockSpec((1, s, gw), col(2)),
                  pl.BlockSpec((1, s, gw), col(3)),
                  pl.BlockSpec((1, 2 * n_heads, nc, M_CHUNK), lambda b, g: (b, 0, 0, 0)),
                  pl.BlockSpec((CONV_WIDTH, gw), wcol(0)),
                  pl.BlockSpec((CONV_WIDTH, gw), wcol(1)),
                  pl.BlockSpec((1, gw), wcol(0)),
                  pl.BlockSpec((1, gw), wcol(1)),
                  pl.BlockSpec((1, gw), wcol(0))],
        out_specs=pl.BlockSpec((1, s, gw), col(0)),
        out_shape=jax.ShapeDtypeStruct((bsz, s, n_heads * head_dim), BF16),
        scratch_shapes=[pltpu.VMEM((group, nc, M_CHUNK), F32), pltpu.VMEM((group, nc, M_CHUNK), F32),
                        pltpu.VMEM((group, head_dim, head_dim), F32)],
        compiler_params=_params("arbitrary", "arbitrary"),
        name="mlstm",
    )(gate_b, pm, pm, pm, pm, gates, conv_w, conv_w, conv_b, conv_b, norm_g)


def _diffattn_kernel(q_ref, k_ref, v_ref, lam_ref, ng_ref, o_ref, m_scr, l_scr, acc_scr,
                     *, dh, lam_init, group):
    i = pl.program_id(2)
    tq = q_ref.shape[1]
    hw = 2 * dh
    scale = dh ** -0.5
    exact_scale = math.frexp(scale)[0] == 0.5
    lane = lax.broadcasted_iota(jnp.int32, (tq, hw), 1)
    qs = []
    for u in range(group):
        q = q_ref[0, :, u * hw:(u + 1) * hw]
        if exact_scale:
            q = q * scale
        zero = jnp.zeros_like(q)
        qs += [jnp.where(lane < dh, q, zero), jnp.where(lane < dh, zero, q)]
    lp = lam_ref[...]
    lam = (jnp.exp(jnp.sum(lp[0:1] * lp[1:2], axis=1, keepdims=True))
           - jnp.exp(jnp.sum(lp[2:3] * lp[3:4], axis=1, keepdims=True)) + lam_init)
    on_or_below_diag = (lax.broadcasted_iota(jnp.int32, (tq, tq), 1)
                        <= lax.broadcasted_iota(jnp.int32, (tq, tq), 0))

    def block(k0, width, diagonal):
        for st in range(2 * group):
            cols = slice((st // 2) * hw, (st // 2 + 1) * hw)
            kb = k_ref[0, pl.ds(k0, width), cols]
            vb = v_ref[0, pl.ds(k0, width), cols]
            s = lax.dot_general(qs[st], kb, NT, preferred_element_type=F32)
            if not exact_scale:
                s = s * scale
            row_max = lambda a: jnp.broadcast_to(jnp.max(a, axis=1, keepdims=True), (tq, LANES))
            if diagonal:
                s = jnp.where(on_or_below_diag, s, -jnp.inf)
                m_new = row_max(s)
            else:
                m_old = m_scr[st]
                m_new = jnp.maximum(m_old, row_max(s))
                alpha = jnp.exp(m_old - m_new)
            e = [jnp.exp(s[:, c:c + LANES] - m_new) for c in range(0, width, LANES)]
            part = functools.reduce(jnp.add, e)
            pv = jnp.dot(jnp.concatenate([x.astype(BF16) for x in e], axis=1), vb, preferred_element_type=F32)
            l_scr[st] = part if diagonal else alpha * l_scr[st] + part
            acc_scr[st] = pv if diagonal else alpha * acc_scr[st] + pv
            m_scr[st] = m_new

    def below(j, carry):
        block(pl.multiple_of(j * tq, tq), tq, False)
        return carry

    block(pl.multiple_of(i * tq, tq), tq, True)
    lax.fori_loop(0, i, below, 0)
    for u in range(group):
        o1 = acc_scr[2 * u] / jnp.sum(l_scr[2 * u], axis=1, keepdims=True)
        o2 = acc_scr[2 * u + 1] / jnp.sum(l_scr[2 * u + 1], axis=1, keepdims=True)
        o = o1 - lam * o2
        y = o * lax.rsqrt(jnp.mean(o * o, axis=-1, keepdims=True) + EPS) * ng_ref[...] * (1.0 - lam_init)
        o_ref[0, :, u * hw:(u + 1) * hw] = y.astype(o_ref.dtype)


def _diffattn(pd, lam_p, norm_g, n_heads, dh, lam_init, tq=512, group=4):
    bsz, s, _ = pd.shape
    tq = min(tq, s)
    hw = 2 * dh
    gw = group * hw
    n_groups = n_heads // group
    kern = functools.partial(_diffattn_kernel, dh=dh, lam_init=lam_init, group=group)
    return pl.pallas_call(
        kern,
        grid=(bsz, n_groups, s // tq),
        in_specs=[pl.BlockSpec((1, tq, gw), lambda b, g, i: (b, i, g)),
                  pl.BlockSpec((1, s, gw), lambda b, g, i: (b, 0, n_groups + g)),
                  pl.BlockSpec((1, s, gw), lambda b, g, i: (b, 0, 2 * n_groups + g)),
                  pl.BlockSpec((4, dh), lambda b, g, i: (0, 0)),
                  pl.BlockSpec((1, hw), lambda b, g, i: (0, 0))],
        out_specs=pl.BlockSpec((1, tq, gw), lambda b, g, i: (b, i, g)),
        out_shape=jax.ShapeDtypeStruct((bsz, s, n_heads * hw), BF16),
        scratch_shapes=[pltpu.VMEM((2 * group, tq, LANES), F32), pltpu.VMEM((2 * group, tq, LANES), F32),
                        pltpu.VMEM((2 * group, tq, hw), F32)],
        compiler_params=_params("arbitrary", "arbitrary", "arbitrary"),
        name="diffattn",
    )(pd, pd, pd, lam_p, norm_g)


def _outproj_kernel(hm_ref, od_ref, x_ref, g1_ref, sh_ref, sc_ref, ng_ref, wm_ref, wd_ref, wq_ref,
                    x1_ref, h2_ref, q_ref):
    y = (jnp.dot(hm_ref[0], wm_ref[...], preferred_element_type=F32)
         + jnp.dot(od_ref[0], wd_ref[...], preferred_element_type=F32))
    x1 = x_ref[0] + g1_ref[0] * y
    x1_ref[0] = x1
    ms = jnp.mean(x1 * x1, axis=-1, keepdims=True)
    h2 = x1 * lax.rsqrt(ms + EPS) * ng_ref[...]
    h2 = (h2 * (1.0 + sc_ref[0]) + sh_ref[0]).astype(BF16)
    h2_ref[0] = h2
    q_ref[0] = jnp.dot(h2, wq_ref[...], preferred_element_type=F32).astype(BF16)


def _outproj(hm, od, x, g1, sh2, sc2, ng, wm, wd, wq, tm=512):
    bsz, s, d = x.shape
    tm = min(tm, s)
    wm_w, wd_w, nq = hm.shape[2], od.shape[2], wq.shape[1]
    tok = lambda b, i: (b, i, 0)
    per_b = lambda b, i: (b, 0, 0)
    const = lambda b, i: (0, 0)
    return pl.pallas_call(
        _outproj_kernel,
        grid=(bsz, s // tm),
        in_specs=[pl.BlockSpec((1, tm, wm_w), tok),
                  pl.BlockSpec((1, tm, wd_w), tok),
                  pl.BlockSpec((1, tm, d), tok),
                  pl.BlockSpec((1, 1, d), per_b),
                  pl.BlockSpec((1, 1, d), per_b),
                  pl.BlockSpec((1, 1, d), per_b),
                  pl.BlockSpec((1, d), const),
                  pl.BlockSpec((wm_w, d), const),
                  pl.BlockSpec((wd_w, d), const),
                  pl.BlockSpec((d, nq), const)],
        out_specs=[pl.BlockSpec((1, tm, d), tok),
                   pl.BlockSpec((1, tm, d), tok),
                   pl.BlockSpec((1, tm, nq), tok)],
        out_shape=[jax.ShapeDtypeStruct((bsz, s, d), F32),
                   jax.ShapeDtypeStruct((bsz, s, d), BF16),
                   jax.ShapeDtypeStruct((bsz, s, nq), BF16)],
        compiler_params=_params("arbitrary", "arbitrary"),
        name="outproj",
    )(hm, od, x, g1, sh2, sc2, ng, wm, wd, wq)


def _oddeven_merge_sort(n):
    def merge(lo, hi, r):
        step = 2 * r
        if step < hi - lo:
            yield from merge(lo, hi, step)
            yield from merge(lo + r, hi, step)
            yield from ((i, i + r) for i in range(lo + r, hi - r, step))
        else:
            yield (lo, lo + r)

    def sort(lo, hi):
        if hi > lo:
            mid = lo + (hi - lo) // 2
            yield from sort(lo, mid)
            yield from sort(mid + 1, hi)
            yield from merge(lo, hi, 1)

    return list(sort(0, n - 1))


def _topk_rows_sorted(s, k, payload=None):
    n_rows, t = s.shape
    nv = n_rows // SUBLANES
    assert nv & (nv - 1) == 0
    slabs = lambda a: [a[SUBLANES * v:SUBLANES * (v + 1), :] for v in range(nv)]
    sub = lax.broadcasted_iota(jnp.int32, (SUBLANES, t), 0)
    out_row = lax.broadcasted_iota(jnp.int32, (k, t), 0)
    val = slabs(s)
    pay = None if payload is None else slabs(payload)
    sid = list(range(nv))
    for i, j in _oddeven_merge_sort(nv):
        va, vb, ia, ib = val[i], val[j], sid[i], sid[j]
        if isinstance(ia, int) and isinstance(ib, int):
            first = va >= vb if ia < ib else va > vb
        else:
            first = (va > vb) | ((va == vb) & (ia < ib))
        val[i], val[j] = jnp.where(first, va, vb), jnp.where(first, vb, va)
        sid[i], sid[j] = jnp.where(first, ia, ib), jnp.where(first, ib, ia)
        if pay is not None:
            pay[i], pay[j] = jnp.where(first, pay[i], pay[j]), jnp.where(first, pay[j], pay[i])
    row = [x * SUBLANES + sub for x in sid]
    vals = jnp.zeros((k, t), s.dtype)
    rows = jnp.zeros((k, t), jnp.int32)
    pays = None if pay is None else jnp.zeros((k, t), payload.dtype)
    for r in range(k):
        m = jnp.max(val[0], axis=0, keepdims=True)
        idx = jnp.min(jnp.where(val[0] == m, row[0], n_rows), axis=0, keepdims=True)
        hit = row[0] == idx
        vals = jnp.where(out_row == r, m, vals)
        rows = jnp.where(out_row == r, idx, rows)
        if pay is not None:
            pays = jnp.where(out_row == r, jnp.sum(jnp.where(hit, pay[0], 0), axis=0, keepdims=True), pays)
        live = min(nv, k - 1 - r)
        for q in range(min(live, nv - 1)):
            val[q] = jnp.where(hit, val[q + 1], val[q])
            row[q] = jnp.where(hit, row[q + 1], row[q])
            if pay is not None:
                pay[q] = jnp.where(hit, pay[q + 1], pay[q])
        if live == nv:
            val[nv - 1] = jnp.where(hit, -jnp.inf, val[nv - 1])
    return vals, rows, pays


def _candidate_layout(k):
    slabs, cur = [], []

    def flush():
        n_valid = len(cur)
        while len(cur) < SUBLANES:
            r = len(cur)
            period = 1
            while period < max(b for _, b in cur[:n_valid]) + 1:
                period *= 2
            b = cur[r - period][1] if r >= period else r
            cur.append((cur[n_valid - 1][0], b))
        slabs.append(([a for a, _ in cur], [b for _, b in cur], n_valid))
        cur.clear()

    for a in range(k):
        nb = k // (a + 1)
        for b0 in range(0, nb, SUBLANES):
            group = [(a, b) for b in range(b0, min(nb, b0 + SUBLANES))]
            if cur and (len(cur) + len(group) > SUBLANES or (nb == 1 and a % SUBLANES == 0)):
                flush()
            cur.extend(group)
    if cur:
        flush()
    return slabs


def _rows_by_pattern(x, pattern, sub):
    p0 = pattern[0]
    if p0 % SUBLANES == 0 and pattern == list(range(p0, p0 + SUBLANES)):
        return x[p0:p0 + SUBLANES, :]
    period = SUBLANES
    while period > 1 and all(pattern[r] == pattern[r % (period // 2)] for r in range(SUBLANES)):
        period //= 2
    pos = sub if period == SUBLANES else sub & (period - 1)
    out = x[p0:p0 + 1, :]
    for r in range(1, period):
        if pattern[r] != pattern[r - 1]:
            out = jnp.where(pos >= r, x[pattern[r]:pattern[r] + 1, :], out)
    return jnp.broadcast_to(out, (SUBLANES, x.shape[1]))


def _route_kernel(q_ref, keys_ref, gm_ref, sv_scr, si_scr, it_scr, jt_scr, gt_scr, il_scr, jl_scr, gl_scr,
                  w_scr, *, n_heads, n_keys, stride):
    K = P_TOPK
    t = q_ref.shape[0]
    slots = n_heads * K
    step = pl.program_id(0)
    cur = step % 2
    prev = 1 - cur
    sub = lax.broadcasted_iota(jnp.int32, (SUBLANES, t), 0)
    key_row = lax.broadcasted_iota(jnp.int32, (n_keys, slots), 0)
    layout = _candidate_layout(K)
    ways2 = 4
    n_loop = n_heads // ways2
    blk_per_iter = n_keys // n_loop

    @pl.when(step == 0)
    def _():
        il_scr[1] = jnp.zeros((t, slots), jnp.int32)
        jl_scr[1] = jnp.zeros((t, slots), jnp.int32)
        gl_scr[1] = jnp.zeros((t, slots), F32)

    def gate_rows(tok):
        i_row = il_scr[prev, pl.ds(tok, 1), :]
        j_row = jl_scr[prev, pl.ds(tok, 1), :]
        g_row = gl_scr[prev, pl.ds(tok, 1), :]
        a_t = jnp.where(key_row == i_row, 1.0, 0.0).astype(BF16)
        b_t = jnp.where(key_row == j_row, g_row, 0.0).astype(BF16)
        m_t = lax.dot_general(a_t, b_t, NT, preferred_element_type=F32)
        w_scr[pl.ds(tok, n_keys, stride=stride), :] = m_t

    ways = 4
    n_loop1 = 2 * n_heads // ways
    tok_per_iter = t // n_loop1

    def stage1(hh, _):
        for p in range(ways):
            hp = ways * hh + p
            c0 = pl.multiple_of(hp * LANES, LANES)
            qb = q_ref[:, pl.ds(c0, LANES)]
            st = lax.dot_general(keys_ref[hp], qb, NT, preferred_element_type=F32)
            v, ix, _ = _topk_rows_sorted(st, K)
            sv_scr[hp] = v
            si_scr[hp] = ix
        t0 = pl.multiple_of(hh * tok_per_iter, tok_per_iter)
        for u in range(tok_per_iter):
            gate_rows(t0 + u)
        return 0

    lax.fori_loop(0, n_loop1, stage1, 0, unroll=True)

    def stage2(hh, _):
        b0 = hh * blk_per_iter
        for u in range(blk_per_iter):
            r0 = pl.multiple_of((b0 + u) * stride, SUBLANES)
            gm_ref[b0 + u] = w_scr[pl.ds(r0, t), :].astype(gm_ref.dtype)
        for u in range(ways2):
            h = ways2 * hh + u
            sv0, sv1 = sv_scr[2 * h], sv_scr[2 * h + 1]
            si0, si1 = si_scr[2 * h], si_scr[2 * h + 1]
            cand, cidx = [], []
            for a_pat, b_pat, n_valid in layout:
                c = _rows_by_pattern(sv0, a_pat, sub) + _rows_by_pattern(sv1, b_pat, sub)
                if n_valid < SUBLANES:
                    c = jnp.where(sub < n_valid, c, -jnp.inf)
                cand.append(c)
                cidx.append(_rows_by_pattern(si0, a_pat, sub) * n_keys + _rows_by_pattern(si1, b_pat, sub))
            while len(cand) & (len(cand) - 1):
                cand.append(jnp.full((SUBLANES, t), -jnp.inf, F32))
                cidx.append(jnp.zeros((SUBLANES, t), jnp.int32))
            fv, _, eidx = _topk_rows_sorted(jnp.concatenate(cand, axis=0), K, jnp.concatenate(cidx, axis=0))
            e = jnp.exp(fv - fv[0:1, :])
            g = e / jnp.sum(e, axis=0, keepdims=True)
            r0 = pl.multiple_of(h * K, K)
            it_scr[pl.ds(r0, K), :] = eidx // n_keys
            jt_scr[pl.ds(r0, K), :] = eidx % n_keys
            gt_scr[pl.ds(r0, K), :] = g
        return 0

    lax.fori_loop(0, n_loop, stage2, 0, unroll=True)
    il_scr[cur] = it_scr[...].T
    jl_scr[cur] = jt_scr[...].T
    gl_scr[cur] = gt_scr[...].T


def _route(q, keys, n_heads, n_keys):
    n = q.shape[0]
    t = LANES
    n_tiles = n // t
    slots = n_heads * P_TOPK
    stride = t + SUBLANES
    kern = functools.partial(_route_kernel, n_heads=n_heads, n_keys=n_keys, stride=stride)
    return pl.pallas_call(
        kern,
        grid=(n_tiles + 1,),
        in_specs=[pl.BlockSpec((t, q.shape[1]), lambda s: (jnp.minimum(s, n_tiles - 1), 0)),
                  pl.BlockSpec(keys.shape, lambda s: (0, 0, 0))],
        out_specs=pl.BlockSpec((n_keys, t, n_keys), lambda s: (0, jnp.maximum(s - 1, 0), 0)),
        out_shape=jax.ShapeDtypeStruct((n_keys, n, n_keys), BF16),
        scratch_shapes=[pltpu.VMEM((2 * n_heads, P_TOPK, t), F32),
                        pltpu.VMEM((2 * n_heads, P_TOPK, t), jnp.int32),
                        pltpu.VMEM((slots, t), jnp.int32),
                        pltpu.VMEM((slots, t), jnp.int32),
                        pltpu.VMEM((slots, t), F32),
                        pltpu.VMEM((2, t, slots), jnp.int32),
                        pltpu.VMEM((2, t, slots), jnp.int32),
                        pltpu.VMEM((2, t, slots), F32),
                        pltpu.VMEM((n_keys * stride, n_keys), F32)],
        compiler_params=_params("arbitrary"),
        name="route",
    )(q, keys)


def _experts_kernel(h2_ref, u_ref, v_ref, m_ref, x1_ref, g2_ref, fg_ref, o_ref, acc_ref, *, final):
    j = pl.program_id(2)

    @pl.when(j == 0)
    def _():
        acc_ref[...] = jnp.zeros_like(acc_ref)

    s = lax.dot_general(h2_ref[0], u_ref[...].astype(BF16), NT, preferred_element_type=F32)
    act = 0.5 * s * (1.0 + lax.erf(s * (2.0 ** -0.5)))
    gate = jnp.concatenate([m_ref[ib] for ib in range(m_ref.shape[0])], axis=1)
    w = (gate.astype(F32) * act).astype(BF16)
    acc_ref[...] += jnp.dot(w, v_ref[...].astype(BF16), preferred_element_type=F32)

    @pl.when(j == pl.num_programs(2) - 1)
    def _():
        x2 = x1_ref[0] + g2_ref[0] * acc_ref[...]
        if final:
            x2 = x2 * lax.rsqrt(jnp.mean(x2 * x2, axis=-1, keepdims=True) + EPS) * fg_ref[...]
        o_ref[0] = x2


def _experts(h2, u, v, gmat, x1, g2, final_g, final, tm=1024, te=1024):
    bsz, s, d = x1.shape
    tm = min(tm, s)
    ne = u.shape[0]
    n_keys = gmat.shape[2]
    n_tiles = s // tm
    kern = functools.partial(_experts_kernel, final=final)
    return pl.pallas_call(
        kern,
        grid=(bsz, n_tiles, ne // te),
        in_specs=[pl.BlockSpec((1, tm, d), lambda b, i, j: (b, i, 0)),
                  pl.BlockSpec((te, d), lambda b, i, j: (j, 0)),
                  pl.BlockSpec((te, d), lambda b, i, j: (j, 0)),
                  pl.BlockSpec((te // n_keys, tm, n_keys), lambda b, i, j: (j, b * n_tiles + i, 0)),
                  pl.BlockSpec((1, tm, d), lambda b, i, j: (b, i, 0)),
                  pl.BlockSpec((1, 1, d), lambda b, i, j: (b, 0, 0)),
                  pl.BlockSpec((1, d), lambda b, i, j: (0, 0))],
        out_specs=pl.BlockSpec((1, tm, d), lambda b, i, j: (b, i, 0)),
        out_shape=jax.ShapeDtypeStruct((bsz, s, d), F32),
        scratch_shapes=[pltpu.VMEM((tm, d), F32)],
        compiler_params=_params("arbitrary", "arbitrary", "arbitrary"),
        name="experts",
    )(h2, u, v, gmat, x1, g2, final_g)


def kernel(x, c, ada_w, ada_b, norm1_g, w_in, conv_w, conv_b, mlstm_gate_b, mlstm_norm_g, lambda_q1,
           lambda_k1, lambda_q2, lambda_k2, diff_norm_g, w_out, norm2_g, peer_w_query, peer_sub_keys,
           peer_u, peer_v, final_g):
    bsz, s, d = x.shape
    depth = ada_w.shape[0]
    m_heads = mlstm_gate_b.shape[1] // 2
    m_width = mlstm_norm_g.shape[1]
    m_hd = m_width // m_heads
    dh = lambda_q1.shape[1]
    d_width = w_out.shape[1] - m_width
    d_heads = d_width // (2 * dh)
    p_heads, _, n_keys, _ = peer_sub_keys.shape[1:]
    n_m = 4 * m_width
    n_g = 2 * m_heads
    assert s % M_CHUNK == 0 and m_hd == LANES and 2 * dh == LANES and n_keys == LANES
    assert peer_sub_keys.shape[-1] == LANES and p_heads % 2 == 0 and n_keys == SUBLANES * P_TOPK

    for l in range(depth):
        mod = _adaln(c, ada_w[l], ada_b[l])
        sh1, sc1, g1, sh2, sc2, g2 = (a.reshape(bsz, 1, d) for a in jnp.split(mod, 6, axis=-1))

        w = w_in[l].astype(BF16)
        w_m = w[:, :n_m]
        w_g = jnp.pad(w[:, n_m:n_m + n_g], ((0, 0), (0, LANES - n_g)))
        w_d = w[:, n_m + n_g:]
        pm, pg, pd = _inproj(x, sh1, sc1, norm1_g[l].reshape(1, d), w_m, w_g, w_d)

        gates = pg[:, :, :n_g].transpose(0, 2, 1).reshape(bsz, n_g, s // M_CHUNK, M_CHUNK)
        hm = _mlstm(pm, gates, mlstm_gate_b[l], conv_w[l], conv_b[l].reshape(1, -1),
                    mlstm_norm_g[l].reshape(1, -1), m_heads, m_hd)

        lam_init = 0.8 - 0.6 * math.exp(-0.3 * l)
        lam_p = jnp.stack([lambda_q1[l], lambda_k1[l], lambda_q2[l], lambda_k2[l]])
        od = _diffattn(pd, lam_p, diff_norm_g[l].reshape(1, -1), d_heads, dh, lam_init)

        wo = w_out[l].astype(BF16)
        x1, h2, q = _outproj(hm, od, x, g1, sh2, sc2, norm2_g[l].reshape(1, d), wo[:m_width], wo[m_width:],
                             peer_w_query[l].astype(BF16))

        keys = peer_sub_keys[l].reshape(2 * p_heads, n_keys, -1).astype(BF16)
        gmat = _route(q.reshape(bsz * s, -1), keys, p_heads, n_keys)
        x = _experts(h2, peer_u[l], peer_v[l], gmat, x1, g2,
                     final_g.reshape(1, d), final=(l == depth - 1))
    return x
```

```python
import functools
import math

import jax
import jax.numpy as jnp
from jax import lax
from jax.experimental import pallas as pl
from jax.experimental.pallas import tpu as pltpu

F32 = jnp.float32
BF16 = jnp.bfloat16
EPS = 1e-6
LANES = 128
SUBLANES = 8
VMEM_LIMIT = 56 * 1024 * 1024
M_CHUNK = 128
CONV_WIDTH = 4
P_TOPK = 16

NT = (((1,), (1,)), ((), ()))
TN = (((0,), (0,)), ((), ()))


def _params(*sem):
    return pltpu.CompilerParams(dimension_semantics=sem, vmem_limit_bytes=VMEM_LIMIT)


def _adaln_kernel(c_ref, w_ref, b_ref, o_ref):
    c = c_ref[...]
    sc = (c * jax.nn.sigmoid(c)).astype(BF16)
    o_ref[...] = jnp.dot(sc, w_ref[...].astype(BF16), preferred_element_type=F32) + b_ref[...]


def _adaln(c, w, b):
    bsz, d = c.shape
    n = w.shape[1]
    tn = 1536
    return pl.pallas_call(
        _adaln_kernel,
        grid=(n // tn,),
        in_specs=[pl.BlockSpec((bsz, d), lambda j: (0, 0)),
                  pl.BlockSpec((d, tn), lambda j: (0, j)),
                  pl.BlockSpec((1, tn), lambda j: (0, j))],
        out_specs=pl.BlockSpec((bsz, tn), lambda j: (0, j)),
        out_shape=jax.ShapeDtypeStruct((bsz, n), F32),
        compiler_params=_params("arbitrary"),
        name="adaln",
    )(c, w, b.reshape(1, n))


def _inproj_kernel(x_ref, sh_ref, sc_ref, g_ref, wm_ref, wg_ref, wd_ref, pm_ref, pg_ref, pd_ref):
    x = x_ref[0]
    ms = jnp.mean(x * x, axis=-1, keepdims=True)
    h = x * lax.rsqrt(ms + EPS) * g_ref[...]
    h = h * (1.0 + sc_ref[0]) + sh_ref[0]
    hb = h.astype(BF16)
    pm_ref[0] = jnp.dot(hb, wm_ref[...], preferred_element_type=F32)
    pg_ref[0] = jnp.dot(hb, wg_ref[...], preferred_element_type=F32)
    pd_ref[0] = jnp.dot(hb, wd_ref[...], preferred_element_type=F32).astype(BF16)


def _inproj(x, sh, sc, g, wm, wg, wd, tm=512):
    bsz, s, d = x.shape
    tm = min(tm, s)
    nm, ng, nd = wm.shape[1], wg.shape[1], wd.shape[1]
    tok = lambda b, i: (b, i, 0)
    per_b = lambda b, i: (b, 0, 0)
    const = lambda b, i: (0, 0)
    return pl.pallas_call(
        _inproj_kernel,
        grid=(bsz, s // tm),
        in_specs=[pl.BlockSpec((1, tm, d), tok),
                  pl.BlockSpec((1, 1, d), per_b),
                  pl.BlockSpec((1, 1, d), per_b),
                  pl.BlockSpec((1, d), const),
                  pl.BlockSpec((d, nm), const),
                  pl.BlockSpec((d, ng), const),
                  pl.BlockSpec((d, nd), const)],
        out_specs=[pl.BlockSpec((1, tm, nm), tok),
                   pl.BlockSpec((1, tm, ng), tok),
                   pl.BlockSpec((1, tm, nd), tok)],
        out_shape=[jax.ShapeDtypeStruct((bsz, s, nm), F32),
                   jax.ShapeDtypeStruct((bsz, s, ng), F32),
                   jax.ShapeDtypeStruct((bsz, s, nd), BF16)],
        compiler_params=_params("arbitrary", "arbitrary"),
        name="inproj",
    )(x, sh, sc, g, wm, wg, wd)


def _causal_conv(u, tail, w, b):
    ext = jnp.concatenate([tail, u], axis=0)
    y = u * w[CONV_WIDTH - 1:CONV_WIDTH, :]
    for k in range(CONV_WIDTH - 1):
        shift = CONV_WIDTH - 1 - k
        y = y + pltpu.roll(ext, shift, axis=0)[SUBLANES:, :] * w[k:k + 1, :]
    return y + b


def _cumsum_lanes(x):
    lane = lax.broadcasted_iota(jnp.int32, x.shape, 1)
    s = 1
    while s < x.shape[1]:
        x = x + jnp.where(lane >= s, pltpu.roll(x, s, axis=1), 0.0)
        s *= 2
    return x


def _mlstm_kernel(gb_ref, mq_ref, mk_ref, mv_ref, mo_ref, g_ref, cwq_ref, cwk_ref, cbq_ref, cbk_ref,
                  ng_ref, o_ref, b_scr, i_scr, c_scr, *, n_heads, head_dim, group):
    h0 = pl.program_id(1) * group
    L = M_CHUNK
    hd = head_dim
    nc = b_scr.shape[1]
    silu = lambda a: a * jax.nn.sigmoid(a)

    for u in range(group):
        i_scr[u] = g_ref[0, h0 + u] + gb_ref[h0 + u]
        fpre = g_ref[0, n_heads + h0 + u] + gb_ref[n_heads + h0 + u]
        lf = jnp.minimum(fpre, 0.0) - jnp.log1p(jnp.exp(-jnp.abs(fpre)))
        b_scr[u] = _cumsum_lanes(lf)

    r_i = lax.broadcasted_iota(jnp.int32, (L, L), 0)
    c_i = lax.broadcasted_iota(jnp.int32, (L, L), 1)
    eye = r_i == c_i
    causal = c_i <= r_i

    def to_col(row):
        return jnp.sum(jnp.where(eye, row, 0.0), axis=1, keepdims=True)

    def chunk(c, carry):
        t0 = pl.multiple_of(c * L, L)
        return tuple(head_chunk(c, t0, u, carry[u]) for u in range(group))

    def head_chunk(c, t0, u, carry):
        n, m, q_tail, k_tail = carry
        cols = slice(u * hd, (u + 1) * hd)
        q_raw = mq_ref[0, pl.ds(t0, L), cols]
        k_raw = mk_ref[0, pl.ds(t0, L), cols]
        qc = silu(_causal_conv(q_raw, q_tail, cwq_ref[:, cols], cbq_ref[:, cols])).astype(BF16)
        kc = (silu(_causal_conv(k_raw, k_tail, cwk_ref[:, cols], cbk_ref[:, cols])) * (hd ** -0.5)).astype(BF16)
        vc = mv_ref[0, pl.ds(t0, L), cols].astype(BF16)
        b_row = b_scr[u, pl.ds(c, 1), :]
        i_row = i_scr[u, pl.ds(c, 1), :]
        b_col = to_col(b_row)
        i_col = to_col(i_row)

        logD = jnp.where(causal, b_col - b_row + i_row, -jnp.inf)
        m_t = jnp.maximum(b_col + m, jnp.max(logD, axis=1, keepdims=True))
        Dw = jnp.exp(logD - m_t)
        inter = jnp.exp(b_col + m - m_t)
        sqk = lax.dot_general(qc, kc, NT, preferred_element_type=F32) * Dw
        num = (jnp.dot(sqk.astype(BF16), vc, preferred_element_type=F32)
               + inter * jnp.dot(qc, c_scr[u].astype(BF16), preferred_element_type=F32))
        den = (jnp.sum(sqk, axis=1, keepdims=True)
               + inter * jnp.sum(qc.astype(F32) * n, axis=1, keepdims=True))
        hh = num / jnp.maximum(jnp.abs(den), jnp.exp(-m_t))

        bL = b_row[:, L - 1:L]
        m_new = jnp.maximum(bL + m, jnp.max(bL - b_row + i_row, axis=1, keepdims=True))
        w_col = jnp.exp(bL - b_col + i_col - m_new)
        decay = jnp.exp(bL + m - m_new)
        kw = kc.astype(F32) * w_col
        c_scr[u] = decay * c_scr[u] + lax.dot_general(kw.astype(BF16), vc, TN, preferred_element_type=F32)
        n_new = decay * n + jnp.sum(kw, axis=0, keepdims=True)

        y = hh * lax.rsqrt(jnp.mean(hh * hh, axis=-1, keepdims=True) + EPS) * ng_ref[:, cols]
        y = y * jax.nn.sigmoid(mo_ref[0, pl.ds(t0, L), cols])
        o_ref[0, pl.ds(t0, L), cols] = y.astype(o_ref.dtype)
        return n_new, m_new, q_raw[L - SUBLANES:, :], k_raw[L - SUBLANES:, :]

    zeros = lambda *shape: jnp.zeros(shape, F32)
    c_scr[...] = zeros(*c_scr.shape)
    init = (zeros(1, hd), zeros(1, 1), zeros(SUBLANES, hd), zeros(SUBLANES, hd))
    lax.fori_loop(0, nc, chunk, (init,) * group)


def _mlstm(pm, gates, gate_b, conv_w, conv_b, norm_g, n_heads, head_dim, group=2):
    bsz, s, _ = pm.shape
    nc = s // M_CHUNK
    gw = group * head_dim
    n_groups = n_heads // group
    col = lambda off: (lambda b, g: (b, 0, off * n_groups + g))
    wcol = lambda off: (lambda b, g: (0, off * n_groups + g))
    kern = functools.partial(_mlstm_kernel, n_heads=n_heads, head_dim=head_dim, group=group)
    return pl.pallas_call(
        kern,
        grid=(bsz, n_groups),
        in_specs=[pl.BlockSpec(memory_space=pltpu.SMEM),
                  pl.BlockSpec((1, s, gw), col(0)),
                  pl.BlockSpec((1, s, gw), col(1)),
                  pl.BlockSpec((1, s, gw), col(2)),
                  pl.BlockSpec((1, s, gw), col(3)),
                  pl.BlockSpec((1, 2 * n_heads, nc, M_CHUNK), lambda b, g: (b, 0, 0, 0)),
                  pl.BlockSpec((CONV_WIDTH, gw), wcol(0)),
                  pl.BlockSpec((CONV_WIDTH, gw), wcol(1)),
                  pl.BlockSpec((1, gw), wcol(0)),
                  pl.BlockSpec((1, gw), wcol(1)),
                  pl.BlockSpec((1, gw), wcol(0))],
        out_specs=pl.BlockSpec((1, s, gw), col(0)),
        out_shape=jax.ShapeDtypeStruct((bsz, s, n_heads * head_dim), BF16),
        scratch_shapes=[pltpu.VMEM((group, nc, M_CHUNK), F32), pltpu.VMEM((group, nc, M_CHUNK), F32),
                        pltpu.VMEM((group, head_dim, head_dim), F32)],
        compiler_params=_params("arbitrary", "arbitrary"),
        name="mlstm",
    )(gate_b, pm, pm, pm, pm, gates, conv_w, conv_w, conv_b, conv_b, norm_g)


def _diffattn_kernel(q_ref, k_ref, v_ref, lam_ref, ng_ref, o_ref, m_scr, l_scr, acc_scr,
                     *, dh, lam_init, group):
    i = pl.program_id(2)
    tq = q_ref.shape[1]
    hw = 2 * dh
    scale = dh ** -0.5
    exact_scale = math.frexp(scale)[0] == 0.5
    lane = lax.broadcasted_iota(jnp.int32, (tq, hw), 1)
    qs = []
    for u in range(group):
        q = q_ref[0, :, u * hw:(u + 1) * hw]
        if exact_scale:
            q = q * scale
        zero = jnp.zeros_like(q)
        qs += [jnp.where(lane < dh, q, zero), jnp.where(lane < dh, zero, q)]
    lp = lam_ref[...]
    lam = (jnp.exp(jnp.sum(lp[0:1] * lp[1:2], axis=1, keepdims=True))
           - jnp.exp(jnp.sum(lp[2:3] * lp[3:4], axis=1, keepdims=True)) + lam_init)
    on_or_below_diag = (lax.broadcasted_iota(jnp.int32, (tq, tq), 1)
                        <= lax.broadcasted_iota(jnp.int32, (tq, tq), 0))

    def block(k0, width, diagonal):
        for st in range(2 * group):
            cols = slice((st // 2) * hw, (st // 2 + 1) * hw)
            kb = k_ref[0, pl.ds(k0, width), cols]
            vb = v_ref[0, pl.ds(k0, width), cols]
            s = lax.dot_general(qs[st], kb, NT, preferred_element_type=F32)
            if not exact_scale:
                s = s * scale
            row_max = lambda a: jnp.broadcast_to(jnp.max(a, axis=1, keepdims=True), (tq, LANES))
            if diagonal:
                s = jnp.where(on_or_below_diag, s, -jnp.inf)
                m_new = row_max(s)
            else:
                m_old = m_scr[st]
                m_new = jnp.maximum(m_old, row_max(s))
                alpha = jnp.exp(m_old - m_new)
            e = [jnp.exp(s[:, c:c + LANES] - m_new) for c in range(0, width, LANES)]
            part = functools.reduce(jnp.add, e)
            pv = jnp.dot(jnp.concatenate([x.astype(BF16) for x in e], axis=1), vb, preferred_element_type=F32)
            l_scr[st] = part if diagonal else alpha * l_scr[st] + part
            acc_scr[st] = pv if diagonal else alpha * acc_scr[st] + pv
            m_scr[st] = m_new

    def below(j, carry):
        block(pl.multiple_of(j * tq, tq), tq, False)
        return carry

    block(pl.multiple_of(i * tq, tq), tq, True)
    lax.fori_loop(0, i, below, 0)
    for u in range(group):
        o1 = acc_scr[2 * u] / jnp.sum(l_scr[2 * u], axis=1, keepdims=True)
        o2 = acc_scr[2 * u + 1] / jnp.sum(l_scr[2 * u + 1], axis=1, keepdims=True)
        o = o1 - lam * o2
        y = o * lax.rsqrt(jnp.mean(o * o, axis=-1, keepdims=True) + EPS) * ng_ref[...] * (1.0 - lam_init)
        o_ref[0, :, u * hw:(u + 1) * hw] = y.astype(o_ref.dtype)


def _diffattn(pd, lam_p, norm_g, n_heads, dh, lam_init, tq=512, group=4):
    bsz, s, _ = pd.shape
    tq = min(tq, s)
    hw = 2 * dh
    gw = group * hw
    n_groups = n_heads // group
    kern = functools.partial(_diffattn_kernel, dh=dh, lam_init=lam_init, group=group)
    return pl.pallas_call(
        kern,
        grid=(bsz, n_groups, s // tq),
        in_specs=[pl.BlockSpec((1, tq, gw), lambda b, g, i: (b, i, g)),
                  pl.BlockSpec((1, s, gw), lambda b, g, i: (b, 0, n_groups + g)),
                  pl.BlockSpec((1, s, gw), lambda b, g, i: (b, 0, 2 * n_groups + g)),
                  pl.BlockSpec((4, dh), lambda b, g, i: (0, 0)),
                  pl.BlockSpec((1, hw), lambda b, g, i: (0, 0))],
        out_specs=pl.BlockSpec((1, tq, gw), lambda b, g, i: (b, i, g)),
        out_shape=jax.ShapeDtypeStruct((bsz, s, n_heads * hw), BF16),
        scratch_shapes=[pltpu.VMEM((2 * group, tq, LANES), F32), pltpu.VMEM((2 * group, tq, LANES), F32),
                        pltpu.VMEM((2 * group, tq, hw), F32)],
        compiler_params=_params("arbitrary", "arbitrary", "arbitrary"),
        name="diffattn",
    )(pd, pd, pd, lam_p, norm_g)


def _outproj_kernel(hm_ref, od_ref, x_ref, g1_ref, sh_ref, sc_ref, ng_ref, wm_ref, wd_ref, wq_ref,
                    x1_ref, h2_ref, q_ref):
    y = (jnp.dot(hm_ref[0], wm_ref[...], preferred_element_type=F32)
         + jnp.dot(od_ref[0], wd_ref[...], preferred_element_type=F32))
    x1 = x_ref[0] + g1_ref[0] * y
    x1_ref[0] = x1
    ms = jnp.mean(x1 * x1, axis=-1, keepdims=True)
    h2 = x1 * lax.rsqrt(ms + EPS) * ng_ref[...]
    h2 = (h2 * (1.0 + sc_ref[0]) + sh_ref[0]).astype(BF16)
    h2_ref[0] = h2
    q_ref[0] = jnp.dot(h2, wq_ref[...], preferred_element_type=F32).astype(BF16)


def _outproj(hm, od, x, g1, sh2, sc2, ng, wm, wd, wq, tm=512):
    bsz, s, d = x.shape
    tm = min(tm, s)
    wm_w, wd_w, nq = hm.shape[2], od.shape[2], wq.shape[1]
    tok = lambda b, i: (b, i, 0)
    per_b = lambda b, i: (b, 0, 0)
    const = lambda b, i: (0, 0)
    return pl.pallas_call(
        _outproj_kernel,
        grid=(bsz, s // tm),
        in_specs=[pl.BlockSpec((1, tm, wm_w), tok),
                  pl.BlockSpec((1, tm, wd_w), tok),
                  pl.BlockSpec((1, tm, d), tok),
                  pl.BlockSpec((1, 1, d), per_b),
                  pl.BlockSpec((1, 1, d), per_b),
                  pl.BlockSpec((1, 1, d), per_b),
                  pl.BlockSpec((1, d), const),
                  pl.BlockSpec((wm_w, d), const),
                  pl.BlockSpec((wd_w, d), const),
                  pl.BlockSpec((d, nq), const)],
        out_specs=[pl.BlockSpec((1, tm, d), tok),
                   pl.BlockSpec((1, tm, d), tok),
                   pl.BlockSpec((1, tm, nq), tok)],
        out_shape=[jax.ShapeDtypeStruct((bsz, s, d), F32),
                   jax.ShapeDtypeStruct((bsz, s, d), BF16),
                   jax.ShapeDtypeStruct((bsz, s, nq), BF16)],
        compiler_params=_params("arbitrary", "arbitrary"),
        name="outproj",
    )(hm, od, x, g1, sh2, sc2, ng, wm, wd, wq)


def _oddeven_merge_sort(n):
    def merge(lo, hi, r):
        step = 2 * r
        if step < hi - lo:
            yield from merge(lo, hi, step)
            yield from merge(lo + r, hi, step)
            yield from ((i, i + r) for i in range(lo + r, hi - r, step))
        else:
            yield (lo, lo + r)

    def sort(lo, hi):
        if hi > lo:
            mid = lo + (hi - lo) // 2
            yield from sort(lo, mid)
            yield from sort(mid + 1, hi)
            yield from merge(lo, hi, 1)

    return list(sort(0, n - 1))


def _topk_rows_sorted(s, k, payload=None):
    n_rows, t = s.shape
    nv = n_rows // SUBLANES
    assert nv & (nv - 1) == 0
    slabs = lambda a: [a[SUBLANES * v:SUBLANES * (v + 1), :] for v in range(nv)]
    sub = lax.broadcasted_iota(jnp.int32, (SUBLANES, t), 0)
    out_row = lax.broadcasted_iota(jnp.int32, (k, t), 0)
    val = slabs(s)
    pay = None if payload is None else slabs(payload)
    sid = list(range(nv))
    for i, j in _oddeven_merge_sort(nv):
        va, vb, ia, ib = val[i], val[j], sid[i], sid[j]
        if isinstance(ia, int) and isinstance(ib, int):
            first = va >= vb if ia < ib else va > vb
        else:
            first = (va > vb) | ((va == vb) & (ia < ib))
        val[i], val[j] = jnp.where(first, va, vb), jnp.where(first, vb, va)
        sid[i], sid[j] = jnp.where(first, ia, ib), jnp.where(first, ib, ia)
        if pay is not None:
            pay[i], pay[j] = jnp.where(first, pay[i], pay[j]), jnp.where(first, pay[j], pay[i])
    row = [x * SUBLANES + sub for x in sid]
    vals = jnp.zeros((k, t), s.dtype)
    rows = jnp.zeros((k, t), jnp.int32)
    pays = None if pay is None else jnp.zeros((k, t), payload.dtype)
    for r in range(k):
        m = jnp.max(val[0], axis=0, keepdims=True)
        idx = jnp.min(jnp.where(val[0] == m, row[0], n_rows), axis=0, keepdims=True)
        hit = row[0] == idx
        vals = jnp.where(out_row == r, m, vals)
        rows = jnp.where(out_row == r, idx, rows)
        if pay is not None:
            pays = jnp.where(out_row == r, jnp.sum(jnp.where(hit, pay[0], 0), axis=0, keepdims=True), pays)
        live = min(nv, k - 1 - r)
        for q in range(min(live, nv - 1)):
            val[q] = jnp.where(hit, val[q + 1], val[q])
            row[q] = jnp.where(hit, row[q + 1], row[q])
            if pay is not None:
                pay[q] = jnp.where(hit, pay[q + 1], pay[q])
        if live == nv:
            val[nv - 1] = jnp.where(hit, -jnp.inf, val[nv - 1])
    return vals, rows, pays


def _candidate_layout(k):
    slabs, cur = [], []

    def flush():
        n_valid = len(cur)
        while len(cur) < SUBLANES:
            r = len(cur)
            period = 1
            while period < max(b for _, b in cur[:n_valid]) + 1:
                period *= 2
            b = cur[r - period][1] if r >= period else r
            cur.append((cur[n_valid - 1][0], b))
        slabs.append(([a for a, _ in cur], [b for _, b in cur], n_valid))
        cur.clear()

    for a in range(k):
        nb = k // (a + 1)
        for b0 in range(0, nb, SUBLANES):
            group = [(a, b) for b in range(b0, min(nb, b0 + SUBLANES))]
            if cur and (len(cur) + len(group) > SUBLANES or (nb == 1 and a % SUBLANES == 0)):
                flush()
            cur.extend(group)
    if cur:
        flush()
    return slabs


def _rows_by_pattern(x, pattern, sub):
    p0 = pattern[0]
    if p0 % SUBLANES == 0 and pattern == list(range(p0, p0 + SUBLANES)):
        return x[p0:p0 + SUBLANES, :]
    period = SUBLANES
    while period > 1 and all(pattern[r] == pattern[r % (period // 2)] for r in range(SUBLANES)):
        period //= 2
    pos = sub if period == SUBLANES else sub & (period - 1)
    out = x[p0:p0 + 1, :]
    for r in range(1, period):
        if pattern[r] != pattern[r - 1]:
            out = jnp.where(pos >= r, x[pattern[r]:pattern[r] + 1, :], out)
    return jnp.broadcast_to(out, (SUBLANES, x.shape[1]))


def _route_kernel(q_ref, keys_ref, gm_ref, sv_scr, si_scr, it_scr, jt_scr, gt_scr, il_scr, jl_scr, gl_scr,
                  w_scr, *, n_heads, n_keys, stride):
    K = P_TOPK
    t = q_ref.shape[0]
    slots = n_heads * K
    step = pl.program_id(0)
    cur = step % 2
    prev = 1 - cur
    sub = lax.broadcasted_iota(jnp.int32, (SUBLANES, t), 0)
    key_row = lax.broadcasted_iota(jnp.int32, (n_keys, slots), 0)
    layout = _candidate_layout(K)
    ways2 = 4
    n_loop = n_heads // ways2
    blk_per_iter = n_keys // n_loop

    @pl.when(step == 0)
    def _():
        il_scr[1] = jnp.zeros((t, slots), jnp.int32)
        jl_scr[1] = jnp.zeros((t, slots), jnp.int32)
        gl_scr[1] = jnp.zeros((t, slots), F32)

    def gate_rows(tok):
        i_row = il_scr[prev, pl.ds(tok, 1), :]
        j_row = jl_scr[prev, pl.ds(tok, 1), :]
        g_row = gl_scr[prev, pl.ds(tok, 1), :]
        a_t = jnp.where(key_row == i_row, 1.0, 0.0).astype(BF16)
        b_t = jnp.where(key_row == j_row, g_row, 0.0).astype(BF16)
        m_t = lax.dot_general(a_t, b_t, NT, preferred_element_type=F32)
        w_scr[pl.ds(tok, n_keys, stride=stride), :] = m_t

    ways = 4
    n_loop1 = 2 * n_heads // ways
    tok_per_iter = t // n_loop1

    def stage1(hh, _):
        for p in range(ways):
            hp = ways * hh + p
            c0 = pl.multiple_of(hp * LANES, LANES)
            qb = q_ref[:, pl.ds(c0, LANES)]
            st = lax.dot_general(keys_ref[hp], qb, NT, preferred_element_type=F32)
            v, ix, _ = _topk_rows_sorted(st, K)
            sv_scr[hp] = v
            si_scr[hp] = ix
        t0 = pl.multiple_of(hh * tok_per_iter, tok_per_iter)
        for u in range(tok_per_iter):
            gate_rows(t0 + u)
        return 0

    lax.fori_loop(0, n_loop1, stage1, 0, unroll=True)

    def stage2(hh, _):
        b0 = hh * blk_per_iter
        for u in range(blk_per_iter):
            r0 = pl.multiple_of((b0 + u) * stride, SUBLANES)
            gm_ref[b0 + u] = w_scr[pl.ds(r0, t), :].astype(gm_ref.dtype)
        for u in range(ways2):
            h = ways2 * hh + u
            sv0, sv1 = sv_scr[2 * h], sv_scr[2 * h + 1]
            si0, si1 = si_scr[2 * h], si_scr[2 * h + 1]
            cand, cidx = [], []
            for a_pat, b_pat, n_valid in layout:
                c = _rows_by_pattern(sv0, a_pat, sub) + _rows_by_pattern(sv1, b_pat, sub)
                if n_valid < SUBLANES:
                    c = jnp.where(sub < n_valid, c, -jnp.inf)
                cand.append(c)
                cidx.append(_rows_by_pattern(si0, a_pat, sub) * n_keys + _rows_by_pattern(si1, b_pat, sub))
            while len(cand) & (len(cand) - 1):
                cand.append(jnp.full((SUBLANES, t), -jnp.inf, F32))
                cidx.append(jnp.zeros((SUBLANES, t), jnp.int32))
            fv, _, eidx = _topk_rows_sorted(jnp.concatenate(cand, axis=0), K, jnp.concatenate(cidx, axis=0))
            e = jnp.exp(fv - fv[0:1, :])
            g = e / jnp.sum(e, axis=0, keepdims=True)
            r0 = pl.multiple_of(h * K, K)
            it_scr[pl.ds(r0, K), :] = eidx // n_keys
            jt_scr[pl.ds(r0, K), :] = eidx % n_keys
            gt_scr[pl.ds(r0, K), :] = g
        return 0

    lax.fori_loop(0, n_loop, stage2, 0, unroll=True)
    il_scr[cur] = it_scr[...].T
    jl_scr[cur] = jt_scr[...].T
    gl_scr[cur] = gt_scr[...].T


def _route(q, keys, n_heads, n_keys):
    n = q.shape[0]
    t = LANES
    n_tiles = n // t
    slots = n_heads * P_TOPK
    stride = t + SUBLANES
    kern = functools.partial(_route_kernel, n_heads=n_heads, n_keys=n_keys, stride=stride)
    return pl.pallas_call(
        kern,
        grid=(n_tiles + 1,),
        in_specs=[pl.BlockSpec((t, q.shape[1]), lambda s: (jnp.minimum(s, n_tiles - 1), 0)),
                  pl.BlockSpec(keys.shape, lambda s: (0, 0, 0))],
        out_specs=pl.BlockSpec((n_keys, t, n_keys), lambda s: (0, jnp.maximum(s - 1, 0), 0)),
        out_shape=jax.ShapeDtypeStruct((n_keys, n, n_keys), BF16),
        scratch_shapes=[pltpu.VMEM((2 * n_heads, P_TOPK, t), F32),
                        pltpu.VMEM((2 * n_heads, P_TOPK, t), jnp.int32),
                        pltpu.VMEM((slots, t), jnp.int32),
                        pltpu.VMEM((slots, t), jnp.int32),
                        pltpu.VMEM((slots, t), F32),
                        pltpu.VMEM((2, t, slots), jnp.int32),
                        pltpu.VMEM((2, t, slots), jnp.int32),
                        pltpu.VMEM((2, t, slots), F32),
                        pltpu.VMEM((n_keys * stride, n_keys), F32)],
        compiler_params=_params("arbitrary"),
        name="route",
    )(q, keys)


def _experts_kernel(h2_ref, u_ref, v_ref, m_ref, x1_ref, g2_ref, fg_ref, o_ref, acc_ref, *, final):
    j = pl.program_id(2)

    @pl.when(j == 0)
    def _():
        acc_ref[...] = jnp.zeros_like(acc_ref)

    s = lax.dot_general(h2_ref[0], u_ref[...].astype(BF16), NT, preferred_element_type=F32)
    act = 0.5 * s * (1.0 + lax.erf(s * (2.0 ** -0.5)))
    gate = jnp.concatenate([m_ref[ib] for ib in range(m_ref.shape[0])], axis=1)
    w = (gate.astype(F32) * act).astype(BF16)
    acc_ref[...] += jnp.dot(w, v_ref[...].astype(BF16), preferred_element_type=F32)

    @pl.when(j == pl.num_programs(2) - 1)
    def _():
        x2 = x1_ref[0] + g2_ref[0] * acc_ref[...]
        if final:
            x2 = x2 * lax.rsqrt(jnp.mean(x2 * x2, axis=-1, keepdims=True) + EPS) * fg_ref[...]
        o_ref[0] = x2


def _experts(h2, u, v, gmat, x1, g2, final_g, final, tm=1024, te=1024):
    bsz, s, d = x1.shape
    tm = min(tm, s)
    ne = u.shape[0]
    n_keys = gmat.shape[2]
    n_tiles = s // tm
    kern = functools.partial(_experts_kernel, final=final)
    return pl.pallas_call(
        kern,
        grid=(bsz, n_tiles, ne // te),
        in_specs=[pl.BlockSpec((1, tm, d), lambda b, i, j: (b, i, 0)),
                  pl.BlockSpec((te, d), lambda b, i, j: (j, 0)),
                  pl.BlockSpec((te, d), lambda b, i, j: (j, 0)),
                  pl.BlockSpec((te // n_keys, tm, n_keys), lambda b, i, j: (j, b * n_tiles + i, 0)),
                  pl.BlockSpec((1, tm, d), lambda b, i, j: (b, i, 0)),
                  pl.BlockSpec((1, 1, d), lambda b, i, j: (b, 0, 0)),
                  pl.BlockSpec((1, d), lambda b, i, j: (0, 0))],
        out_specs=pl.BlockSpec((1, tm, d), lambda b, i, j: (b, i, 0)),
        out_shape=jax.ShapeDtypeStruct((bsz, s, d), F32),
        scratch_shapes=[pltpu.VMEM((tm, d), F32)],
        compiler_params=_params("arbitrary", "arbitrary", "arbitrary"),
        name="experts",
    )(h2, u, v, gmat, x1, g2, final_g)


def kernel(x, c, ada_w, ada_b, norm1_g, w_in, conv_w, conv_b, mlstm_gate_b, mlstm_norm_g, lambda_q1,
           lambda_k1, lambda_q2, lambda_k2, diff_norm_g, w_out, norm2_g, peer_w_query, peer_sub_keys,
           peer_u, peer_v, final_g):
    bsz, s, d = x.shape
    depth = ada_w.shape[0]
    m_heads = mlstm_gate_b.shape[1] // 2
    m_width = mlstm_norm_g.shape[1]
    m_hd = m_width // m_heads
    dh = lambda_q1.shape[1]
    d_width = w_out.shape[1] - m_width
    d_heads = d_width // (2 * dh)
    p_heads, _, n_keys, _ = peer_sub_keys.shape[1:]
    n_m = 4 * m_width
    n_g = 2 * m_heads
    assert s % M_CHUNK == 0 and m_hd == LANES and 2 * dh == LANES and n_keys == LANES
    assert peer_sub_keys.shape[-1] == LANES and p_heads % 2 == 0 and n_keys == SUBLANES * P_TOPK

    for l in range(depth):
        mod = _adaln(c, ada_w[l], ada_b[l])
        sh1, sc1, g1, sh2, sc2, g2 = (a.reshape(bsz, 1, d) for a in jnp.split(mod, 6, axis=-1))

        w_m = w_in[l][:, :n_m].astype(BF16)
        w_g = jnp.pad(w_in[l][:, n_m:n_m + n_g].astype(BF16), ((0, 0), (0, LANES - n_g)))
        w_d = w_in[l][:, n_m + n_g:].astype(BF16)
        pm, pg, pd = _inproj(x, sh1, sc1, norm1_g[l].reshape(1, d), w_m, w_g, w_d)

        gates = pg[:, :, :n_g].transpose(0, 2, 1).reshape(bsz, n_g, s // M_CHUNK, M_CHUNK)
        hm = _mlstm(pm, gates, mlstm_gate_b[l], conv_w[l], conv_b[l].reshape(1, -1),
                    mlstm_norm_g[l].reshape(1, -1), m_heads, m_hd)

        lam_init = 0.8 - 0.6 * math.exp(-0.3 * l)
        lam_p = jnp.stack([lambda_q1[l], lambda_k1[l], lambda_q2[l], lambda_k2[l]])
        od = _diffattn(pd, lam_p, diff_norm_g[l].reshape(1, -1), d_heads, dh, lam_init)

        x1, h2, q = _outproj(hm, od, x, g1, sh2, sc2, norm2_g[l].reshape(1, d), w_out[l][:m_width].astype(BF16),
                             w_out[l][m_width:].astype(BF16), peer_w_query[l].astype(BF16))

        keys = peer_sub_keys[l].reshape(2 * p_heads, n_keys, -1).astype(BF16)
        gmat = _route(q.reshape(bsz * s, -1), keys, p_heads, n_keys)
        x = _experts(h2, peer_u[l], peer_v[l], gmat, x1, g2,
                     final_g.reshape(1, d), final=(l == depth - 1))
    return x
```

```python
import functools
import math

import jax
import jax.numpy as jnp
from jax import lax
from jax.experimental import pallas as pl
from jax.experimental.pallas import tpu as pltpu

F32 = jnp.float32
BF16 = jnp.bfloat16
EPS = 1e-6
LANES = 128
SUBLANES = 8
VMEM_LIMIT = 56 * 1024 * 1024
M_CHUNK = 128
CONV_WIDTH = 4
P_TOPK = 16

NT = (((1,), (1,)), ((), ()))
TN = (((0,), (0,)), ((), ()))


def _params(*sem):
    return pltpu.CompilerParams(dimension_semantics=sem, vmem_limit_bytes=VMEM_LIMIT)


def _adaln_kernel(c_ref, w_ref, b_ref, o_ref):
    c = c_ref[...]
    sc = (c * jax.nn.sigmoid(c)).astype(BF16)
    o_ref[...] = jnp.dot(sc, w_ref[...].astype(BF16), preferred_element_type=F32) + b_ref[...]


def _adaln(c, w, b):
    bsz, d = c.shape
    n = w.shape[1]
    tn = 1536
    return pl.pallas_call(
        _adaln_kernel,
        grid=(n // tn,),
        in_specs=[pl.BlockSpec((bsz, d), lambda j: (0, 0)),
                  pl.BlockSpec((d, tn), lambda j: (0, j)),
                  pl.BlockSpec((1, tn), lambda j: (0, j))],
        out_specs=pl.BlockSpec((bsz, tn), lambda j: (0, j)),
        out_shape=jax.ShapeDtypeStruct((bsz, n), F32),
        compiler_params=_params("arbitrary"),
        name="adaln",
    )(c, w, b.reshape(1, n))


def _inproj_kernel(x_ref, sh_ref, sc_ref, g_ref, wm_ref, wg_ref, wd_ref, pm_ref, pg_ref, pd_ref):
    x = x_ref[0]
    ms = jnp.mean(x * x, axis=-1, keepdims=True)
    h = x * lax.rsqrt(ms + EPS) * g_ref[...]
    h = h * (1.0 + sc_ref[0]) + sh_ref[0]
    hb = h.astype(BF16)
    pm_ref[0] = jnp.dot(hb, wm_ref[...], preferred_element_type=F32)
    pg_ref[0] = jnp.dot(hb, wg_ref[...], preferred_element_type=F32)
    pd_ref[0] = jnp.dot(hb, wd_ref[...], preferred_element_type=F32).astype(BF16)


def _inproj(x, sh, sc, g, wm, wg, wd, tm=512):
    bsz, s, d = x.shape
    tm = min(tm, s)
    nm, ng, nd = wm.shape[1], wg.shape[1], wd.shape[1]
    tok = lambda b, i: (b, i, 0)
    per_b = lambda b, i: (b, 0, 0)
    const = lambda b, i: (0, 0)
    return pl.pallas_call(
        _inproj_kernel,
        grid=(bsz, s // tm),
        in_specs=[pl.BlockSpec((1, tm, d), tok),
                  pl.BlockSpec((1, 1, d), per_b),
                  pl.BlockSpec((1, 1, d), per_b),
                  pl.BlockSpec((1, d), const),
                  pl.BlockSpec((d, nm), const),
                  pl.BlockSpec((d, ng), const),
                  pl.BlockSpec((d, nd), const)],
        out_specs=[pl.BlockSpec((1, tm, nm), tok),
                   pl.BlockSpec((1, tm, ng), tok),
                   pl.BlockSpec((1, tm, nd), tok)],
        out_shape=[jax.ShapeDtypeStruct((bsz, s, nm), F32),
                   jax.ShapeDtypeStruct((bsz, s, ng), F32),
                   jax.ShapeDtypeStruct((bsz, s, nd), BF16)],
        compiler_params=_params("arbitrary", "arbitrary"),
        name="inproj",
    )(x, sh, sc, g, wm, wg, wd)


def _causal_conv(u, tail, w, b):
    ext = jnp.concatenate([tail, u], axis=0)
    y = u * w[CONV_WIDTH - 1:CONV_WIDTH, :]
    for k in range(CONV_WIDTH - 1):
        shift = CONV_WIDTH - 1 - k
        y = y + pltpu.roll(ext, shift, axis=0)[SUBLANES:, :] * w[k:k + 1, :]
    return y + b


def _cumsum_lanes(x):
    lane = lax.broadcasted_iota(jnp.int32, x.shape, 1)
    s = 1
    while s < x.shape[1]:
        x = x + jnp.where(lane >= s, pltpu.roll(x, s, axis=1), 0.0)
        s *= 2
    return x


def _mlstm_kernel(gb_ref, mq_ref, mk_ref, mv_ref, mo_ref, g_ref, cwq_ref, cwk_ref, cbq_ref, cbk_ref,
                  ng_ref, o_ref, b_scr, i_scr, c_scr, *, n_heads, head_dim, group):
    h0 = pl.program_id(1) * group
    L = M_CHUNK
    hd = head_dim
    nc = b_scr.shape[1]
    silu = lambda a: a * jax.nn.sigmoid(a)

    for u in range(group):
        i_scr[u] = g_ref[0, h0 + u] + gb_ref[h0 + u]
        fpre = g_ref[0, n_heads + h0 + u] + gb_ref[n_heads + h0 + u]
        lf = jnp.minimum(fpre, 0.0) - jnp.log1p(jnp.exp(-jnp.abs(fpre)))
        b_scr[u] = _cumsum_lanes(lf)

    r_i = lax.broadcasted_iota(jnp.int32, (L, L), 0)
    c_i = lax.broadcasted_iota(jnp.int32, (L, L), 1)
    eye = r_i == c_i
    causal = c_i <= r_i

    full = lambda col: jnp.broadcast_to(col, (L, LANES))

    def to_col(row):
        return full(jnp.sum(jnp.where(eye, row, 0.0), axis=1, keepdims=True))

    def chunk(c, carry):
        t0 = pl.multiple_of(c * L, L)
        return tuple(head_chunk(c, t0, u, carry[u]) for u in range(group))

    def head_chunk(c, t0, u, carry):
        n, m, q_tail, k_tail = carry
        cols = slice(u * hd, (u + 1) * hd)
        q_raw = mq_ref[0, pl.ds(t0, L), cols]
        k_raw = mk_ref[0, pl.ds(t0, L), cols]
        qc = silu(_causal_conv(q_raw, q_tail, cwq_ref[:, cols], cbq_ref[:, cols])).astype(BF16)
        kc = (silu(_causal_conv(k_raw, k_tail, cwk_ref[:, cols], cbk_ref[:, cols])) * (hd ** -0.5)).astype(BF16)
        vc = mv_ref[0, pl.ds(t0, L), cols].astype(BF16)
        b_row = b_scr[u, pl.ds(c, 1), :]
        i_row = i_scr[u, pl.ds(c, 1), :]
        b_col = to_col(b_row)
        i_col = to_col(i_row)

        logD = jnp.where(causal, b_col - b_row + i_row, -jnp.inf)
        m_t = jnp.maximum(b_col + m, full(jnp.max(logD, axis=1, keepdims=True)))
        Dw = jnp.exp(logD - m_t)
        inter = jnp.exp(b_col + m - m_t)
        sqk = lax.dot_general(qc, kc, NT, preferred_element_type=F32) * Dw
        num = (jnp.dot(sqk.astype(BF16), vc, preferred_element_type=F32)
               + inter * jnp.dot(qc, c_scr[u].astype(BF16), preferred_element_type=F32))
        den = (full(jnp.sum(sqk, axis=1, keepdims=True))
               + inter * full(jnp.sum(qc.astype(F32) * n, axis=1, keepdims=True)))
        hh = num / jnp.maximum(jnp.abs(den), jnp.exp(-m_t))

        bL = b_row[:, L - 1:L]
        m_new = jnp.maximum(bL + m, jnp.max(bL - b_row + i_row, axis=1, keepdims=True))
        w_col = jnp.exp(bL - b_col + i_col - m_new)
        decay = jnp.exp(bL + m - m_new)
        kw = kc.astype(F32) * w_col
        c_scr[u] = decay * c_scr[u] + lax.dot_general(kw.astype(BF16), vc, TN, preferred_element_type=F32)
        n_new = decay * n + jnp.sum(kw, axis=0, keepdims=True)

        y = hh * lax.rsqrt(full(jnp.mean(hh * hh, axis=-1, keepdims=True)) + EPS) * ng_ref[:, cols]
        y = y * jax.nn.sigmoid(mo_ref[0, pl.ds(t0, L), cols])
        o_ref[0, pl.ds(t0, L), cols] = y.astype(o_ref.dtype)
        return n_new, m_new, q_raw[L - SUBLANES:, :], k_raw[L - SUBLANES:, :]

    zeros = lambda *shape: jnp.zeros(shape, F32)
    c_scr[...] = zeros(*c_scr.shape)
    init = (zeros(1, hd), zeros(1, 1), zeros(SUBLANES, hd), zeros(SUBLANES, hd))
    lax.fori_loop(0, nc, chunk, (init,) * group)


def _mlstm(pm, gates, gate_b, conv_w, conv_b, norm_g, n_heads, head_dim, group=4):
    bsz, s, _ = pm.shape
    nc = s // M_CHUNK
    gw = group * head_dim
    n_groups = n_heads // group
    col = lambda off: (lambda b, g: (b, 0, off * n_groups + g))
    wcol = lambda off: (lambda b, g: (0, off * n_groups + g))
    kern = functools.partial(_mlstm_kernel, n_heads=n_heads, head_dim=head_dim, group=group)
    return pl.pallas_call(
        kern,
        grid=(bsz, n_groups),
        in_specs=[pl.BlockSpec(memory_space=pltpu.SMEM),
                  pl.BlockSpec((1, s, gw), col(0)),
                  pl.BlockSpec((1, s, gw), col(1)),
                  pl.BlockSpec((1, s, gw), col(2)),
                  pl.BlockSpec((1, s, gw), col(3)),
                  pl.BlockSpec((1, 2 * n_heads, nc, M_CHUNK), lambda b, g: (b, 0, 0, 0)),
                  pl.BlockSpec((CONV_WIDTH, gw), wcol(0)),
                  pl.BlockSpec((CONV_WIDTH, gw), wcol(1)),
                  pl.BlockSpec((1, gw), wcol(0)),
                  pl.BlockSpec((1, gw), wcol(1)),
                  pl.BlockSpec((1, gw), wcol(0))],
        out_specs=pl.BlockSpec((1, s, gw), col(0)),
        out_shape=jax.ShapeDtypeStruct((bsz, s, n_heads * head_dim), BF16),
        scratch_shapes=[pltpu.VMEM((group, nc, M_CHUNK), F32), pltpu.VMEM((group, nc, M_CHUNK), F32),
                        pltpu.VMEM((group, head_dim, head_dim), F32)],
        compiler_params=_params("arbitrary", "arbitrary"),
        name="mlstm",
    )(gate_b, pm, pm, pm, pm, gates, conv_w, conv_w, conv_b, conv_b, norm_g)


def _diffattn_kernel(q_ref, k_ref, v_ref, lam_ref, ng_ref, o_ref, m_scr, l_scr, acc_scr,
                     *, dh, lam_init, group):
    i = pl.program_id(2)
    tq = q_ref.shape[1]
    hw = 2 * dh
    scale = dh ** -0.5
    exact_scale = math.frexp(scale)[0] == 0.5
    lane = lax.broadcasted_iota(jnp.int32, (tq, hw), 1)
    qs = []
    for u in range(group):
        q = q_ref[0, :, u * hw:(u + 1) * hw]
        if exact_scale:
            q = q * scale
        zero = jnp.zeros_like(q)
        qs += [jnp.where(lane < dh, q, zero), jnp.where(lane < dh, zero, q)]
    lp = lam_ref[...]
    lam = (jnp.exp(jnp.sum(lp[0:1] * lp[1:2], axis=1, keepdims=True))
           - jnp.exp(jnp.sum(lp[2:3] * lp[3:4], axis=1, keepdims=True)) + lam_init)
    on_or_below_diag = (lax.broadcasted_iota(jnp.int32, (tq, tq), 1)
                        <= lax.broadcasted_iota(jnp.int32, (tq, tq), 0))

    def block(k0, width, rows, mask, first):
        n_rows = rows.stop - rows.start
        for st in range(2 * group):
            cols = slice((st // 2) * hw, (st // 2 + 1) * hw)
            kb = k_ref[0, pl.ds(k0, width), cols]
            vb = v_ref[0, pl.ds(k0, width), cols]
            s = lax.dot_general(qs[st][rows, :], kb, NT, preferred_element_type=F32)
            if not exact_scale:
                s = s * scale
            if mask is not None:
                s = jnp.where(mask, s, -jnp.inf)
            m_new = jnp.broadcast_to(jnp.max(s, axis=1, keepdims=True), (n_rows, LANES))
            if not first:
                m_old = m_scr[st, rows, :]
                m_new = jnp.maximum(m_old, m_new)
                alpha = jnp.exp(m_old - m_new)
            e = [jnp.exp(s[:, c:c + LANES] - m_new) for c in range(0, width, LANES)]
            part = functools.reduce(jnp.add, e)
            pv = jnp.dot(jnp.concatenate([x.astype(BF16) for x in e], axis=1), vb, preferred_element_type=F32)
            l_scr[st, rows, :] = part if first else alpha * l_scr[st, rows, :] + part
            acc_scr[st, rows, :] = pv if first else alpha * acc_scr[st, rows, :] + pv
            m_scr[st, rows, :] = m_new

    def below(j, carry):
        block(pl.multiple_of(j * tq, tq), tq, slice(0, tq), None, False)
        return carry

    half = tq // 2
    d0 = pl.multiple_of(i * tq, tq)
    block(d0, half, slice(0, tq), on_or_below_diag[:, :half], True)
    block(d0 + half, half, slice(half, tq), on_or_below_diag[half:, half:], False)
    lax.fori_loop(0, i, below, 0)
    for u in range(group):
        o1 = acc_scr[2 * u] / jnp.sum(l_scr[2 * u], axis=1, keepdims=True)
        o2 = acc_scr[2 * u + 1] / jnp.sum(l_scr[2 * u + 1], axis=1, keepdims=True)
        o = o1 - lam * o2
        y = o * lax.rsqrt(jnp.mean(o * o, axis=-1, keepdims=True) + EPS) * ng_ref[...] * (1.0 - lam_init)
        o_ref[0, :, u * hw:(u + 1) * hw] = y.astype(o_ref.dtype)


def _diffattn(pd, lam_p, norm_g, n_heads, dh, lam_init, tq=512, group=4):
    bsz, s, _ = pd.shape
    tq = min(tq, s)
    hw = 2 * dh
    gw = group * hw
    n_groups = n_heads // group
    kern = functools.partial(_diffattn_kernel, dh=dh, lam_init=lam_init, group=group)
    return pl.pallas_call(
        kern,
        grid=(bsz, n_groups, s // tq),
        in_specs=[pl.BlockSpec((1, tq, gw), lambda b, g, i: (b, i, g)),
                  pl.BlockSpec((1, s, gw), lambda b, g, i: (b, 0, n_groups + g)),
                  pl.BlockSpec((1, s, gw), lambda b, g, i: (b, 0, 2 * n_groups + g)),
                  pl.BlockSpec((4, dh), lambda b, g, i: (0, 0)),
                  pl.BlockSpec((1, hw), lambda b, g, i: (0, 0))],
        out_specs=pl.BlockSpec((1, tq, gw), lambda b, g, i: (b, i, g)),
        out_shape=jax.ShapeDtypeStruct((bsz, s, n_heads * hw), BF16),
        scratch_shapes=[pltpu.VMEM((2 * group, tq, LANES), F32), pltpu.VMEM((2 * group, tq, LANES), F32),
                        pltpu.VMEM((2 * group, tq, hw), F32)],
        compiler_params=_params("arbitrary", "arbitrary", "arbitrary"),
        name="diffattn",
    )(pd, pd, pd, lam_p, norm_g)


def _outproj_kernel(hm_ref, od_ref, x_ref, g1_ref, sh_ref, sc_ref, ng_ref, wm_ref, wd_ref, wq_ref,
                    x1_ref, h2_ref, q_ref):
    y = (jnp.dot(hm_ref[0], wm_ref[...], preferred_element_type=F32)
         + jnp.dot(od_ref[0], wd_ref[...], preferred_element_type=F32))
    x1 = x_ref[0] + g1_ref[0] * y
    x1_ref[0] = x1
    ms = jnp.mean(x1 * x1, axis=-1, keepdims=True)
    h2 = x1 * lax.rsqrt(ms + EPS) * ng_ref[...]
    h2 = (h2 * (1.0 + sc_ref[0]) + sh_ref[0]).astype(BF16)
    h2_ref[0] = h2
    q_ref[0] = jnp.dot(h2, wq_ref[...], preferred_element_type=F32).astype(BF16)


def _outproj(hm, od, x, g1, sh2, sc2, ng, wm, wd, wq, tm=512):
    bsz, s, d = x.shape
    tm = min(tm, s)
    wm_w, wd_w, nq = hm.shape[2], od.shape[2], wq.shape[1]
    tok = lambda b, i: (b, i, 0)
    per_b = lambda b, i: (b, 0, 0)
    const = lambda b, i: (0, 0)
    return pl.pallas_call(
        _outproj_kernel,
        grid=(bsz, s // tm),
        in_specs=[pl.BlockSpec((1, tm, wm_w), tok),
                  pl.BlockSpec((1, tm, wd_w), tok),
                  pl.BlockSpec((1, tm, d), tok),
                  pl.BlockSpec((1, 1, d), per_b),
                  pl.BlockSpec((1, 1, d), per_b),
                  pl.BlockSpec((1, 1, d), per_b),
                  pl.BlockSpec((1, d), const),
                  pl.BlockSpec((wm_w, d), const),
                  pl.BlockSpec((wd_w, d), const),
                  pl.BlockSpec((d, nq), const)],
        out_specs=[pl.BlockSpec((1, tm, d), tok),
                   pl.BlockSpec((1, tm, d), tok),
                   pl.BlockSpec((1, tm, nq), tok)],
        out_shape=[jax.ShapeDtypeStruct((bsz, s, d), F32),
                   jax.ShapeDtypeStruct((bsz, s, d), BF16),
                   jax.ShapeDtypeStruct((bsz, s, nq), BF16)],
        compiler_params=_params("arbitrary", "arbitrary"),
        name="outproj",
    )(hm, od, x, g1, sh2, sc2, ng, wm, wd, wq)


def _oddeven_merge_sort(n):
    def merge(lo, hi, r):
        step = 2 * r
        if step < hi - lo:
            yield from merge(lo, hi, step)
            yield from merge(lo + r, hi, step)
            yield from ((i, i + r) for i in range(lo + r, hi - r, step))
        else:
            yield (lo, lo + r)

    def sort(lo, hi):
        if hi > lo:
            mid = lo + (hi - lo) // 2
            yield from sort(lo, mid)
            yield from sort(mid + 1, hi)
            yield from merge(lo, hi, 1)

    return list(sort(0, n - 1))


def _topk_rows_sorted(s, k, payload=None):
    n_rows, t = s.shape
    nv = n_rows // SUBLANES
    assert nv & (nv - 1) == 0
    slabs = lambda a: [a[SUBLANES * v:SUBLANES * (v + 1), :] for v in range(nv)]
    sub = lax.broadcasted_iota(jnp.int32, (SUBLANES, t), 0)
    out_row = lax.broadcasted_iota(jnp.int32, (k, t), 0)
    val = slabs(s)
    pay = None if payload is None else slabs(payload)
    sid = list(range(nv))
    for i, j in _oddeven_merge_sort(nv):
        va, vb, ia, ib = val[i], val[j], sid[i], sid[j]
        if isinstance(ia, int) and isinstance(ib, int):
            first = va >= vb if ia < ib else va > vb
        else:
            first = (va > vb) | ((va == vb) & (ia < ib))
        val[i], val[j] = jnp.where(first, va, vb), jnp.where(first, vb, va)
        sid[i], sid[j] = jnp.where(first, ia, ib), jnp.where(first, ib, ia)
        if pay is not None:
            pay[i], pay[j] = jnp.where(first, pay[i], pay[j]), jnp.where(first, pay[j], pay[i])
    row = [x * SUBLANES + sub for x in sid]
    vals = jnp.zeros((k, t), s.dtype)
    rows = jnp.zeros((k, t), jnp.int32)
    pays = None if pay is None else jnp.zeros((k, t), payload.dtype)
    for r in range(k):
        m = jnp.max(val[0], axis=0, keepdims=True)
        idx = jnp.min(jnp.where(val[0] == m, row[0], n_rows), axis=0, keepdims=True)
        hit = row[0] == idx
        vals = jnp.where(out_row == r, m, vals)
        rows = jnp.where(out_row == r, idx, rows)
        if pay is not None:
            pays = jnp.where(out_row == r, jnp.sum(jnp.where(hit, pay[0], 0), axis=0, keepdims=True), pays)
        live = min(nv, k - 1 - r)
        for q in range(min(live, nv - 1)):
            val[q] = jnp.where(hit, val[q + 1], val[q])
            row[q] = jnp.where(hit, row[q + 1], row[q])
            if pay is not None:
                pay[q] = jnp.where(hit, pay[q + 1], pay[q])
        if live == nv:
            val[nv - 1] = jnp.where(hit, -jnp.inf, val[nv - 1])
    return vals, rows, pays


def _candidate_layout(k):
    slabs, cur = [], []

    def flush():
        n_valid = len(cur)
        while len(cur) < SUBLANES:
            r = len(cur)
            period = 1
            while period < max(b for _, b in cur[:n_valid]) + 1:
                period *= 2
            b = cur[r - period][1] if r >= period else r
            cur.append((cur[n_valid - 1][0], b))
        slabs.append(([a for a, _ in cur], [b for _, b in cur], n_valid))
        cur.clear()

    for a in range(k):
        nb = k // (a + 1)
        for b0 in range(0, nb, SUBLANES):
            group = [(a, b) for b in range(b0, min(nb, b0 + SUBLANES))]
            if cur and (len(cur) + len(group) > SUBLANES or (nb == 1 and a % SUBLANES == 0)):
                flush()
            cur.extend(group)
    if cur:
        flush()
    return slabs


def _rows_by_pattern(x, pattern, sub):
    p0 = pattern[0]
    if p0 % SUBLANES == 0 and pattern == list(range(p0, p0 + SUBLANES)):
        return x[p0:p0 + SUBLANES, :]
    period = SUBLANES
    while period > 1 and all(pattern[r] == pattern[r % (period // 2)] for r in range(SUBLANES)):
        period //= 2
    pos = sub if period == SUBLANES else sub & (period - 1)
    out = x[p0:p0 + 1, :]
    for r in range(1, period):
        if pattern[r] != pattern[r - 1]:
            out = jnp.where(pos >= r, x[pattern[r]:pattern[r] + 1, :], out)
    return jnp.broadcast_to(out, (SUBLANES, x.shape[1]))


def _route_kernel(q_ref, keys_ref, gm_ref, sv_scr, si_scr, it_scr, jt_scr, gt_scr, il_scr, jl_scr, gl_scr,
                  w_scr, *, n_heads, n_keys, stride):
    K = P_TOPK
    t = q_ref.shape[0]
    slots = n_heads * K
    step = pl.program_id(0)
    cur = step % 2
    prev = 1 - cur
    sub = lax.broadcasted_iota(jnp.int32, (SUBLANES, t), 0)
    key_row = lax.broadcasted_iota(jnp.int32, (n_keys, slots), 0)
    layout = _candidate_layout(K)
    ways2 = 4
    n_loop = n_heads // ways2
    blk_per_iter = n_keys // n_loop

    @pl.when(step == 0)
    def _():
        il_scr[1] = jnp.zeros((t, slots), jnp.int32)
        jl_scr[1] = jnp.zeros((t, slots), jnp.int32)
        gl_scr[1] = jnp.zeros((t, slots), F32)

    def gate_rows(tok):
        i_row = il_scr[prev, pl.ds(tok, 1), :]
        j_row = jl_scr[prev, pl.ds(tok, 1), :]
        g_row = gl_scr[prev, pl.ds(tok, 1), :]
        a_t = jnp.where(key_row == i_row, 1.0, 0.0).astype(BF16)
        b_t = jnp.where(key_row == j_row, g_row, 0.0).astype(BF16)
        m_t = lax.dot_general(a_t, b_t, NT, preferred_element_type=F32)
        w_scr[pl.ds(tok, n_keys, stride=stride), :] = m_t

    ways = 4
    n_loop1 = 2 * n_heads // ways
    tok_per_iter = t // n_loop1

    def stage1(hh, _):
        for p in range(ways):
            hp = ways * hh + p
            c0 = pl.multiple_of(hp * LANES, LANES)
            qb = q_ref[:, pl.ds(c0, LANES)]
            st = lax.dot_general(keys_ref[hp], qb, NT, preferred_element_type=F32)
            v, ix, _ = _topk_rows_sorted(st, K)
            sv_scr[hp] = v
            si_scr[hp] = ix
        t0 = pl.multiple_of(hh * tok_per_iter, tok_per_iter)
        for u in range(tok_per_iter):
            gate_rows(t0 + u)
        return 0

    lax.fori_loop(0, n_loop1, stage1, 0, unroll=True)

    def stage2(hh, _):
        b0 = hh * blk_per_iter
        for u in range(blk_per_iter):
            r0 = pl.multiple_of((b0 + u) * stride, SUBLANES)
            gm_ref[b0 + u] = w_scr[pl.ds(r0, t), :].astype(gm_ref.dtype)
        for u in range(ways2):
            h = ways2 * hh + u
            sv0, sv1 = sv_scr[2 * h], sv_scr[2 * h + 1]
            si0, si1 = si_scr[2 * h], si_scr[2 * h + 1]
            cand, cidx = [], []
            for a_pat, b_pat, n_valid in layout:
                c = _rows_by_pattern(sv0, a_pat, sub) + _rows_by_pattern(sv1, b_pat, sub)
                if n_valid < SUBLANES:
                    c = jnp.where(sub < n_valid, c, -jnp.inf)
                cand.append(c)
                cidx.append(_rows_by_pattern(si0, a_pat, sub) * n_keys + _rows_by_pattern(si1, b_pat, sub))
            while len(cand) & (len(cand) - 1):
                cand.append(jnp.full((SUBLANES, t), -jnp.inf, F32))
                cidx.append(jnp.zeros((SUBLANES, t), jnp.int32))
            fv, _, eidx = _topk_rows_sorted(jnp.concatenate(cand, axis=0), K, jnp.concatenate(cidx, axis=0))
            e = jnp.exp(fv - fv[0:1, :])
            g = e / jnp.sum(e, axis=0, keepdims=True)
            r0 = pl.multiple_of(h * K, K)
            it_scr[pl.ds(r0, K), :] = eidx // n_keys
            jt_scr[pl.ds(r0, K), :] = eidx % n_keys
            gt_scr[pl.ds(r0, K), :] = g
        return 0

    lax.fori_loop(0, n_loop, stage2, 0, unroll=True)
    il_scr[cur] = it_scr[...].T
    jl_scr[cur] = jt_scr[...].T
    gl_scr[cur] = gt_scr[...].T


def _route(q, keys, n_heads, n_keys):
    n = q.shape[0]
    t = LANES
    n_tiles = n // t
    slots = n_heads * P_TOPK
    stride = t + SUBLANES
    kern = functools.partial(_route_kernel, n_heads=n_heads, n_keys=n_keys, stride=stride)
    return pl.pallas_call(
        kern,
        grid=(n_tiles + 1,),
        in_specs=[pl.BlockSpec((t, q.shape[1]), lambda s: (jnp.minimum(s, n_tiles - 1), 0)),
                  pl.BlockSpec(keys.shape, lambda s: (0, 0, 0))],
        out_specs=pl.BlockSpec((n_keys, t, n_keys), lambda s: (0, jnp.maximum(s - 1, 0), 0)),
        out_shape=jax.ShapeDtypeStruct((n_keys, n, n_keys), BF16),
        scratch_shapes=[pltpu.VMEM((2 * n_heads, P_TOPK, t), F32),
                        pltpu.VMEM((2 * n_heads, P_TOPK, t), jnp.int32),
                        pltpu.VMEM((slots, t), jnp.int32),
                        pltpu.VMEM((slots, t), jnp.int32),
                        pltpu.VMEM((slots, t), F32),
                        pltpu.VMEM((2, t, slots), jnp.int32),
                        pltpu.VMEM((2, t, slots), jnp.int32),
                        pltpu.VMEM((2, t, slots), F32),
                        pltpu.VMEM((n_keys * stride, n_keys), F32)],
        compiler_params=_params("arbitrary"),
        name="route",
    )(q, keys)


def _experts_kernel(h2_ref, u_ref, v_ref, m_ref, x1_ref, g2_ref, fg_ref, o_ref, acc_ref, *, final):
    j = pl.program_id(2)

    @pl.when(j == 0)
    def _():
        acc_ref[...] = jnp.zeros_like(acc_ref)

    s = lax.dot_general(h2_ref[0], u_ref[...].astype(BF16), NT, preferred_element_type=F32)
    act = 0.5 * s * (1.0 + lax.erf(s * (2.0 ** -0.5)))
    gate = jnp.concatenate([m_ref[ib] for ib in range(m_ref.shape[0])], axis=1)
    w = (gate.astype(F32) * act).astype(BF16)
    acc_ref[...] += jnp.dot(w, v_ref[...].astype(BF16), preferred_element_type=F32)

    @pl.when(j == pl.num_programs(2) - 1)
    def _():
        x2 = x1_ref[0] + g2_ref[0] * acc_ref[...]
        if final:
            x2 = x2 * lax.rsqrt(jnp.mean(x2 * x2, axis=-1, keepdims=True) + EPS) * fg_ref[...]
        o_ref[0] = x2


def _experts(h2, u, v, gmat, x1, g2, final_g, final, tm=1024, te=1024):
    bsz, s, d = x1.shape
    tm = min(tm, s)
    ne = u.shape[0]
    n_keys = gmat.shape[2]
    n_tiles = s // tm
    kern = functools.partial(_experts_kernel, final=final)
    return pl.pallas_call(
        kern,
        grid=(bsz, n_tiles, ne // te),
        in_specs=[pl.BlockSpec((1, tm, d), lambda b, i, j: (b, i, 0)),
                  pl.BlockSpec((te, d), lambda b, i, j: (j, 0)),
                  pl.BlockSpec((te, d), lambda b, i, j: (j, 0)),
                  pl.BlockSpec((te // n_keys, tm, n_keys), lambda b, i, j: (j, b * n_tiles + i, 0)),
                  pl.BlockSpec((1, tm, d), lambda b, i, j: (b, i, 0)),
                  pl.BlockSpec((1, 1, d), lambda b, i, j: (b, 0, 0)),
                  pl.BlockSpec((1, d), lambda b, i, j: (0, 0))],
        out_specs=pl.BlockSpec((1, tm, d), lambda b, i, j: (b, i, 0)),
        out_shape=jax.ShapeDtypeStruct((bsz, s, d), F32),
        scratch_shapes=[pltpu.VMEM((tm, d), F32)],
        compiler_params=_params("arbitrary", "arbitrary", "arbitrary"),
        name="experts",
    )(h2, u, v, gmat, x1, g2, final_g)


def kernel(x, c, ada_w, ada_b, norm1_g, w_in, conv_w, conv_b, mlstm_gate_b, mlstm_norm_g, lambda_q1,
           lambda_k1, lambda_q2, lambda_k2, diff_norm_g, w_out, norm2_g, peer_w_query, peer_sub_keys,
           peer_u, peer_v, final_g):
    bsz, s, d = x.shape
    depth = ada_w.shape[0]
    m_heads = mlstm_gate_b.shape[1] // 2
    m_width = mlstm_norm_g.shape[1]
    m_hd = m_width // m_heads
    dh = lambda_q1.shape[1]
    d_width = w_out.shape[1] - m_width
    d_heads = d_width // (2 * dh)
    p_heads, _, n_keys, _ = peer_sub_keys.shape[1:]
    n_m = 4 * m_width
    n_g = 2 * m_heads
    assert s % M_CHUNK == 0 and m_hd == LANES and 2 * dh == LANES and n_keys == LANES
    assert peer_sub_keys.shape[-1] == LANES and p_heads % 2 == 0 and n_keys == SUBLANES * P_TOPK

    for l in range(depth):
        mod = _adaln(c, ada_w[l], ada_b[l])
        sh1, sc1, g1, sh2, sc2, g2 = (a.reshape(bsz, 1, d) for a in jnp.split(mod, 6, axis=-1))

        w_m = w_in[l][:, :n_m].astype(BF16)
        w_g = jnp.pad(w_in[l][:, n_m:n_m + n_g].astype(BF16), ((0, 0), (0, LANES - n_g)))
        w_d = w_in[l][:, n_m + n_g:].astype(BF16)
        pm, pg, pd = _inproj(x, sh1, sc1, norm1_g[l].reshape(1, d), w_m, w_g, w_d)

        gates = pg[:, :, :n_g].transpose(0, 2, 1).reshape(bsz, n_g, s // M_CHUNK, M_CHUNK)
        hm = _mlstm(pm, gates, mlstm_gate_b[l], conv_w[l], conv_b[l].reshape(1, -1),
                    mlstm_norm_g[l].reshape(1, -1), m_heads, m_hd)

        lam_init = 0.8 - 0.6 * math.exp(-0.3 * l)
        lam_p = jnp.stack([lambda_q1[l], lambda_k1[l], lambda_q2[l], lambda_k2[l]])
        od = _diffattn(pd, lam_p, diff_norm_g[l].reshape(1, -1), d_heads, dh, lam_init)

        x1, h2, q = _outproj(hm, od, x, g1, sh2, sc2, norm2_g[l].reshape(1, d), w_out[l][:m_width].astype(BF16),
                             w_out[l][m_width:].astype(BF16), peer_w_query[l].astype(BF16))

        keys = peer_sub_keys[l].reshape(2 * p_heads, n_keys, -1).astype(BF16)
        gmat = _route(q.reshape(bsz * s, -1), keys, p_heads, n_keys)
        x = _experts(h2, peer_u[l], peer_v[l], gmat, x1, g2,
                     final_g.reshape(1, d), final=(l == depth - 1))
    return x
```

```python
import functools
import math

import jax
import jax.numpy as jnp
from jax import lax
from jax.experimental import pallas as pl
from jax.experimental.pallas import tpu as pltpu

F32 = jnp.float32
BF16 = jnp.bfloat16
EPS = 1e-6
LANES = 128
SUBLANES = 8
VMEM_LIMIT = 56 * 1024 * 1024
M_CHUNK = 128
CONV_WIDTH = 4
P_TOPK = 16

NT = (((1,), (1,)), ((), ()))
TN = (((0,), (0,)), ((), ()))


def _params(*sem):
    return pltpu.CompilerParams(dimension_semantics=sem, vmem_limit_bytes=VMEM_LIMIT)


def _adaln_kernel(c_ref, w_ref, b_ref, o_ref):
    c = c_ref[...]
    sc = (c * jax.nn.sigmoid(c)).astype(BF16)
    o_ref[...] = jnp.dot(sc, w_ref[...].astype(BF16), preferred_element_type=F32) + b_ref[...]


def _adaln(c, w, b):
    bsz, d = c.shape
    n = w.shape[1]
    tn = 1536
    return pl.pallas_call(
        _adaln_kernel,
        grid=(n // tn,),
        in_specs=[pl.BlockSpec((bsz, d), lambda j: (0, 0)),
                  pl.BlockSpec((d, tn), lambda j: (0, j)),
                  pl.BlockSpec((1, tn), lambda j: (0, j))],
        out_specs=pl.BlockSpec((bsz, tn), lambda j: (0, j)),
        out_shape=jax.ShapeDtypeStruct((bsz, n), F32),
        compiler_params=_params("arbitrary"),
        name="adaln",
    )(c, w, b.reshape(1, n))


def _inproj_kernel(x_ref, sh_ref, sc_ref, g_ref, wm_ref, wg_ref, wd_ref, pm_ref, pg_ref, pd_ref):
    x = x_ref[0]
    ms = jnp.mean(x * x, axis=-1, keepdims=True)
    h = x * lax.rsqrt(ms + EPS) * g_ref[...]
    h = h * (1.0 + sc_ref[0]) + sh_ref[0]
    hb = h.astype(BF16)
    pm_ref[0] = jnp.dot(hb, wm_ref[...], preferred_element_type=F32)
    pg_ref[0] = jnp.dot(hb, wg_ref[...], preferred_element_type=F32)
    pd_ref[0] = jnp.dot(hb, wd_ref[...], preferred_element_type=F32).astype(BF16)


def _inproj(x, sh, sc, g, wm, wg, wd, tm=512):
    bsz, s, d = x.shape
    tm = min(tm, s)
    nm, ng, nd = wm.shape[1], wg.shape[1], wd.shape[1]
    tok = lambda b, i: (b, i, 0)
    per_b = lambda b, i: (b, 0, 0)
    const = lambda b, i: (0, 0)
    return pl.pallas_call(
        _inproj_kernel,
        grid=(bsz, s // tm),
        in_specs=[pl.BlockSpec((1, tm, d), tok),
                  pl.BlockSpec((1, 1, d), per_b),
                  pl.BlockSpec((1, 1, d), per_b),
                  pl.BlockSpec((1, d), const),
                  pl.BlockSpec((d, nm), const),
                  pl.BlockSpec((d, ng), const),
                  pl.BlockSpec((d, nd), const)],
        out_specs=[pl.BlockSpec((1, tm, nm), tok),
                   pl.BlockSpec((1, tm, ng), tok),
                   pl.BlockSpec((1, tm, nd), tok)],
        out_shape=[jax.ShapeDtypeStruct((bsz, s, nm), F32),
                   jax.ShapeDtypeStruct((bsz, s, ng), F32),
                   jax.ShapeDtypeStruct((bsz, s, nd), BF16)],
        compiler_params=_params("arbitrary", "arbitrary"),
        name="inproj",
    )(x, sh, sc, g, wm, wg, wd)


def _causal_conv(u, tail, w, b):
    ext = jnp.concatenate([tail, u], axis=0)
    y = u * w[CONV_WIDTH - 1:CONV_WIDTH, :]
    for k in range(CONV_WIDTH - 1):
        shift = CONV_WIDTH - 1 - k
        y = y + pltpu.roll(ext, shift, axis=0)[SUBLANES:, :] * w[k:k + 1, :]
    return y + b


def _cumsum_lanes(x):
    lane = lax.broadcasted_iota(jnp.int32, x.shape, 1)
    s = 1
    while s < x.shape[1]:
        x = x + jnp.where(lane >= s, pltpu.roll(x, s, axis=1), 0.0)
        s *= 2
    return x


def _mlstm_kernel(gb_ref, mq_ref, mk_ref, mv_ref, mo_ref, g_ref, cwq_ref, cwk_ref, cbq_ref, cbk_ref,
                  ng_ref, o_ref, b_scr, i_scr, c_scr, *, n_heads, head_dim, group):
    h0 = pl.program_id(1) * group
    L = M_CHUNK
    hd = head_dim
    nc = b_scr.shape[1]
    silu = lambda a: a * jax.nn.sigmoid(a)

    for u in range(group):
        i_scr[u] = g_ref[0, h0 + u] + gb_ref[h0 + u]
        fpre = g_ref[0, n_heads + h0 + u] + gb_ref[n_heads + h0 + u]
        lf = jnp.minimum(fpre, 0.0) - jnp.log1p(jnp.exp(-jnp.abs(fpre)))
        b_scr[u] = _cumsum_lanes(lf)

    r_i = lax.broadcasted_iota(jnp.int32, (L, L), 0)
    c_i = lax.broadcasted_iota(jnp.int32, (L, L), 1)
    eye = r_i == c_i
    causal = c_i <= r_i

    full = lambda col: jnp.broadcast_to(col, (L, LANES))

    def to_col(row):
        return full(jnp.sum(jnp.where(eye, row, 0.0), axis=1, keepdims=True))

    def chunk(c, carry):
        t0 = pl.multiple_of(c * L, L)
        return tuple(head_chunk(c, t0, u, carry[u]) for u in range(group))

    def head_chunk(c, t0, u, carry):
        n, m, q_tail, k_tail = carry
        cols = slice(u * hd, (u + 1) * hd)
        q_raw = mq_ref[0, pl.ds(t0, L), cols]
        k_raw = mk_ref[0, pl.ds(t0, L), cols]
        qc = silu(_causal_conv(q_raw, q_tail, cwq_ref[:, cols], cbq_ref[:, cols])).astype(BF16)
        kc = (silu(_causal_conv(k_raw, k_tail, cwk_ref[:, cols], cbk_ref[:, cols])) * (hd ** -0.5)).astype(BF16)
        vc = mv_ref[0, pl.ds(t0, L), cols].astype(BF16)
        b_row = b_scr[u, pl.ds(c, 1), :]
        i_row = i_scr[u, pl.ds(c, 1), :]
        b_col = to_col(b_row)
        i_col = to_col(i_row)

        logD = jnp.where(causal, b_col - b_row + i_row, -jnp.inf)
        m_t = jnp.maximum(b_col + m, full(jnp.max(logD, axis=1, keepdims=True)))
        Dw = jnp.exp(logD - m_t)
        inter = jnp.exp(b_col + m - m_t)
        sqk = lax.dot_general(qc, kc, NT, preferred_element_type=F32) * Dw
        num = (jnp.dot(sqk.astype(BF16), vc, preferred_element_type=F32)
               + inter * jnp.dot(qc, c_scr[u].astype(BF16), preferred_element_type=F32))
        den = (full(jnp.sum(sqk, axis=1, keepdims=True))
               + inter * full(jnp.sum(qc.astype(F32) * n, axis=1, keepdims=True)))
        hh = num / jnp.maximum(jnp.abs(den), jnp.exp(-m_t))

        bL = b_row[:, L - 1:L]
        m_new = jnp.maximum(bL + m, jnp.max(bL - b_row + i_row, axis=1, keepdims=True))
        w_col = jnp.exp(bL - b_col + i_col - m_new)
        decay = jnp.exp(bL + m - m_new)
        kw = kc.astype(F32) * w_col
        c_scr[u] = decay * c_scr[u] + lax.dot_general(kw.astype(BF16), vc, TN, preferred_element_type=F32)
        n_new = decay * n + jnp.sum(kw, axis=0, keepdims=True)

        y = hh * lax.rsqrt(full(jnp.mean(hh * hh, axis=-1, keepdims=True)) + EPS) * ng_ref[:, cols]
        y = y * jax.nn.sigmoid(mo_ref[0, pl.ds(t0, L), cols])
        o_ref[0, pl.ds(t0, L), cols] = y.astype(o_ref.dtype)
        return n_new, m_new, q_raw[L - SUBLANES:, :], k_raw[L - SUBLANES:, :]

    zeros = lambda *shape: jnp.zeros(shape, F32)
    c_scr[...] = zeros(*c_scr.shape)
    init = (zeros(1, hd), zeros(1, 1), zeros(SUBLANES, hd), zeros(SUBLANES, hd))
    lax.fori_loop(0, nc, chunk, (init,) * group)


def _mlstm(pm, gates, gate_b, conv_w, conv_b, norm_g, n_heads, head_dim, group=4):
    bsz, s, _ = pm.shape
    nc = s // M_CHUNK
    gw = group * head_dim
    n_groups = n_heads // group
    col = lambda off: (lambda b, g: (b, 0, off * n_groups + g))
    wcol = lambda off: (lambda b, g: (0, off * n_groups + g))
    kern = functools.partial(_mlstm_kernel, n_heads=n_heads, head_dim=head_dim, group=group)
    return pl.pallas_call(
        kern,
        grid=(bsz, n_groups),
        in_specs=[pl.BlockSpec(memory_space=pltpu.SMEM),
                  pl.BlockSpec((1, s, gw), col(0)),
                  pl.BlockSpec((1, s, gw), col(1)),
                  pl.BlockSpec((1, s, gw), col(2)),
                  pl.BlockSpec((1, s, gw), col(3)),
                  pl.BlockSpec((1, 2 * n_heads, nc, M_CHUNK), lambda b, g: (b, 0, 0, 0)),
                  pl.BlockSpec((CONV_WIDTH, gw), wcol(0)),
                  pl.BlockSpec((CONV_WIDTH, gw), wcol(1)),
                  pl.BlockSpec((1, gw), wcol(0)),
                  pl.BlockSpec((1, gw), wcol(1)),
                  pl.BlockSpec((1, gw), wcol(0))],
        out_specs=pl.BlockSpec((1, s, gw), col(0)),
        out_shape=jax.ShapeDtypeStruct((bsz, s, n_heads * head_dim), BF16),
        scratch_shapes=[pltpu.VMEM((group, nc, M_CHUNK), F32), pltpu.VMEM((group, nc, M_CHUNK), F32),
                        pltpu.VMEM((group, head_dim, head_dim), F32)],
        compiler_params=_params("arbitrary", "arbitrary"),
        name="mlstm",
    )(gate_b, pm, pm, pm, pm, gates, conv_w, conv_w, conv_b, conv_b, norm_g)


def _diffattn_kernel(q_ref, k_ref, v_ref, lam_ref, ng_ref, o_ref, m_scr, l_scr, acc_scr,
                     *, dh, lam_init, group):
    i = pl.program_id(2)
    tq = q_ref.shape[1]
    hw = 2 * dh
    scale = dh ** -0.5
    exact_scale = math.frexp(scale)[0] == 0.5
    lane = lax.broadcasted_iota(jnp.int32, (tq, hw), 1)
    qs = []
    for u in range(group):
        q = q_ref[0, :, u * hw:(u + 1) * hw]
        if exact_scale:
            q = q * scale
        zero = jnp.zeros_like(q)
        qs += [jnp.where(lane < dh, q, zero), jnp.where(lane < dh, zero, q)]
    lp = lam_ref[...]
    lam = (jnp.exp(jnp.sum(lp[0:1] * lp[1:2], axis=1, keepdims=True))
           - jnp.exp(jnp.sum(lp[2:3] * lp[3:4], axis=1, keepdims=True)) + lam_init)
    on_or_below_diag = (lax.broadcasted_iota(jnp.int32, (tq, tq), 1)
                        <= lax.broadcasted_iota(jnp.int32, (tq, tq), 0))

    def block(k0, width, rows, mask, first):
        n_rows = rows.stop - rows.start
        for st in range(2 * group):
            cols = slice((st // 2) * hw, (st // 2 + 1) * hw)
            kb = k_ref[0, pl.ds(k0, width), cols]
            vb = v_ref[0, pl.ds(k0, width), cols]
            s = lax.dot_general(qs[st][rows, :], kb, NT, preferred_element_type=F32)
            if not exact_scale:
                s = s * scale
            if mask is not None:
                s = jnp.where(mask, s, -jnp.inf)
            m_new = jnp.broadcast_to(jnp.max(s, axis=1, keepdims=True), (n_rows, LANES))
            if not first:
                m_old = m_scr[st, rows, :]
                m_new = jnp.maximum(m_old, m_new)
                alpha = jnp.exp(m_old - m_new)
            e = [jnp.exp(s[:, c:c + LANES] - m_new) for c in range(0, width, LANES)]
            part = functools.reduce(jnp.add, e)
            pv = jnp.dot(jnp.concatenate([x.astype(BF16) for x in e], axis=1), vb, preferred_element_type=F32)
            l_scr[st, rows, :] = part if first else alpha * l_scr[st, rows, :] + part
            acc_scr[st, rows, :] = pv if first else alpha * acc_scr[st, rows, :] + pv
            m_scr[st, rows, :] = m_new

    def below(j, carry):
        block(pl.multiple_of(j * tq, tq), tq, slice(0, tq), None, False)
        return carry

    half = tq // 2
    d0 = pl.multiple_of(i * tq, tq)
    block(d0, half, slice(0, tq), on_or_below_diag[:, :half], True)
    block(d0 + half, half, slice(half, tq), on_or_below_diag[half:, half:], False)
    lax.fori_loop(0, i, below, 0)
    for u in range(group):
        o1 = acc_scr[2 * u] / jnp.sum(l_scr[2 * u], axis=1, keepdims=True)
        o2 = acc_scr[2 * u + 1] / jnp.sum(l_scr[2 * u + 1], axis=1, keepdims=True)
        o = o1 - lam * o2
        y = o * lax.rsqrt(jnp.mean(o * o, axis=-1, keepdims=True) + EPS) * ng_ref[...] * (1.0 - lam_init)
        o_ref[0, :, u * hw:(u + 1) * hw] = y.astype(o_ref.dtype)


def _diffattn(pd, lam_p, norm_g, n_heads, dh, lam_init, tq=512, group=4):
    bsz, s, _ = pd.shape
    tq = min(tq, s)
    hw = 2 * dh
    gw = group * hw
    n_groups = n_heads // group
    kern = functools.partial(_diffattn_kernel, dh=dh, lam_init=lam_init, group=group)
    return pl.pallas_call(
        kern,
        grid=(bsz, n_groups, s // tq),
        in_specs=[pl.BlockSpec((1, tq, gw), lambda b, g, i: (b, i, g)),
                  pl.BlockSpec((1, s, gw), lambda b, g, i: (b, 0, n_groups + g)),
                  pl.BlockSpec((1, s, gw), lambda b, g, i: (b, 0, 2 * n_groups + g)),
                  pl.BlockSpec((4, dh), lambda b, g, i: (0, 0)),
                  pl.BlockSpec((1, hw), lambda b, g, i: (0, 0))],
        out_specs=pl.BlockSpec((1, tq, gw), lambda b, g, i: (b, i, g)),
        out_shape=jax.ShapeDtypeStruct((bsz, s, n_heads * hw), BF16),
        scratch_shapes=[pltpu.VMEM((2 * group, tq, LANES), F32), pltpu.VMEM((2 * group, tq, LANES), F32),
                        pltpu.VMEM((2 * group, tq, hw), F32)],
        compiler_params=_params("arbitrary", "arbitrary", "arbitrary"),
        name="diffattn",
    )(pd, pd, pd, lam_p, norm_g)


def _outproj_kernel(hm_ref, od_ref, x_ref, g1_ref, sh_ref, sc_ref, ng_ref, wm_ref, wd_ref, wq_ref,
                    x1_ref, h2_ref, q_ref):
    y = (jnp.dot(hm_ref[0], wm_ref[...], preferred_element_type=F32)
         + jnp.dot(od_ref[0], wd_ref[...], preferred_element_type=F32))
    x1 = x_ref[0] + g1_ref[0] * y
    x1_ref[0] = x1
    ms = jnp.mean(x1 * x1, axis=-1, keepdims=True)
    h2 = x1 * lax.rsqrt(ms + EPS) * ng_ref[...]
    h2 = (h2 * (1.0 + sc_ref[0]) + sh_ref[0]).astype(BF16)
    h2_ref[0] = h2
    q_ref[0] = jnp.dot(h2, wq_ref[...], preferred_element_type=F32).astype(BF16)


def _outproj(hm, od, x, g1, sh2, sc2, ng, wm, wd, wq, tm=512):
    bsz, s, d = x.shape
    tm = min(tm, s)
    wm_w, wd_w, nq = hm.shape[2], od.shape[2], wq.shape[1]
    tok = lambda b, i: (b, i, 0)
    per_b = lambda b, i: (b, 0, 0)
    const = lambda b, i: (0, 0)
    return pl.pallas_call(
        _outproj_kernel,
        grid=(bsz, s // tm),
        in_specs=[pl.BlockSpec((1, tm, wm_w), tok),
                  pl.BlockSpec((1, tm, wd_w), tok),
                  pl.BlockSpec((1, tm, d), tok),
                  pl.BlockSpec((1, 1, d), per_b),
                  pl.BlockSpec((1, 1, d), per_b),
                  pl.BlockSpec((1, 1, d), per_b),
                  pl.BlockSpec((1, d), const),
                  pl.BlockSpec((wm_w, d), const),
                  pl.BlockSpec((wd_w, d), const),
                  pl.BlockSpec((d, nq), const)],
        out_specs=[pl.BlockSpec((1, tm, d), tok),
                   pl.BlockSpec((1, tm, d), tok),
                   pl.BlockSpec((1, tm, nq), tok)],
        out_shape=[jax.ShapeDtypeStruct((bsz, s, d), F32),
                   jax.ShapeDtypeStruct((bsz, s, d), BF16),
                   jax.ShapeDtypeStruct((bsz, s, nq), BF16)],
        compiler_params=_params("arbitrary", "arbitrary"),
        name="outproj",
    )(hm, od, x, g1, sh2, sc2, ng, wm, wd, wq)


def _oddeven_merge_sort(n):
    def merge(lo, hi, r):
        step = 2 * r
        if step < hi - lo:
            yield from merge(lo, hi, step)
            yield from merge(lo + r, hi, step)
            yield from ((i, i + r) for i in range(lo + r, hi - r, step))
        else:
            yield (lo, lo + r)

    def sort(lo, hi):
        if hi > lo:
            mid = lo + (hi - lo) // 2
            yield from sort(lo, mid)
            yield from sort(mid + 1, hi)
            yield from merge(lo, hi, 1)

    return list(sort(0, n - 1))


def _topk_rows_sorted(s, k, payload=None):
    n_rows, t = s.shape
    nv = n_rows // SUBLANES
    assert nv & (nv - 1) == 0
    slabs = lambda a: [a[SUBLANES * v:SUBLANES * (v + 1), :] for v in range(nv)]
    sub = lax.broadcasted_iota(jnp.int32, (SUBLANES, t), 0)
    out_row = lax.broadcasted_iota(jnp.int32, (k, t), 0)
    val = slabs(s)
    pay = None if payload is None else slabs(payload)
    sid = list(range(nv))
    for i, j in _oddeven_merge_sort(nv):
        va, vb, ia, ib = val[i], val[j], sid[i], sid[j]
        if isinstance(ia, int) and isinstance(ib, int):
            first = va >= vb if ia < ib else va > vb
        else:
            first = (va > vb) | ((va == vb) & (ia < ib))
        val[i], val[j] = jnp.maximum(va, vb), jnp.minimum(va, vb)
        sid[i], sid[j] = jnp.where(first, ia, ib), jnp.where(first, ib, ia)
        if pay is not None:
            pay[i], pay[j] = jnp.where(first, pay[i], pay[j]), jnp.where(first, pay[j], pay[i])
    row = [x * SUBLANES + sub for x in sid]
    vals = jnp.zeros((k, t), s.dtype)
    rows = jnp.zeros((k, t), jnp.int32)
    pays = None if pay is None else jnp.zeros((k, t), payload.dtype)
    for r in range(k):
        m = jnp.max(val[0], axis=0, keepdims=True)
        idx = jnp.min(jnp.where(val[0] == m, row[0], n_rows), axis=0, keepdims=True)
        hit = row[0] == idx
        vals = jnp.where(out_row == r, m, vals)
        rows = jnp.where(out_row == r, idx, rows)
        if pay is not None:
            pays = jnp.where(out_row == r, jnp.sum(jnp.where(hit, pay[0], 0), axis=0, keepdims=True), pays)
        live = min(nv, k - 1 - r)
        for q in range(min(live, nv - 1)):
            val[q] = jnp.where(hit, val[q + 1], val[q])
            row[q] = jnp.where(hit, row[q + 1], row[q])
            if pay is not None:
                pay[q] = jnp.where(hit, pay[q + 1], pay[q])
        if live == nv:
            val[nv - 1] = jnp.where(hit, -jnp.inf, val[nv - 1])
    return vals, rows, pays


def _candidate_layout(k):
    slabs, cur = [], []

    def flush():
        n_valid = len(cur)
        while len(cur) < SUBLANES:
            r = len(cur)
            period = 1
            while period < max(b for _, b in cur[:n_valid]) + 1:
                period *= 2
            b = cur[r - period][1] if r >= period else r
            cur.append((cur[n_valid - 1][0], b))
        slabs.append(([a for a, _ in cur], [b for _, b in cur], n_valid))
        cur.clear()

    for a in range(k):
        nb = k // (a + 1)
        for b0 in range(0, nb, SUBLANES):
            group = [(a, b) for b in range(b0, min(nb, b0 + SUBLANES))]
            if cur and (len(cur) + len(group) > SUBLANES or (nb == 1 and a % SUBLANES == 0)):
                flush()
            cur.extend(group)
    if cur:
        flush()
    return slabs


def _rows_by_pattern(x, pattern, sub):
    p0 = pattern[0]
    if p0 % SUBLANES == 0 and pattern == list(range(p0, p0 + SUBLANES)):
        return x[p0:p0 + SUBLANES, :]
    period = SUBLANES
    while period > 1 and all(pattern[r] == pattern[r % (period // 2)] for r in range(SUBLANES)):
        period //= 2
    pos = sub if period == SUBLANES else sub & (period - 1)
    out = x[p0:p0 + 1, :]
    for r in range(1, period):
        if pattern[r] != pattern[r - 1]:
            out = jnp.where(pos >= r, x[pattern[r]:pattern[r] + 1, :], out)
    return jnp.broadcast_to(out, (SUBLANES, x.shape[1]))


def _route_kernel(q_ref, keys_ref, gm_ref, sv_scr, si_scr, it_scr, jt_scr, gt_scr, il_scr, jl_scr, gl_scr,
                  w_scr, *, n_heads, n_keys, stride):
    K = P_TOPK
    t = q_ref.shape[0]
    slots = n_heads * K
    step = pl.program_id(0)
    cur = step % 2
    prev = 1 - cur
    sub = lax.broadcasted_iota(jnp.int32, (SUBLANES, t), 0)
    key_row = lax.broadcasted_iota(jnp.int32, (n_keys, slots), 0)
    layout = _candidate_layout(K)
    ways2 = 4
    n_loop = n_heads // ways2
    blk_per_iter = n_keys // n_loop

    @pl.when(step == 0)
    def _():
        il_scr[1] = jnp.zeros((t, slots), jnp.int32)
        jl_scr[1] = jnp.zeros((t, slots), jnp.int32)
        gl_scr[1] = jnp.zeros((t, slots), F32)

    def gate_rows(tok):
        i_row = il_scr[prev, pl.ds(tok, 1), :]
        j_row = jl_scr[prev, pl.ds(tok, 1), :]
        g_row = gl_scr[prev, pl.ds(tok, 1), :]
        a_t = jnp.where(key_row == i_row, 1.0, 0.0).astype(BF16)
        b_t = jnp.where(key_row == j_row, g_row, 0.0).astype(BF16)
        m_t = lax.dot_general(a_t, b_t, NT, preferred_element_type=F32)
        w_scr[pl.ds(tok, n_keys, stride=stride), :] = m_t

    ways = 4
    n_loop1 = 2 * n_heads // ways
    tok_per_iter = t // n_loop1

    def stage1(hh, _):
        for p in range(ways):
            hp = ways * hh + p
            c0 = pl.multiple_of(hp * LANES, LANES)
            qb = q_ref[:, pl.ds(c0, LANES)]
            st = lax.dot_general(keys_ref[hp], qb, NT, preferred_element_type=F32)
            v, ix, _ = _topk_rows_sorted(st, K)
            sv_scr[hp] = v
            si_scr[hp] = ix
        t0 = pl.multiple_of(hh * tok_per_iter, tok_per_iter)
        for u in range(tok_per_iter):
            gate_rows(t0 + u)
        return 0

    lax.fori_loop(0, n_loop1, stage1, 0, unroll=True)

    def stage2(hh, _):
        b0 = hh * blk_per_iter
        for u in range(blk_per_iter):
            r0 = pl.multiple_of((b0 + u) * stride, SUBLANES)
            gm_ref[b0 + u] = w_scr[pl.ds(r0, t), :].astype(gm_ref.dtype)
        for u in range(ways2):
            h = ways2 * hh + u
            sv0, sv1 = sv_scr[2 * h], sv_scr[2 * h + 1]
            si0, si1 = si_scr[2 * h], si_scr[2 * h + 1]
            cand, cidx = [], []
            for a_pat, b_pat, n_valid in layout:
                c = _rows_by_pattern(sv0, a_pat, sub) + _rows_by_pattern(sv1, b_pat, sub)
                if n_valid < SUBLANES:
                    c = jnp.where(sub < n_valid, c, -jnp.inf)
                cand.append(c)
                cidx.append(_rows_by_pattern(si0, a_pat, sub) * n_keys + _rows_by_pattern(si1, b_pat, sub))
            while len(cand) & (len(cand) - 1):
                cand.append(jnp.full((SUBLANES, t), -jnp.inf, F32))
                cidx.append(jnp.zeros((SUBLANES, t), jnp.int32))
            fv, _, eidx = _topk_rows_sorted(jnp.concatenate(cand, axis=0), K, jnp.concatenate(cidx, axis=0))
            e = jnp.exp(fv - fv[0:1, :])
            g = e / jnp.sum(e, axis=0, keepdims=True)
            r0 = pl.multiple_of(h * K, K)
            it_scr[pl.ds(r0, K), :] = eidx // n_keys
            jt_scr[pl.ds(r0, K), :] = eidx % n_keys
            gt_scr[pl.ds(r0, K), :] = g
        return 0

    lax.fori_loop(0, n_loop, stage2, 0, unroll=True)
    il_scr[cur] = it_scr[...].T
    jl_scr[cur] = jt_scr[...].T
    gl_scr[cur] = gt_scr[...].T


def _route(q, keys, n_heads, n_keys):
    n = q.shape[0]
    t = LANES
    n_tiles = n // t
    slots = n_heads * P_TOPK
    stride = t + SUBLANES
    kern = functools.partial(_route_kernel, n_heads=n_heads, n_keys=n_keys, stride=stride)
    return pl.pallas_call(
        kern,
        grid=(n_tiles + 1,),
        in_specs=[pl.BlockSpec((t, q.shape[1]), lambda s: (jnp.minimum(s, n_tiles - 1), 0)),
                  pl.BlockSpec(keys.shape, lambda s: (0, 0, 0))],
        out_specs=pl.BlockSpec((n_keys, t, n_keys), lambda s: (0, jnp.maximum(s - 1, 0), 0)),
        out_shape=jax.ShapeDtypeStruct((n_keys, n, n_keys), BF16),
        scratch_shapes=[pltpu.VMEM((2 * n_heads, P_TOPK, t), F32),
                        pltpu.VMEM((2 * n_heads, P_TOPK, t), jnp.int32),
                        pltpu.VMEM((slots, t), jnp.int32),
                        pltpu.VMEM((slots, t), jnp.int32),
                        pltpu.VMEM((slots, t), F32),
                        pltpu.VMEM((2, t, slots), jnp.int32),
                        pltpu.VMEM((2, t, slots), jnp.int32),
                        pltpu.VMEM((2, t, slots), F32),
                        pltpu.VMEM((n_keys * stride, n_keys), F32)],
        compiler_params=_params("arbitrary"),
        name="route",
    )(q, keys)


def _experts_kernel(h2_ref, u_ref, v_ref, m_ref, x1_ref, g2_ref, fg_ref, o_ref, acc_ref, *, final):
    j = pl.program_id(2)

    @pl.when(j == 0)
    def _():
        acc_ref[...] = jnp.zeros_like(acc_ref)

    s = lax.dot_general(h2_ref[0], u_ref[...].astype(BF16), NT, preferred_element_type=F32)
    act = 0.5 * s * (1.0 + lax.erf(s * (2.0 ** -0.5)))
    gate = jnp.concatenate([m_ref[ib] for ib in range(m_ref.shape[0])], axis=1)
    w = (gate.astype(F32) * act).astype(BF16)
    acc_ref[...] += jnp.dot(w, v_ref[...].astype(BF16), preferred_element_type=F32)

    @pl.when(j == pl.num_programs(2) - 1)
    def _():
        x2 = x1_ref[0] + g2_ref[0] * acc_ref[...]
        if final:
            x2 = x2 * lax.rsqrt(jnp.mean(x2 * x2, axis=-1, keepdims=True) + EPS) * fg_ref[...]
        o_ref[0] = x2


def _experts(h2, u, v, gmat, x1, g2, final_g, final, tm=1024, te=1024):
    bsz, s, d = x1.shape
    tm = min(tm, s)
    ne = u.shape[0]
    n_keys = gmat.shape[2]
    n_tiles = s // tm
    kern = functools.partial(_experts_kernel, final=final)
    return pl.pallas_call(
        kern,
        grid=(bsz, n_tiles, ne // te),
        in_specs=[pl.BlockSpec((1, tm, d), lambda b, i, j: (b, i, 0)),
                  pl.BlockSpec((te, d), lambda b, i, j: (j, 0)),
                  pl.BlockSpec((te, d), lambda b, i, j: (j, 0)),
                  pl.BlockSpec((te // n_keys, tm, n_keys), lambda b, i, j: (j, b * n_tiles + i, 0)),
                  pl.BlockSpec((1, tm, d), lambda b, i, j: (b, i, 0)),
                  pl.BlockSpec((1, 1, d), lambda b, i, j: (b, 0, 0)),
                  pl.BlockSpec((1, d), lambda b, i, j: (0, 0))],
        out_specs=pl.BlockSpec((1, tm, d), lambda b, i, j: (b, i, 0)),
        out_shape=jax.ShapeDtypeStruct((bsz, s, d), F32),
        scratch_shapes=[pltpu.VMEM((tm, d), F32)],
        compiler_params=_params("arbitrary", "arbitrary", "arbitrary"),
        name="experts",
    )(h2, u, v, gmat, x1, g2, final_g)


def kernel(x, c, ada_w, ada_b, norm1_g, w_in, conv_w, conv_b, mlstm_gate_b, mlstm_norm_g, lambda_q1,
           lambda_k1, lambda_q2, lambda_k2, diff_norm_g, w_out, norm2_g, peer_w_query, peer_sub_keys,
           peer_u, peer_v, final_g):
    bsz, s, d = x.shape
    depth = ada_w.shape[0]
    m_heads = mlstm_gate_b.shape[1] // 2
    m_width = mlstm_norm_g.shape[1]
    m_hd = m_width // m_heads
    dh = lambda_q1.shape[1]
    d_width = w_out.shape[1] - m_width
    d_heads = d_width // (2 * dh)
    p_heads, _, n_keys, _ = peer_sub_keys.shape[1:]
    n_m = 4 * m_width
    n_g = 2 * m_heads
    assert s % M_CHUNK == 0 and m_hd == LANES and 2 * dh == LANES and n_keys == LANES
    assert peer_sub_keys.shape[-1] == LANES and p_heads % 2 == 0 and n_keys == SUBLANES * P_TOPK

    for l in range(depth):
        mod = _adaln(c, ada_w[l], ada_b[l])
        sh1, sc1, g1, sh2, sc2, g2 = (a.reshape(bsz, 1, d) for a in jnp.split(mod, 6, axis=-1))

        w_m = w_in[l][:, :n_m].astype(BF16)
        w_g = jnp.pad(w_in[l][:, n_m:n_m + n_g].astype(BF16), ((0, 0), (0, LANES - n_g)))
        w_d = w_in[l][:, n_m + n_g:].astype(BF16)
        pm, pg, pd = _inproj(x, sh1, sc1, norm1_g[l].reshape(1, d), w_m, w_g, w_d)

        gates = pg[:, :, :n_g].transpose(0, 2, 1).reshape(bsz, n_g, s // M_CHUNK, M_CHUNK)
        hm = _mlstm(pm, gates, mlstm_gate_b[l], conv_w[l], conv_b[l].reshape(1, -1),
                    mlstm_norm_g[l].reshape(1, -1), m_heads, m_hd)

        lam_init = 0.8 - 0.6 * math.exp(-0.3 * l)
        lam_p = jnp.stack([lambda_q1[l], lambda_k1[l], lambda_q2[l], lambda_k2[l]])
        od = _diffattn(pd, lam_p, diff_norm_g[l].reshape(1, -1), d_heads, dh, lam_init)

        x1, h2, q = _outproj(hm, od, x, g1, sh2, sc2, norm2_g[l].reshape(1, d), w_out[l][:m_width].astype(BF16),
                             w_out[l][m_width:].astype(BF16), peer_w_query[l].astype(BF16))

        keys = peer_sub_keys[l].reshape(2 * p_heads, n_keys, -1).astype(BF16)
        gmat = _route(q.reshape(bsz * s, -1), keys, p_heads, n_keys)
        x = _experts(h2, peer_u[l], peer_v[l], gmat, x1, g2,
                     final_g.reshape(1, d), final=(l == depth - 1))
    return x
```

```python
import functools
import math

import jax
import jax.numpy as jnp
from jax import lax
from jax.experimental import pallas as pl
from jax.experimental.pallas import tpu as pltpu

F32 = jnp.float32
BF16 = jnp.bfloat16
EPS = 1e-6
LANES = 128
SUBLANES = 8
VMEM_LIMIT = 56 * 1024 * 1024
M_CHUNK = 128
CONV_WIDTH = 4
P_TOPK = 16

NT = (((1,), (1,)), ((), ()))
TN = (((0,), (0,)), ((), ()))


def _params(*sem):
    return pltpu.CompilerParams(dimension_semantics=sem, vmem_limit_bytes=VMEM_LIMIT)


def _adaln_kernel(c_ref, w_ref, b_ref, o_ref):
    c = c_ref[...]
    sc = (c * jax.nn.sigmoid(c)).astype(BF16)
    o_ref[...] = jnp.dot(sc, w_ref[...].astype(BF16), preferred_element_type=F32) + b_ref[...]


def _adaln(c, w, b):
    bsz, d = c.shape
    n = w.shape[1]
    tn = 1536
    return pl.pallas_call(
        _adaln_kernel,
        grid=(n // tn,),
        in_specs=[pl.BlockSpec((bsz, d), lambda j: (0, 0)),
                  pl.BlockSpec((d, tn), lambda j: (0, j)),
                  pl.BlockSpec((1, tn), lambda j: (0, j))],
        out_specs=pl.BlockSpec((bsz, tn), lambda j: (0, j)),
        out_shape=jax.ShapeDtypeStruct((bsz, n), F32),
        compiler_params=_params("arbitrary"),
        name="adaln",
    )(c, w, b.reshape(1, n))


def _inproj_kernel(x_ref, sh_ref, sc_ref, g_ref, wm_ref, wg_ref, wd_ref, pm_ref, pg_ref, pd_ref):
    x = x_ref[0]
    ms = jnp.mean(x * x, axis=-1, keepdims=True)
    h = x * lax.rsqrt(ms + EPS) * g_ref[...]
    h = h * (1.0 + sc_ref[0]) + sh_ref[0]
    hb = h.astype(BF16)
    pm_ref[0] = jnp.dot(hb, wm_ref[...], preferred_element_type=F32)
    pg_ref[0] = jnp.dot(hb, wg_ref[...], preferred_element_type=F32)
    pd_ref[0] = jnp.dot(hb, wd_ref[...], preferred_element_type=F32).astype(BF16)


def _inproj(x, sh, sc, g, wm, wg, wd, tm=512):
    bsz, s, d = x.shape
    tm = min(tm, s)
    nm, ng, nd = wm.shape[1], wg.shape[1], wd.shape[1]
    tok = lambda b, i: (b, i, 0)
    per_b = lambda b, i: (b, 0, 0)
    const = lambda b, i: (0, 0)
    return pl.pallas_call(
        _inproj_kernel,
        grid=(bsz, s // tm),
        in_specs=[pl.BlockSpec((1, tm, d), tok),
                  pl.BlockSpec((1, 1, d), per_b),
                  pl.BlockSpec((1, 1, d), per_b),
                  pl.BlockSpec((1, d), const),
                  pl.BlockSpec((d, nm), const),
                  pl.BlockSpec((d, ng), const),
                  pl.BlockSpec((d, nd), const)],
        out_specs=[pl.BlockSpec((1, tm, nm), tok),
                   pl.BlockSpec((1, tm, ng), tok),
                   pl.BlockSpec((1, tm, nd), tok)],
        out_shape=[jax.ShapeDtypeStruct((bsz, s, nm), F32),
                   jax.ShapeDtypeStruct((bsz, s, ng), F32),
                   jax.ShapeDtypeStruct((bsz, s, nd), BF16)],
        compiler_params=_params("arbitrary", "arbitrary"),
        name="inproj",
    )(x, sh, sc, g, wm, wg, wd)


def _causal_conv(u, tail, w, b):
    ext = jnp.concatenate([tail, u], axis=0)
    y = u * w[CONV_WIDTH - 1:CONV_WIDTH, :]
    for k in range(CONV_WIDTH - 1):
        shift = CONV_WIDTH - 1 - k
        y = y + pltpu.roll(ext, shift, axis=0)[SUBLANES:, :] * w[k:k + 1, :]
    return y + b


def _cumsum_lanes(x):
    lane = lax.broadcasted_iota(jnp.int32, x.shape, 1)
    s = 1
    while s < x.shape[1]:
        x = x + jnp.where(lane >= s, pltpu.roll(x, s, axis=1), 0.0)
        s *= 2
    return x


def _mlstm_kernel(gb_ref, mq_ref, mk_ref, mv_ref, mo_ref, g_ref, cwq_ref, cwk_ref, cbq_ref, cbk_ref,
                  ng_ref, o_ref, b_scr, i_scr, c_scr, *, n_heads, head_dim, group):
    h0 = pl.program_id(1) * group
    L = M_CHUNK
    hd = head_dim
    nc = b_scr.shape[1]
    silu = lambda a: a * jax.nn.sigmoid(a)

    for u in range(group):
        i_scr[u] = g_ref[0, h0 + u] + gb_ref[h0 + u]
        fpre = g_ref[0, n_heads + h0 + u] + gb_ref[n_heads + h0 + u]
        lf = jnp.minimum(fpre, 0.0) - jnp.log1p(jnp.exp(-jnp.abs(fpre)))
        b_scr[u] = _cumsum_lanes(lf)

    r_i = lax.broadcasted_iota(jnp.int32, (L, L), 0)
    c_i = lax.broadcasted_iota(jnp.int32, (L, L), 1)
    eye = r_i == c_i
    causal = c_i <= r_i

    full = lambda col: jnp.broadcast_to(col, (L, LANES))

    def to_col(row):
        return full(jnp.sum(jnp.where(eye, row, 0.0), axis=1, keepdims=True))

    def chunk(c, carry):
        t0 = pl.multiple_of(c * L, L)
        return tuple(head_chunk(c, t0, u, carry[u]) for u in range(group))

    def head_chunk(c, t0, u, carry):
        n, m, q_tail, k_tail = carry
        cols = slice(u * hd, (u + 1) * hd)
        q_raw = mq_ref[0, pl.ds(t0, L), cols]
        k_raw = mk_ref[0, pl.ds(t0, L), cols]
        qc = silu(_causal_conv(q_raw, q_tail, cwq_ref[:, cols], cbq_ref[:, cols])).astype(BF16)
        kc = (silu(_causal_conv(k_raw, k_tail, cwk_ref[:, cols], cbk_ref[:, cols])) * (hd ** -0.5)).astype(BF16)
        vc = mv_ref[0, pl.ds(t0, L), cols].astype(BF16)
        b_row = b_scr[u, pl.ds(c, 1), :]
        i_row = i_scr[u, pl.ds(c, 1), :]
        b_col = to_col(b_row)
        i_col = to_col(i_row)

        logD = jnp.where(causal, b_col - b_row + i_row, -jnp.inf)
        m_t = jnp.maximum(b_col + m, full(jnp.max(logD, axis=1, keepdims=True)))
        Dw = jnp.exp(logD - m_t)
        inter = jnp.exp(b_col + m - m_t)
        sqk = lax.dot_general(qc, kc, NT, preferred_element_type=F32) * Dw
        num = (jnp.dot(sqk.astype(BF16), vc, preferred_element_type=F32)
               + inter * jnp.dot(qc, c_scr[u].astype(BF16), preferred_element_type=F32))
        den = (full(jnp.sum(sqk, axis=1, keepdims=True))
               + inter * full(jnp.sum(qc.astype(F32) * n, axis=1, keepdims=True)))
        hh = num / jnp.maximum(jnp.abs(den), jnp.exp(-m_t))

        bL = b_row[:, L - 1:L]
        m_new = jnp.maximum(bL + m, jnp.max(bL - b_row + i_row, axis=1, keepdims=True))
        w_col = jnp.exp(bL - b_col + i_col - m_new)
        decay = jnp.exp(bL + m - m_new)
        kw = kc.astype(F32) * w_col
        c_scr[u] = decay * c_scr[u] + lax.dot_general(kw.astype(BF16), vc, TN, preferred_element_type=F32)
        n_new = decay * n + jnp.sum(kw, axis=0, keepdims=True)

        y = hh * lax.rsqrt(full(jnp.mean(hh * hh, axis=-1, keepdims=True)) + EPS) * ng_ref[:, cols]
        y = y * jax.nn.sigmoid(mo_ref[0, pl.ds(t0, L), cols])
        o_ref[0, pl.ds(t0, L), cols] = y.astype(o_ref.dtype)
        return n_new, m_new, q_raw[L - SUBLANES:, :], k_raw[L - SUBLANES:, :]

    zeros = lambda *shape: jnp.zeros(shape, F32)
    c_scr[...] = zeros(*c_scr.shape)
    init = (zeros(1, hd), zeros(1, 1), zeros(SUBLANES, hd), zeros(SUBLANES, hd))
    lax.fori_loop(0, nc, chunk, (init,) * group)


def _mlstm(pm, gates, gate_b, conv_w, conv_b, norm_g, n_heads, head_dim, group=4):
    bsz, s, _ = pm.shape
    nc = s // M_CHUNK
    gw = group * head_dim
    n_groups = n_heads // group
    col = lambda off: (lambda b, g: (b, 0, off * n_groups + g))
    wcol = lambda off: (lambda b, g: (0, off * n_groups + g))
    kern = functools.partial(_mlstm_kernel, n_heads=n_heads, head_dim=head_dim, group=group)
    return pl.pallas_call(
        kern,
        grid=(bsz, n_groups),
        in_specs=[pl.BlockSpec(memory_space=pltpu.SMEM),
                  pl.BlockSpec((1, s, gw), col(0)),
                  pl.BlockSpec((1, s, gw), col(1)),
                  pl.BlockSpec((1, s, gw), col(2)),
                  pl.BlockSpec((1, s, gw), col(3)),
                  pl.BlockSpec((1, 2 * n_heads, nc, M_CHUNK), lambda b, g: (b, 0, 0, 0)),
                  pl.BlockSpec((CONV_WIDTH, gw), wcol(0)),
                  pl.BlockSpec((CONV_WIDTH, gw), wcol(1)),
                  pl.BlockSpec((1, gw), wcol(0)),
                  pl.BlockSpec((1, gw), wcol(1)),
                  pl.BlockSpec((1, gw), wcol(0))],
        out_specs=pl.BlockSpec((1, s, gw), col(0)),
        out_shape=jax.ShapeDtypeStruct((bsz, s, n_heads * head_dim), BF16),
        scratch_shapes=[pltpu.VMEM((group, nc, M_CHUNK), F32), pltpu.VMEM((group, nc, M_CHUNK), F32),
                        pltpu.VMEM((group, head_dim, head_dim), F32)],
        compiler_params=_params("arbitrary", "arbitrary"),
        name="mlstm",
    )(gate_b, pm, pm, pm, pm, gates, conv_w, conv_w, conv_b, conv_b, norm_g)


def _diffattn_kernel(q_ref, k_ref, v_ref, lam_ref, ng_ref, o_ref, m_scr, l_scr, acc_scr,
                     *, dh, lam_init, group):
    i = pl.program_id(2)
    tq = q_ref.shape[1]
    hw = 2 * dh
    scale = dh ** -0.5
    exact_scale = math.frexp(scale)[0] == 0.5
    lane = lax.broadcasted_iota(jnp.int32, (tq, hw), 1)
    qs = []
    for u in range(group):
        q = q_ref[0, :, u * hw:(u + 1) * hw]
        if exact_scale:
            q = q * scale
        zero = jnp.zeros_like(q)
        qs += [jnp.where(lane < dh, q, zero), jnp.where(lane < dh, zero, q)]
    lp = lam_ref[...]
    lam = (jnp.exp(jnp.sum(lp[0:1] * lp[1:2], axis=1, keepdims=True))
           - jnp.exp(jnp.sum(lp[2:3] * lp[3:4], axis=1, keepdims=True)) + lam_init)
    on_or_below_diag = (lax.broadcasted_iota(jnp.int32, (tq, tq), 1)
                        <= lax.broadcasted_iota(jnp.int32, (tq, tq), 0))

    def block(k0, width, rows, mask, first):
        n_rows = rows.stop - rows.start
        for st in range(2 * group):
            cols = slice((st // 2) * hw, (st // 2 + 1) * hw)
            kb = k_ref[0, pl.ds(k0, width), cols]
            vb = v_ref[0, pl.ds(k0, width), cols]
            s = lax.dot_general(qs[st][rows, :], kb, NT, preferred_element_type=F32)
            if not exact_scale:
                s = s * scale
            if mask is not None:
                s = jnp.where(mask, s, -jnp.inf)
            m_new = jnp.broadcast_to(jnp.max(s, axis=1, keepdims=True), (n_rows, LANES))
            if not first:
                m_old = m_scr[st, rows, :]
                m_new = jnp.maximum(m_old, m_new)
                alpha = jnp.exp(m_old - m_new)
            e = [jnp.exp(s[:, c:c + LANES] - m_new) for c in range(0, width, LANES)]
            part = functools.reduce(jnp.add, e)
            pv = jnp.dot(jnp.concatenate([x.astype(BF16) for x in e], axis=1), vb, preferred_element_type=F32)
            l_scr[st, rows, :] = part if first else alpha * l_scr[st, rows, :] + part
            acc_scr[st, rows, :] = pv if first else alpha * acc_scr[st, rows, :] + pv
            m_scr[st, rows, :] = m_new

    def below(j, carry):
        block(pl.multiple_of(j * 2 * tq, 2 * tq), 2 * tq, slice(0, tq), None, False)
        return carry

    half = tq // 2
    d0 = pl.multiple_of(i * tq, tq)
    block(d0, half, slice(0, tq), on_or_below_diag[:, :half], True)
    block(d0 + half, half, slice(half, tq), on_or_below_diag[half:, half:], False)
    lax.fori_loop(0, i // 2, below, 0)

    @pl.when(i % 2 == 1)
    def _():
        block(pl.multiple_of((i - 1) * tq, tq), tq, slice(0, tq), None, False)

    for u in range(group):
        o1 = acc_scr[2 * u] / jnp.sum(l_scr[2 * u], axis=1, keepdims=True)
        o2 = acc_scr[2 * u + 1] / jnp.sum(l_scr[2 * u + 1], axis=1, keepdims=True)
        o = o1 - lam * o2
        y = o * lax.rsqrt(jnp.mean(o * o, axis=-1, keepdims=True) + EPS) * ng_ref[...] * (1.0 - lam_init)
        o_ref[0, :, u * hw:(u + 1) * hw] = y.astype(o_ref.dtype)


def _diffattn(pd, lam_p, norm_g, n_heads, dh, lam_init, tq=512, group=4):
    bsz, s, _ = pd.shape
    tq = min(tq, s)
    hw = 2 * dh
    gw = group * hw
    n_groups = n_heads // group
    kern = functools.partial(_diffattn_kernel, dh=dh, lam_init=lam_init, group=group)
    return pl.pallas_call(
        kern,
        grid=(bsz, n_groups, s // tq),
        in_specs=[pl.BlockSpec((1, tq, gw), lambda b, g, i: (b, i, g)),
                  pl.BlockSpec((1, s, gw), lambda b, g, i: (b, 0, n_groups + g)),
                  pl.BlockSpec((1, s, gw), lambda b, g, i: (b, 0, 2 * n_groups + g)),
                  pl.BlockSpec((4, dh), lambda b, g, i: (0, 0)),
                  pl.BlockSpec((1, hw), lambda b, g, i: (0, 0))],
        out_specs=pl.BlockSpec((1, tq, gw), lambda b, g, i: (b, i, g)),
        out_shape=jax.ShapeDtypeStruct((bsz, s, n_heads * hw), BF16),
        scratch_shapes=[pltpu.VMEM((2 * group, tq, LANES), F32), pltpu.VMEM((2 * group, tq, LANES), F32),
                        pltpu.VMEM((2 * group, tq, hw), F32)],
        compiler_params=_params("arbitrary", "arbitrary", "arbitrary"),
        name="diffattn",
    )(pd, pd, pd, lam_p, norm_g)


def _outproj_kernel(hm_ref, od_ref, x_ref, g1_ref, sh_ref, sc_ref, ng_ref, wm_ref, wd_ref, wq_ref,
                    x1_ref, h2_ref, q_ref):
    y = (jnp.dot(hm_ref[0], wm_ref[...], preferred_element_type=F32)
         + jnp.dot(od_ref[0], wd_ref[...], preferred_element_type=F32))
    x1 = x_ref[0] + g1_ref[0] * y
    x1_ref[0] = x1
    ms = jnp.mean(x1 * x1, axis=-1, keepdims=True)
    h2 = x1 * lax.rsqrt(ms + EPS) * ng_ref[...]
    h2 = (h2 * (1.0 + sc_ref[0]) + sh_ref[0]).astype(BF16)
    h2_ref[0] = h2
    q_ref[0] = jnp.dot(h2, wq_ref[...], preferred_element_type=F32).astype(BF16)


def _outproj(hm, od, x, g1, sh2, sc2, ng, wm, wd, wq, tm=512):
    bsz, s, d = x.shape
    tm = min(tm, s)
    wm_w, wd_w, nq = hm.shape[2], od.shape[2], wq.shape[1]
    tok = lambda b, i: (b, i, 0)
    per_b = lambda b, i: (b, 0, 0)
    const = lambda b, i: (0, 0)
    return pl.pallas_call(
        _outproj_kernel,
        grid=(bsz, s // tm),
        in_specs=[pl.BlockSpec((1, tm, wm_w), tok),
                  pl.BlockSpec((1, tm, wd_w), tok),
                  pl.BlockSpec((1, tm, d), tok),
                  pl.BlockSpec((1, 1, d), per_b),
                  pl.BlockSpec((1, 1, d), per_b),
                  pl.BlockSpec((1, 1, d), per_b),
                  pl.BlockSpec((1, d), const),
                  pl.BlockSpec((wm_w, d), const),
                  pl.BlockSpec((wd_w, d), const),
                  pl.BlockSpec((d, nq), const)],
        out_specs=[pl.BlockSpec((1, tm, d), tok),
                   pl.BlockSpec((1, tm, d), tok),
                   pl.BlockSpec((1, tm, nq), tok)],
        out_shape=[jax.ShapeDtypeStruct((bsz, s, d), F32),
                   jax.ShapeDtypeStruct((bsz, s, d), BF16),
                   jax.ShapeDtypeStruct((bsz, s, nq), BF16)],
        compiler_params=_params("arbitrary", "arbitrary"),
        name="outproj",
    )(hm, od, x, g1, sh2, sc2, ng, wm, wd, wq)


def _oddeven_merge_sort(n):
    def merge(lo, hi, r):
        step = 2 * r
        if step < hi - lo:
            yield from merge(lo, hi, step)
            yield from merge(lo + r, hi, step)
            yield from ((i, i + r) for i in range(lo + r, hi - r, step))
        else:
            yield (lo, lo + r)

    def sort(lo, hi):
        if hi > lo:
            mid = lo + (hi - lo) // 2
            yield from sort(lo, mid)
            yield from sort(mid + 1, hi)
            yield from merge(lo, hi, 1)

    return list(sort(0, n - 1))


def _topk_rows_sorted(s, k, payload=None):
    n_rows, t = s.shape
    nv = n_rows // SUBLANES
    assert nv & (nv - 1) == 0
    slabs = lambda a: [a[SUBLANES * v:SUBLANES * (v + 1), :] for v in range(nv)]
    sub = lax.broadcasted_iota(jnp.int32, (SUBLANES, t), 0)
    out_row = lax.broadcasted_iota(jnp.int32, (k, t), 0)
    val = slabs(s)
    pay = None if payload is None else slabs(payload)
    sid = list(range(nv))
    for i, j in _oddeven_merge_sort(nv):
        va, vb, ia, ib = val[i], val[j], sid[i], sid[j]
        if isinstance(ia, int) and isinstance(ib, int):
            first = va >= vb if ia < ib else va > vb
        else:
            first = (va > vb) | ((va == vb) & (ia < ib))
        val[i], val[j] = jnp.maximum(va, vb), jnp.minimum(va, vb)
        sid[i], sid[j] = jnp.where(first, ia, ib), jnp.where(first, ib, ia)
        if pay is not None:
            pay[i], pay[j] = jnp.where(first, pay[i], pay[j]), jnp.where(first, pay[j], pay[i])
    row = [x * SUBLANES + sub for x in sid]
    vals = jnp.zeros((k, t), s.dtype)
    rows = jnp.zeros((k, t), jnp.int32)
    pays = None if pay is None else jnp.zeros((k, t), payload.dtype)
    for r in range(k):
        m = jnp.max(val[0], axis=0, keepdims=True)
        idx = jnp.min(jnp.where(val[0] == m, row[0], n_rows), axis=0, keepdims=True)
        hit = row[0] == idx
        vals = jnp.where(out_row == r, m, vals)
        rows = jnp.where(out_row == r, idx, rows)
        if pay is not None:
            pays = jnp.where(out_row == r, jnp.sum(jnp.where(hit, pay[0], 0), axis=0, keepdims=True), pays)
        live = min(nv, k - 1 - r)
        for q in range(min(live, nv - 1)):
            val[q] = jnp.where(hit, val[q + 1], val[q])
            row[q] = jnp.where(hit, row[q + 1], row[q])
            if pay is not None:
                pay[q] = jnp.where(hit, pay[q + 1], pay[q])
        if live == nv:
            val[nv - 1] = jnp.where(hit, -jnp.inf, val[nv - 1])
    return vals, rows, pays


def _candidate_layout(k):
    slabs, cur = [], []

    def flush():
        n_valid = len(cur)
        while len(cur) < SUBLANES:
            r = len(cur)
            period = 1
            while period < max(b for _, b in cur[:n_valid]) + 1:
                period *= 2
            b = cur[r - period][1] if r >= period else r
            cur.append((cur[n_valid - 1][0], b))
        slabs.append(([a for a, _ in cur], [b for _, b in cur], n_valid))
        cur.clear()

    for a in range(k):
        nb = k // (a + 1)
        for b0 in range(0, nb, SUBLANES):
            group = [(a, b) for b in range(b0, min(nb, b0 + SUBLANES))]
            if cur and (len(cur) + len(group) > SUBLANES or (nb == 1 and a % SUBLANES == 0)):
                flush()
            cur.extend(group)
    if cur:
        flush()
    return slabs


def _rows_by_pattern(x, pattern, sub):
    p0 = pattern[0]
    if p0 % SUBLANES == 0 and pattern == list(range(p0, p0 + SUBLANES)):
        return x[p0:p0 + SUBLANES, :]
    period = SUBLANES
    while period > 1 and all(pattern[r] == pattern[r % (period // 2)] for r in range(SUBLANES)):
        period //= 2
    pos = sub if period == SUBLANES else sub & (period - 1)
    out = x[p0:p0 + 1, :]
    for r in range(1, period):
        if pattern[r] != pattern[r - 1]:
            out = jnp.where(pos >= r, x[pattern[r]:pattern[r] + 1, :], out)
    return jnp.broadcast_to(out, (SUBLANES, x.shape[1]))


def _route_kernel(q_ref, keys_ref, gm_ref, sv_scr, si_scr, it_scr, jt_scr, gt_scr, il_scr, jl_scr, gl_scr,
                  w_scr, *, n_heads, n_keys, stride):
    K = P_TOPK
    t = q_ref.shape[0]
    slots = n_heads * K
    step = pl.program_id(0)
    cur = step % 2
    prev = 1 - cur
    sub = lax.broadcasted_iota(jnp.int32, (SUBLANES, t), 0)
    key_row = lax.broadcasted_iota(jnp.int32, (n_keys, slots), 0)
    layout = _candidate_layout(K)
    ways2 = 4
    n_loop = n_heads // ways2
    blk_per_iter = n_keys // n_loop

    @pl.when(step == 0)
    def _():
        il_scr[1] = jnp.zeros((t, slots), jnp.int32)
        jl_scr[1] = jnp.zeros((t, slots), jnp.int32)
        gl_scr[1] = jnp.zeros((t, slots), F32)

    def gate_rows(tok):
        i_row = il_scr[prev, pl.ds(tok, 1), :]
        j_row = jl_scr[prev, pl.ds(tok, 1), :]
        g_row = gl_scr[prev, pl.ds(tok, 1), :]
        a_t = jnp.where(key_row == i_row, 1.0, 0.0).astype(BF16)
        b_t = jnp.where(key_row == j_row, g_row, 0.0).astype(BF16)
        m_t = lax.dot_general(a_t, b_t, NT, preferred_element_type=F32)
        w_scr[pl.ds(tok, n_keys, stride=stride), :] = m_t

    ways = 4
    n_loop1 = 2 * n_heads // ways
    tok_per_iter = t // n_loop1

    def stage1(hh, _):
        for p in range(ways):
            hp = ways * hh + p
            c0 = pl.multiple_of(hp * LANES, LANES)
            qb = q_ref[:, pl.ds(c0, LANES)]
            st = lax.dot_general(keys_ref[hp], qb, NT, preferred_element_type=F32)
            v, ix, _ = _topk_rows_sorted(st, K)
            sv_scr[hp] = v
            si_scr[hp] = ix
        t0 = pl.multiple_of(hh * tok_per_iter, tok_per_iter)
        for u in range(tok_per_iter):
            gate_rows(t0 + u)
        return 0

    lax.fori_loop(0, n_loop1, stage1, 0, unroll=True)

    def stage2(hh, _):
        b0 = hh * blk_per_iter
        for u in range(blk_per_iter):
            r0 = pl.multiple_of((b0 + u) * stride, SUBLANES)
            gm_ref[b0 + u] = w_scr[pl.ds(r0, t), :].astype(gm_ref.dtype)
        for u in range(ways2):
            h = ways2 * hh + u
            sv0, sv1 = sv_scr[2 * h], sv_scr[2 * h + 1]
            si0, si1 = si_scr[2 * h], si_scr[2 * h + 1]
            cand, cidx = [], []
            for a_pat, b_pat, n_valid in layout:
                c = _rows_by_pattern(sv0, a_pat, sub) + _rows_by_pattern(sv1, b_pat, sub)
                if n_valid < SUBLANES:
                    c = jnp.where(sub < n_valid, c, -jnp.inf)
                cand.append(c)
                cidx.append(_rows_by_pattern(si0, a_pat, sub) * n_keys + _rows_by_pattern(si1, b_pat, sub))
            while len(cand) & (len(cand) - 1):
                cand.append(jnp.full((SUBLANES, t), -jnp.inf, F32))
                cidx.append(jnp.zeros((SUBLANES, t), jnp.int32))
            fv, _, eidx = _topk_rows_sorted(jnp.concatenate(cand, axis=0), K, jnp.concatenate(cidx, axis=0))
            e = jnp.exp(fv - fv[0:1, :])
            g = e / jnp.sum(e, axis=0, keepdims=True)
            r0 = pl.multiple_of(h * K, K)
            it_scr[pl.ds(r0, K), :] = eidx // n_keys
            jt_scr[pl.ds(r0, K), :] = eidx % n_keys
            gt_scr[pl.ds(r0, K), :] = g
        return 0

    lax.fori_loop(0, n_loop, stage2, 0, unroll=True)
    il_scr[cur] = it_scr[...].T
    jl_scr[cur] = jt_scr[...].T
    gl_scr[cur] = gt_scr[...].T


def _route(q, keys, n_heads, n_keys):
    n = q.shape[0]
    t = LANES
    n_tiles = n // t
    slots = n_heads * P_TOPK
    stride = t + SUBLANES
    kern = functools.partial(_route_kernel, n_heads=n_heads, n_keys=n_keys, stride=stride)
    return pl.pallas_call(
        kern,
        grid=(n_tiles + 1,),
        in_specs=[pl.BlockSpec((t, q.shape[1]), lambda s: (jnp.minimum(s, n_tiles - 1), 0)),
                  pl.BlockSpec(keys.shape, lambda s: (0, 0, 0))],
        out_specs=pl.BlockSpec((n_keys, t, n_keys), lambda s: (0, jnp.maximum(s - 1, 0), 0)),
        out_shape=jax.ShapeDtypeStruct((n_keys, n, n_keys), BF16),
        scratch_shapes=[pltpu.VMEM((2 * n_heads, P_TOPK, t), F32),
                        pltpu.VMEM((2 * n_heads, P_TOPK, t), jnp.int32),
                        pltpu.VMEM((slots, t), jnp.int32),
                        pltpu.VMEM((slots, t), jnp.int32),
                        pltpu.VMEM((slots, t), F32),
                        pltpu.VMEM((2, t, slots), jnp.int32),
                        pltpu.VMEM((2, t, slots), jnp.int32),
                        pltpu.VMEM((2, t, slots), F32),
                        pltpu.VMEM((n_keys * stride, n_keys), F32)],
        compiler_params=_params("arbitrary"),
        name="route",
    )(q, keys)


def _experts_kernel(h2_ref, u_ref, v_ref, m_ref, x1_ref, g2_ref, fg_ref, o_ref, acc_ref, *, final):
    j = pl.program_id(2)

    @pl.when(j == 0)
    def _():
        acc_ref[...] = jnp.zeros_like(acc_ref)

    s = lax.dot_general(h2_ref[0], u_ref[...].astype(BF16), NT, preferred_element_type=F32)
    act = 0.5 * s * (1.0 + lax.erf(s * (2.0 ** -0.5)))
    gate = jnp.concatenate([m_ref[ib] for ib in range(m_ref.shape[0])], axis=1)
    w = (gate.astype(F32) * act).astype(BF16)
    acc_ref[...] += jnp.dot(w, v_ref[...].astype(BF16), preferred_element_type=F32)

    @pl.when(j == pl.num_programs(2) - 1)
    def _():
        x2 = x1_ref[0] + g2_ref[0] * acc_ref[...]
        if final:
            x2 = x2 * lax.rsqrt(jnp.mean(x2 * x2, axis=-1, keepdims=True) + EPS) * fg_ref[...]
        o_ref[0] = x2


def _experts(h2, u, v, gmat, x1, g2, final_g, final, tm=1024, te=1024):
    bsz, s, d = x1.shape
    tm = min(tm, s)
    ne = u.shape[0]
    n_keys = gmat.shape[2]
    n_tiles = s // tm
    kern = functools.partial(_experts_kernel, final=final)
    return pl.pallas_call(
        kern,
        grid=(bsz, n_tiles, ne // te),
        in_specs=[pl.BlockSpec((1, tm, d), lambda b, i, j: (b, i, 0)),
                  pl.BlockSpec((te, d), lambda b, i, j: (j, 0)),
                  pl.BlockSpec((te, d), lambda b, i, j: (j, 0)),
                  pl.BlockSpec((te // n_keys, tm, n_keys), lambda b, i, j: (j, b * n_tiles + i, 0)),
                  pl.BlockSpec((1, tm, d), lambda b, i, j: (b, i, 0)),
                  pl.BlockSpec((1, 1, d), lambda b, i, j: (b, 0, 0)),
                  pl.BlockSpec((1, d), lambda b, i, j: (0, 0))],
        out_specs=pl.BlockSpec((1, tm, d), lambda b, i, j: (b, i, 0)),
        out_shape=jax.ShapeDtypeStruct((bsz, s, d), F32),
        scratch_shapes=[pltpu.VMEM((tm, d), F32)],
        compiler_params=_params("arbitrary", "arbitrary", "arbitrary"),
        name="experts",
    )(h2, u, v, gmat, x1, g2, final_g)


def kernel(x, c, ada_w, ada_b, norm1_g, w_in, conv_w, conv_b, mlstm_gate_b, mlstm_norm_g, lambda_q1,
           lambda_k1, lambda_q2, lambda_k2, diff_norm_g, w_out, norm2_g, peer_w_query, peer_sub_keys,
           peer_u, peer_v, final_g):
    bsz, s, d = x.shape
    depth = ada_w.shape[0]
    m_heads = mlstm_gate_b.shape[1] // 2
    m_width = mlstm_norm_g.shape[1]
    m_hd = m_width // m_heads
    dh = lambda_q1.shape[1]
    d_width = w_out.shape[1] - m_width
    d_heads = d_width // (2 * dh)
    p_heads, _, n_keys, _ = peer_sub_keys.shape[1:]
    n_m = 4 * m_width
    n_g = 2 * m_heads
    assert s % M_CHUNK == 0 and m_hd == LANES and 2 * dh == LANES and n_keys == LANES
    assert peer_sub_keys.shape[-1] == LANES and p_heads % 2 == 0 and n_keys == SUBLANES * P_TOPK

    for l in range(depth):
        mod = _adaln(c, ada_w[l], ada_b[l])
        sh1, sc1, g1, sh2, sc2, g2 = (a.reshape(bsz, 1, d) for a in jnp.split(mod, 6, axis=-1))

        w_m = w_in[l][:, :n_m].astype(BF16)
        w_g = jnp.pad(w_in[l][:, n_m:n_m + n_g].astype(BF16), ((0, 0), (0, LANES - n_g)))
        w_d = w_in[l][:, n_m + n_g:].astype(BF16)
        pm, pg, pd = _inproj(x, sh1, sc1, norm1_g[l].reshape(1, d), w_m, w_g, w_d)

        gates = pg[:, :, :n_g].transpose(0, 2, 1).reshape(bsz, n_g, s // M_CHUNK, M_CHUNK)
        hm = _mlstm(pm, gates, mlstm_gate_b[l], conv_w[l], conv_b[l].reshape(1, -1),
                    mlstm_norm_g[l].reshape(1, -1), m_heads, m_hd)

        lam_init = 0.8 - 0.6 * math.exp(-0.3 * l)
        lam_p = jnp.stack([lambda_q1[l], lambda_k1[l], lambda_q2[l], lambda_k2[l]])
        od = _diffattn(pd, lam_p, diff_norm_g[l].reshape(1, -1), d_heads, dh, lam_init)

        x1, h2, q = _outproj(hm, od, x, g1, sh2, sc2, norm2_g[l].reshape(1, d), w_out[l][:m_width].astype(BF16),
                             w_out[l][m_width:].astype(BF16), peer_w_query[l].astype(BF16))

        keys = peer_sub_keys[l].reshape(2 * p_heads, n_keys, -1).astype(BF16)
        gmat = _route(q.reshape(bsz * s, -1), keys, p_heads, n_keys)
        x = _experts(h2, peer_u[l], peer_v[l], gmat, x1, g2,
                     final_g.reshape(1, d), final=(l == depth - 1))
    return x
```

```python
import functools
import math

import jax
import jax.numpy as jnp
from jax import lax
from jax.experimental import pallas as pl
from jax.experimental.pallas import tpu as pltpu

F32 = jnp.float32
BF16 = jnp.bfloat16
EPS = 1e-6
LANES = 128
SUBLANES = 8
VMEM_LIMIT = 56 * 1024 * 1024
M_CHUNK = 128
CONV_WIDTH = 4
P_TOPK = 16

NT = (((1,), (1,)), ((), ()))
TN = (((0,), (0,)), ((), ()))


def _params(*sem):
    return pltpu.CompilerParams(dimension_semantics=sem, vmem_limit_bytes=VMEM_LIMIT)


def _adaln_kernel(c_ref, w_ref, b_ref, o_ref):
    c = c_ref[...]
    sc = (c * jax.nn.sigmoid(c)).astype(BF16)
    o_ref[...] = jnp.dot(sc, w_ref[...].astype(BF16), preferred_element_type=F32) + b_ref[...]


def _adaln(c, w, b):
    bsz, d = c.shape
    n = w.shape[1]
    tn = 1536
    return pl.pallas_call(
        _adaln_kernel,
        grid=(n // tn,),
        in_specs=[pl.BlockSpec((bsz, d), lambda j: (0, 0)),
                  pl.BlockSpec((d, tn), lambda j: (0, j)),
                  pl.BlockSpec((1, tn), lambda j: (0, j))],
        out_specs=pl.BlockSpec((bsz, tn), lambda j: (0, j)),
        out_shape=jax.ShapeDtypeStruct((bsz, n), F32),
        compiler_params=_params("arbitrary"),
        name="adaln",
    )(c, w, b.reshape(1, n))


def _inproj_kernel(x_ref, sh_ref, sc_ref, g_ref, wm_ref, wg_ref, wd_ref, pm_ref, pg_ref, pd_ref):
    x = x_ref[0]
    ms = jnp.mean(x * x, axis=-1, keepdims=True)
    h = x * lax.rsqrt(ms + EPS) * g_ref[...]
    h = h * (1.0 + sc_ref[0]) + sh_ref[0]
    hb = h.astype(BF16)
    pm_ref[0] = jnp.dot(hb, wm_ref[...], preferred_element_type=F32)
    pg_ref[0] = jnp.dot(hb, wg_ref[...], preferred_element_type=F32)
    pd_ref[0] = jnp.dot(hb, wd_ref[...], preferred_element_type=F32).astype(BF16)


def _inproj(x, sh, sc, g, wm, wg, wd, tm=512):
    bsz, s, d = x.shape
    tm = min(tm, s)
    nm, ng, nd = wm.shape[1], wg.shape[1], wd.shape[1]
    tok = lambda b, i: (b, i, 0)
    per_b = lambda b, i: (b, 0, 0)
    const = lambda b, i: (0, 0)
    return pl.pallas_call(
        _inproj_kernel,
        grid=(bsz, s // tm),
        in_specs=[pl.BlockSpec((1, tm, d), tok),
                  pl.BlockSpec((1, 1, d), per_b),
                  pl.BlockSpec((1, 1, d), per_b),
                  pl.BlockSpec((1, d), const),
                  pl.BlockSpec((d, nm), const),
                  pl.BlockSpec((d, ng), const),
                  pl.BlockSpec((d, nd), const)],
        out_specs=[pl.BlockSpec((1, tm, nm), tok),
                   pl.BlockSpec((1, tm, ng), tok),
                   pl.BlockSpec((1, tm, nd), tok)],
        out_shape=[jax.ShapeDtypeStruct((bsz, s, nm), F32),
                   jax.ShapeDtypeStruct((bsz, s, ng), F32),
                   jax.ShapeDtypeStruct((bsz, s, nd), BF16)],
        compiler_params=_params("arbitrary", "arbitrary"),
        name="inproj",
    )(x, sh, sc, g, wm, wg, wd)


def _causal_conv(u, tail, w, b):
    ext = jnp.concatenate([tail, u], axis=0)
    y = u * w[CONV_WIDTH - 1:CONV_WIDTH, :]
    for k in range(CONV_WIDTH - 1):
        shift = CONV_WIDTH - 1 - k
        y = y + pltpu.roll(ext, shift, axis=0)[SUBLANES:, :] * w[k:k + 1, :]
    return y + b


def _cumsum_lanes(x):
    lane = lax.broadcasted_iota(jnp.int32, x.shape, 1)
    s = 1
    while s < x.shape[1]:
        x = x + jnp.where(lane >= s, pltpu.roll(x, s, axis=1), 0.0)
        s *= 2
    return x


def _mlstm_kernel(gb_ref, mq_ref, mk_ref, mv_ref, mo_ref, g_ref, cwq_ref, cwk_ref, cbq_ref, cbk_ref,
                  ng_ref, o_ref, b_scr, i_scr, c_scr, *, n_heads, head_dim, group):
    h0 = pl.program_id(1) * group
    L = M_CHUNK
    hd = head_dim
    nc = b_scr.shape[1]
    silu = lambda a: a * jax.nn.sigmoid(a)

    for u in range(group):
        i_scr[u] = g_ref[0, h0 + u] + gb_ref[h0 + u]
        fpre = g_ref[0, n_heads + h0 + u] + gb_ref[n_heads + h0 + u]
        lf = jnp.minimum(fpre, 0.0) - jnp.log1p(jnp.exp(-jnp.abs(fpre)))
        b_scr[u] = _cumsum_lanes(lf)

    r_i = lax.broadcasted_iota(jnp.int32, (L, L), 0)
    c_i = lax.broadcasted_iota(jnp.int32, (L, L), 1)
    eye = r_i == c_i
    causal = c_i <= r_i

    full = lambda col: jnp.broadcast_to(col, (L, LANES))

    def to_col(row):
        return full(jnp.sum(jnp.where(eye, row, 0.0), axis=1, keepdims=True))

    def chunk(c, carry):
        t0 = pl.multiple_of(c * L, L)
        return tuple(head_chunk(c, t0, u, carry[u]) for u in range(group))

    def head_chunk(c, t0, u, carry):
        n, m, q_tail, k_tail = carry
        cols = slice(u * hd, (u + 1) * hd)
        q_raw = mq_ref[0, pl.ds(t0, L), cols]
        k_raw = mk_ref[0, pl.ds(t0, L), cols]
        qc = silu(_causal_conv(q_raw, q_tail, cwq_ref[:, cols], cbq_ref[:, cols])).astype(BF16)
        kc = (silu(_causal_conv(k_raw, k_tail, cwk_ref[:, cols], cbk_ref[:, cols])) * (hd ** -0.5)).astype(BF16)
        vc = mv_ref[0, pl.ds(t0, L), cols].astype(BF16)
        b_row = b_scr[u, pl.ds(c, 1), :]
        i_row = i_scr[u, pl.ds(c, 1), :]
        b_col = to_col(b_row)
        i_col = to_col(i_row)

        logD = jnp.where(causal, b_col - b_row + i_row, -jnp.inf)
        m_t = jnp.maximum(b_col + m, full(jnp.max(logD, axis=1, keepdims=True)))
        Dw = jnp.exp(logD - m_t)
        inter = jnp.exp(b_col + m - m_t)
        sqk = lax.dot_general(qc, kc, NT, preferred_element_type=F32) * Dw
        num = (jnp.dot(sqk.astype(BF16), vc, preferred_element_type=F32)
               + inter * jnp.dot(qc, c_scr[u].astype(BF16), preferred_element_type=F32))
        den = (full(jnp.sum(sqk, axis=1, keepdims=True))
               + inter * full(jnp.sum(qc.astype(F32) * n, axis=1, keepdims=True)))
        hh = num / jnp.maximum(jnp.abs(den), jnp.exp(-m_t))

        bL = b_row[:, L - 1:L]
        m_new = jnp.maximum(bL + m, jnp.max(bL - b_row + i_row, axis=1, keepdims=True))
        w_col = jnp.exp(bL - b_col + i_col - m_new)
        decay = jnp.exp(bL + m - m_new)
        kw = kc.astype(F32) * w_col
        c_scr[u] = decay * c_scr[u] + lax.dot_general(kw.astype(BF16), vc, TN, preferred_element_type=F32)
        n_new = decay * n + jnp.sum(kw, axis=0, keepdims=True)

        y = hh * lax.rsqrt(full(jnp.mean(hh * hh, axis=-1, keepdims=True)) + EPS) * ng_ref[:, cols]
        y = y * jax.nn.sigmoid(mo_ref[0, pl.ds(t0, L), cols])
        o_ref[0, pl.ds(t0, L), cols] = y.astype(o_ref.dtype)
        return n_new, m_new, q_raw[L - SUBLANES:, :], k_raw[L - SUBLANES:, :]

    zeros = lambda *shape: jnp.zeros(shape, F32)
    c_scr[...] = zeros(*c_scr.shape)
    init = (zeros(1, hd), zeros(1, 1), zeros(SUBLANES, hd), zeros(SUBLANES, hd))
    lax.fori_loop(0, nc, chunk, (init,) * group)


def _mlstm(pm, gates, gate_b, conv_w, conv_b, norm_g, n_heads, head_dim, group=4):
    bsz, s, _ = pm.shape
    nc = s // M_CHUNK
    gw = group * head_dim
    n_groups = n_heads // group
    col = lambda off: (lambda b, g: (b, 0, off * n_groups + g))
    wcol = lambda off: (lambda b, g: (0, off * n_groups + g))
    kern = functools.partial(_mlstm_kernel, n_heads=n_heads, head_dim=head_dim, group=group)
    return pl.pallas_call(
        kern,
        grid=(bsz, n_groups),
        in_specs=[pl.BlockSpec(memory_space=pltpu.SMEM),
                  pl.BlockSpec((1, s, gw), col(0)),
                  pl.BlockSpec((1, s, gw), col(1)),
                  pl.BlockSpec((1, s, gw), col(2)),
                  pl.BlockSpec((1, s, gw), col(3)),
                  pl.BlockSpec((1, 2 * n_heads, nc, M_CHUNK), lambda b, g: (b, 0, 0, 0)),
                  pl.BlockSpec((CONV_WIDTH, gw), wcol(0)),
                  pl.BlockSpec((CONV_WIDTH, gw), wcol(1)),
                  pl.BlockSpec((1, gw), wcol(0)),
                  pl.BlockSpec((1, gw), wcol(1)),
                  pl.BlockSpec((1, gw), wcol(0))],
        out_specs=pl.BlockSpec((1, s, gw), col(0)),
        out_shape=jax.ShapeDtypeStruct((bsz, s, n_heads * head_dim), BF16),
        scratch_shapes=[pltpu.VMEM((group, nc, M_CHUNK), F32), pltpu.VMEM((group, nc, M_CHUNK), F32),
                        pltpu.VMEM((group, head_dim, head_dim), F32)],
        compiler_params=_params("arbitrary", "arbitrary"),
        name="mlstm",
    )(gate_b, pm, pm, pm, pm, gates, conv_w, conv_w, conv_b, conv_b, norm_g)


def _diffattn_kernel(q_ref, k_ref, v_ref, lam_ref, ng_ref, o_ref, m_scr, l_scr, acc_scr,
                     *, dh, lam_init, group):
    i = pl.program_id(2)
    tq = q_ref.shape[1]
    hw = 2 * dh
    scale = dh ** -0.5
    exact_scale = math.frexp(scale)[0] == 0.5
    lane = lax.broadcasted_iota(jnp.int32, (tq, hw), 1)
    qs = []
    for u in range(group):
        q = q_ref[0, :, u * hw:(u + 1) * hw]
        if exact_scale:
            q = q * scale
        zero = jnp.zeros_like(q)
        qs += [jnp.where(lane < dh, q, zero), jnp.where(lane < dh, zero, q)]
    lp = lam_ref[...]
    lam = (jnp.exp(jnp.sum(lp[0:1] * lp[1:2], axis=1, keepdims=True))
           - jnp.exp(jnp.sum(lp[2:3] * lp[3:4], axis=1, keepdims=True)) + lam_init)
    on_or_below_diag = (lax.broadcasted_iota(jnp.int32, (tq, tq), 1)
                        <= lax.broadcasted_iota(jnp.int32, (tq, tq), 0))

    def block(k0, width, rows, mask, first):
        n_rows = rows.stop - rows.start
        scores = []
        for st in range(2 * group):
            cols = slice((st // 2) * hw, (st // 2 + 1) * hw)
            kb = k_ref[0, pl.ds(k0, width), cols]
            scores.append(lax.dot_general(qs[st][rows, :], kb, NT, preferred_element_type=F32))
        for st in range(2 * group):
            cols = slice((st // 2) * hw, (st // 2 + 1) * hw)
            vb = v_ref[0, pl.ds(k0, width), cols]
            s = scores[st]
            if not exact_scale:
                s = s * scale
            if mask is not None:
                s = jnp.where(mask, s, -jnp.inf)
            m_new = jnp.broadcast_to(jnp.max(s, axis=1, keepdims=True), (n_rows, LANES))
            if not first:
                m_old = m_scr[st, rows, :]
                m_new = jnp.maximum(m_old, m_new)
                alpha = jnp.exp(m_old - m_new)
            e = [jnp.exp(s[:, c:c + LANES] - m_new) for c in range(0, width, LANES)]
            part = functools.reduce(jnp.add, e)
            pv = jnp.dot(jnp.concatenate([x.astype(BF16) for x in e], axis=1), vb, preferred_element_type=F32)
            l_scr[st, rows, :] = part if first else alpha * l_scr[st, rows, :] + part
            acc_scr[st, rows, :] = pv if first else alpha * acc_scr[st, rows, :] + pv
            m_scr[st, rows, :] = m_new

    def below(j, carry):
        block(pl.multiple_of(j * 2 * tq, 2 * tq), 2 * tq, slice(0, tq), None, False)
        return carry

    half = tq // 2
    d0 = pl.multiple_of(i * tq, tq)
    block(d0, half, slice(0, tq), on_or_below_diag[:, :half], True)
    block(d0 + half, half, slice(half, tq), on_or_below_diag[half:, half:], False)
    lax.fori_loop(0, i // 2, below, 0)

    @pl.when(i % 2 == 1)
    def _():
        block(pl.multiple_of((i - 1) * tq, tq), tq, slice(0, tq), None, False)

    for u in range(group):
        o1 = acc_scr[2 * u] / jnp.sum(l_scr[2 * u], axis=1, keepdims=True)
        o2 = acc_scr[2 * u + 1] / jnp.sum(l_scr[2 * u + 1], axis=1, keepdims=True)
        o = o1 - lam * o2
        y = o * lax.rsqrt(jnp.mean(o * o, axis=-1, keepdims=True) + EPS) * ng_ref[...] * (1.0 - lam_init)
        o_ref[0, :, u * hw:(u + 1) * hw] = y.astype(o_ref.dtype)


def _diffattn(pd, lam_p, norm_g, n_heads, dh, lam_init, tq=512, group=4):
    bsz, s, _ = pd.shape
    tq = min(tq, s)
    hw = 2 * dh
    gw = group * hw
    n_groups = n_heads // group
    kern = functools.partial(_diffattn_kernel, dh=dh, lam_init=lam_init, group=group)
    return pl.pallas_call(
        kern,
        grid=(bsz, n_groups, s // tq),
        in_specs=[pl.BlockSpec((1, tq, gw), lambda b, g, i: (b, i, g)),
                  pl.BlockSpec((1, s, gw), lambda b, g, i: (b, 0, n_groups + g)),
                  pl.BlockSpec((1, s, gw), lambda b, g, i: (b, 0, 2 * n_groups + g)),
                  pl.BlockSpec((4, dh), lambda b, g, i: (0, 0)),
                  pl.BlockSpec((1, hw), lambda b, g, i: (0, 0))],
        out_specs=pl.BlockSpec((1, tq, gw), lambda b, g, i: (b, i, g)),
        out_shape=jax.ShapeDtypeStruct((bsz, s, n_heads * hw), BF16),
        scratch_shapes=[pltpu.VMEM((2 * group, tq, LANES), F32), pltpu.VMEM((2 * group, tq, LANES), F32),
                        pltpu.VMEM((2 * group, tq, hw), F32)],
        compiler_params=_params("arbitrary", "arbitrary", "arbitrary"),
        name="diffattn",
    )(pd, pd, pd, lam_p, norm_g)


def _outproj_kernel(hm_ref, od_ref, x_ref, g1_ref, sh_ref, sc_ref, ng_ref, wm_ref, wd_ref, wq_ref,
                    x1_ref, h2_ref, q_ref):
    y = (jnp.dot(hm_ref[0], wm_ref[...], preferred_element_type=F32)
         + jnp.dot(od_ref[0], wd_ref[...], preferred_element_type=F32))
    x1 = x_ref[0] + g1_ref[0] * y
    x1_ref[0] = x1
    ms = jnp.mean(x1 * x1, axis=-1, keepdims=True)
    h2 = x1 * lax.rsqrt(ms + EPS) * ng_ref[...]
    h2 = (h2 * (1.0 + sc_ref[0]) + sh_ref[0]).astype(BF16)
    h2_ref[0] = h2
    q_ref[0] = jnp.dot(h2, wq_ref[...], preferred_element_type=F32).astype(BF16)


def _outproj(hm, od, x, g1, sh2, sc2, ng, wm, wd, wq, tm=512):
    bsz, s, d = x.shape
    tm = min(tm, s)
    wm_w, wd_w, nq = hm.shape[2], od.shape[2], wq.shape[1]
    tok = lambda b, i: (b, i, 0)
    per_b = lambda b, i: (b, 0, 0)
    const = lambda b, i: (0, 0)
    return pl.pallas_call(
        _outproj_kernel,
        grid=(bsz, s // tm),
        in_specs=[pl.BlockSpec((1, tm, wm_w), tok),
                  pl.BlockSpec((1, tm, wd_w), tok),
                  pl.BlockSpec((1, tm, d), tok),
                  pl.BlockSpec((1, 1, d), per_b),
                  pl.BlockSpec((1, 1, d), per_b),
                  pl.BlockSpec((1, 1, d), per_b),
                  pl.BlockSpec((1, d), const),
                  pl.BlockSpec((wm_w, d), const),
                  pl.BlockSpec((wd_w, d), const),
                  pl.BlockSpec((d, nq), const)],
        out_specs=[pl.BlockSpec((1, tm, d), tok),
                   pl.BlockSpec((1, tm, d), tok),
                   pl.BlockSpec((1, tm, nq), tok)],
        out_shape=[jax.ShapeDtypeStruct((bsz, s, d), F32),
                   jax.ShapeDtypeStruct((bsz, s, d), BF16),
                   jax.ShapeDtypeStruct((bsz, s, nq), BF16)],
        compiler_params=_params("arbitrary", "arbitrary"),
        name="outproj",
    )(hm, od, x, g1, sh2, sc2, ng, wm, wd, wq)


def _oddeven_merge_sort(n):
    def merge(lo, hi, r):
        step = 2 * r
        if step < hi - lo:
            yield from merge(lo, hi, step)
            yield from merge(lo + r, hi, step)
            yield from ((i, i + r) for i in range(lo + r, hi - r, step))
        else:
            yield (lo, lo + r)

    def sort(lo, hi):
        if hi > lo:
            mid = lo + (hi - lo) // 2
            yield from sort(lo, mid)
            yield from sort(mid + 1, hi)
            yield from merge(lo, hi, 1)

    return list(sort(0, n - 1))


def _topk_rows_sorted(s, k, payload=None):
    n_rows, t = s.shape
    nv = n_rows // SUBLANES
    assert nv & (nv - 1) == 0
    slabs = lambda a: [a[SUBLANES * v:SUBLANES * (v + 1), :] for v in range(nv)]
    sub = lax.broadcasted_iota(jnp.int32, (SUBLANES, t), 0)
    out_row = lax.broadcasted_iota(jnp.int32, (k, t), 0)
    val = slabs(s)
    pay = None if payload is None else slabs(payload)
    sid = list(range(nv))
    for i, j in _oddeven_merge_sort(nv):
        va, vb, ia, ib = val[i], val[j], sid[i], sid[j]
        if isinstance(ia, int) and isinstance(ib, int):
            first = va >= vb if ia < ib else va > vb
        else:
            first = (va > vb) | ((va == vb) & (ia < ib))
        val[i], val[j] = jnp.maximum(va, vb), jnp.minimum(va, vb)
        sid[i], sid[j] = jnp.where(first, ia, ib), jnp.where(first, ib, ia)
        if pay is not None:
            pay[i], pay[j] = jnp.where(first, pay[i], pay[j]), jnp.where(first, pay[j], pay[i])
    row = [x * SUBLANES + sub for x in sid]
    vals = jnp.zeros((k, t), s.dtype)
    rows = jnp.zeros((k, t), jnp.int32)
    pays = None if pay is None else jnp.zeros((k, t), payload.dtype)
    for r in range(k):
        m = jnp.max(val[0], axis=0, keepdims=True)
        idx = jnp.min(jnp.where(val[0] == m, row[0], n_rows), axis=0, keepdims=True)
        hit = row[0] == idx
        vals = jnp.where(out_row == r, m, vals)
        rows = jnp.where(out_row == r, idx, rows)
        if pay is not None:
            pays = jnp.where(out_row == r, jnp.sum(jnp.where(hit, pay[0], 0), axis=0, keepdims=True), pays)
        live = min(nv, k - 1 - r)
        for q in range(min(live, nv - 1)):
            val[q] = jnp.where(hit, val[q + 1], val[q])
            row[q] = jnp.where(hit, row[q + 1], row[q])
            if pay is not None:
                pay[q] = jnp.where(hit, pay[q + 1], pay[q])
        if live == nv:
            val[nv - 1] = jnp.where(hit, -jnp.inf, val[nv - 1])
    return vals, rows, pays


def _candidate_layout(k):
    slabs, cur = [], []

    def flush():
        n_valid = len(cur)
        while len(cur) < SUBLANES:
            r = len(cur)
            period = 1
            while period < max(b for _, b in cur[:n_valid]) + 1:
                period *= 2
            b = cur[r - period][1] if r >= period else r
            cur.append((cur[n_valid - 1][0], b))
        slabs.append(([a for a, _ in cur], [b for _, b in cur], n_valid))
        cur.clear()

    for a in range(k):
        nb = k // (a + 1)
        for b0 in range(0, nb, SUBLANES):
            group = [(a, b) for b in range(b0, min(nb, b0 + SUBLANES))]
            if cur and (len(cur) + len(group) > SUBLANES or (nb == 1 and a % SUBLANES == 0)):
                flush()
            cur.extend(group)
    if cur:
        flush()
    return slabs


def _rows_by_pattern(x, pattern, sub):
    p0 = pattern[0]
    if p0 % SUBLANES == 0 and pattern == list(range(p0, p0 + SUBLANES)):
        return x[p0:p0 + SUBLANES, :]
    period = SUBLANES
    while period > 1 and all(pattern[r] == pattern[r % (period // 2)] for r in range(SUBLANES)):
        period //= 2
    pos = sub if period == SUBLANES else sub & (period - 1)
    out = x[p0:p0 + 1, :]
    for r in range(1, period):
        if pattern[r] != pattern[r - 1]:
            out = jnp.where(pos >= r, x[pattern[r]:pattern[r] + 1, :], out)
    return jnp.broadcast_to(out, (SUBLANES, x.shape[1]))


def _route_kernel(q_ref, keys_ref, gm_ref, sv_scr, si_scr, it_scr, jt_scr, gt_scr, il_scr, jl_scr, gl_scr,
                  w_scr, *, n_heads, n_keys, stride):
    K = P_TOPK
    t = q_ref.shape[0]
    slots = n_heads * K
    step = pl.program_id(0)
    cur = step % 2
    prev = 1 - cur
    sub = lax.broadcasted_iota(jnp.int32, (SUBLANES, t), 0)
    key_row = lax.broadcasted_iota(jnp.int32, (n_keys, slots), 0)
    layout = _candidate_layout(K)
    ways2 = 4
    n_loop = n_heads // ways2
    blk_per_iter = n_keys // n_loop

    @pl.when(step == 0)
    def _():
        il_scr[1] = jnp.zeros((t, slots), jnp.int32)
        jl_scr[1] = jnp.zeros((t, slots), jnp.int32)
        gl_scr[1] = jnp.zeros((t, slots), F32)

    def gate_rows(tok):
        i_row = il_scr[prev, pl.ds(tok, 1), :]
        j_row = jl_scr[prev, pl.ds(tok, 1), :]
        g_row = gl_scr[prev, pl.ds(tok, 1), :]
        a_t = jnp.where(key_row == i_row, 1.0, 0.0).astype(BF16)
        b_t = jnp.where(key_row == j_row, g_row, 0.0).astype(BF16)
        m_t = lax.dot_general(a_t, b_t, NT, preferred_element_type=F32)
        w_scr[pl.ds(tok, n_keys, stride=stride), :] = m_t

    ways = 4
    n_loop1 = 2 * n_heads // ways
    tok_per_iter = t // n_loop1

    def stage1(hh, _):
        for p in range(ways):
            hp = ways * hh + p
            c0 = pl.multiple_of(hp * LANES, LANES)
            qb = q_ref[:, pl.ds(c0, LANES)]
            st = lax.dot_general(keys_ref[hp], qb, NT, preferred_element_type=F32)
            v, ix, _ = _topk_rows_sorted(st, K)
            sv_scr[hp] = v
            si_scr[hp] = ix
        t0 = pl.multiple_of(hh * tok_per_iter, tok_per_iter)
        for u in range(tok_per_iter):
            gate_rows(t0 + u)
        return 0

    lax.fori_loop(0, n_loop1, stage1, 0, unroll=True)

    def stage2(hh, _):
        b0 = hh * blk_per_iter
        for u in range(blk_per_iter):
            r0 = pl.multiple_of((b0 + u) * stride, SUBLANES)
            gm_ref[b0 + u] = w_scr[pl.ds(r0, t), :].astype(gm_ref.dtype)
        for u in range(ways2):
            h = ways2 * hh + u
            sv0, sv1 = sv_scr[2 * h], sv_scr[2 * h + 1]
            si0, si1 = si_scr[2 * h], si_scr[2 * h + 1]
            cand, cidx = [], []
            for a_pat, b_pat, n_valid in layout:
                c = _rows_by_pattern(sv0, a_pat, sub) + _rows_by_pattern(sv1, b_pat, sub)
                if n_valid < SUBLANES:
                    c = jnp.where(sub < n_valid, c, -jnp.inf)
                cand.append(c)
                cidx.append(_rows_by_pattern(si0, a_pat, sub) * n_keys + _rows_by_pattern(si1, b_pat, sub))
            while len(cand) & (len(cand) - 1):
                cand.append(jnp.full((SUBLANES, t), -jnp.inf, F32))
                cidx.append(jnp.zeros((SUBLANES, t), jnp.int32))
            fv, _, eidx = _topk_rows_sorted(jnp.concatenate(cand, axis=0), K, jnp.concatenate(cidx, axis=0))
            e = jnp.exp(fv - fv[0:1, :])
            g = e / jnp.sum(e, axis=0, keepdims=True)
            r0 = pl.multiple_of(h * K, K)
            it_scr[pl.ds(r0, K), :] = eidx // n_keys
            jt_scr[pl.ds(r0, K), :] = eidx % n_keys
            gt_scr[pl.ds(r0, K), :] = g
        return 0

    lax.fori_loop(0, n_loop, stage2, 0, unroll=True)
    il_scr[cur] = it_scr[...].T
    jl_scr[cur] = jt_scr[...].T
    gl_scr[cur] = gt_scr[...].T


def _route(q, keys, n_heads, n_keys):
    n = q.shape[0]
    t = LANES
    n_tiles = n // t
    slots = n_heads * P_TOPK
    stride = t + SUBLANES
    kern = functools.partial(_route_kernel, n_heads=n_heads, n_keys=n_keys, stride=stride)
    return pl.pallas_call(
        kern,
        grid=(n_tiles + 1,),
        in_specs=[pl.BlockSpec((t, q.shape[1]), lambda s: (jnp.minimum(s, n_tiles - 1), 0)),
                  pl.BlockSpec(keys.shape, lambda s: (0, 0, 0))],
        out_specs=pl.BlockSpec((n_keys, t, n_keys), lambda s: (0, jnp.maximum(s - 1, 0), 0)),
        out_shape=jax.ShapeDtypeStruct((n_keys, n, n_keys), BF16),
        scratch_shapes=[pltpu.VMEM((2 * n_heads, P_TOPK, t), F32),
                        pltpu.VMEM((2 * n_heads, P_TOPK, t), jnp.int32),
                        pltpu.VMEM((slots, t), jnp.int32),
                        pltpu.VMEM((slots, t), jnp.int32),
                        pltpu.VMEM((slots, t), F32),
                        pltpu.VMEM((2, t, slots), jnp.int32),
                        pltpu.VMEM((2, t, slots), jnp.int32),
                        pltpu.VMEM((2, t, slots), F32),
                        pltpu.VMEM((n_keys * stride, n_keys), F32)],
        compiler_params=_params("arbitrary"),
        name="route",
    )(q, keys)


def _experts_kernel(h2_ref, u_ref, v_ref, m_ref, x1_ref, g2_ref, fg_ref, o_ref, acc_ref, *, final):
    j = pl.program_id(2)

    @pl.when(j == 0)
    def _():
        acc_ref[...] = jnp.zeros_like(acc_ref)

    s = lax.dot_general(h2_ref[0], u_ref[...].astype(BF16), NT, preferred_element_type=F32)
    act = 0.5 * s * (1.0 + lax.erf(s * (2.0 ** -0.5)))
    gate = jnp.concatenate([m_ref[ib] for ib in range(m_ref.shape[0])], axis=1)
    w = (gate.astype(F32) * act).astype(BF16)
    acc_ref[...] += jnp.dot(w, v_ref[...].astype(BF16), preferred_element_type=F32)

    @pl.when(j == pl.num_programs(2) - 1)
    def _():
        x2 = x1_ref[0] + g2_ref[0] * acc_ref[...]
        if final:
            x2 = x2 * lax.rsqrt(jnp.mean(x2 * x2, axis=-1, keepdims=True) + EPS) * fg_ref[...]
        o_ref[0] = x2


def _experts(h2, u, v, gmat, x1, g2, final_g, final, tm=1024, te=1024):
    bsz, s, d = x1.shape
    tm = min(tm, s)
    ne = u.shape[0]
    n_keys = gmat.shape[2]
    n_tiles = s // tm
    kern = functools.partial(_experts_kernel, final=final)
    return pl.pallas_call(
        kern,
        grid=(bsz, n_tiles, ne // te),
        in_specs=[pl.BlockSpec((1, tm, d), lambda b, i, j: (b, i, 0)),
                  pl.BlockSpec((te, d), lambda b, i, j: (j, 0)),
                  pl.BlockSpec((te, d), lambda b, i, j: (j, 0)),
                  pl.BlockSpec((te // n_keys, tm, n_keys), lambda b, i, j: (j, b * n_tiles + i, 0)),
                  pl.BlockSpec((1, tm, d), lambda b, i, j: (b, i, 0)),
                  pl.BlockSpec((1, 1, d), lambda b, i, j: (b, 0, 0)),
                  pl.BlockSpec((1, d), lambda b, i, j: (0, 0))],
        out_specs=pl.BlockSpec((1, tm, d), lambda b, i, j: (b, i, 0)),
        out_shape=jax.ShapeDtypeStruct((bsz, s, d), F32),
        scratch_shapes=[pltpu.VMEM((tm, d), F32)],
        compiler_params=_params("arbitrary", "arbitrary", "arbitrary"),
        name="experts",
    )(h2, u, v, gmat, x1, g2, final_g)


def kernel(x, c, ada_w, ada_b, norm1_g, w_in, conv_w, conv_b, mlstm_gate_b, mlstm_norm_g, lambda_q1,
           lambda_k1, lambda_q2, lambda_k2, diff_norm_g, w_out, norm2_g, peer_w_query, peer_sub_keys,
           peer_u, peer_v, final_g):
    bsz, s, d = x.shape
    depth = ada_w.shape[0]
    m_heads = mlstm_gate_b.shape[1] // 2
    m_width = mlstm_norm_g.shape[1]
    m_hd = m_width // m_heads
    dh = lambda_q1.shape[1]
    d_width = w_out.shape[1] - m_width
    d_heads = d_width // (2 * dh)
    p_heads, _, n_keys, _ = peer_sub_keys.shape[1:]
    n_m = 4 * m_width
    n_g = 2 * m_heads
    assert s % M_CHUNK == 0 and m_hd == LANES and 2 * dh == LANES and n_keys == LANES
    assert peer_sub_keys.shape[-1] == LANES and p_heads % 2 == 0 and n_keys == SUBLANES * P_TOPK

    for l in range(depth):
        mod = _adaln(c, ada_w[l], ada_b[l])
        sh1, sc1, g1, sh2, sc2, g2 = (a.reshape(bsz, 1, d) for a in jnp.split(mod, 6, axis=-1))

        w_m = w_in[l][:, :n_m].astype(BF16)
        w_g = jnp.pad(w_in[l][:, n_m:n_m + n_g].astype(BF16), ((0, 0), (0, LANES - n_g)))
        w_d = w_in[l][:, n_m + n_g:].astype(BF16)
        pm, pg, pd = _inproj(x, sh1, sc1, norm1_g[l].reshape(1, d), w_m, w_g, w_d)

        gates = pg[:, :, :n_g].transpose(0, 2, 1).reshape(bsz, n_g, s // M_CHUNK, M_CHUNK)
        hm = _mlstm(pm, gates, mlstm_gate_b[l], conv_w[l], conv_b[l].reshape(1, -1),
                    mlstm_norm_g[l].reshape(1, -1), m_heads, m_hd)

        lam_init = 0.8 - 0.6 * math.exp(-0.3 * l)
        lam_p = jnp.stack([lambda_q1[l], lambda_k1[l], lambda_q2[l], lambda_k2[l]])
        od = _diffattn(pd, lam_p, diff_norm_g[l].reshape(1, -1), d_heads, dh, lam_init)

        x1, h2, q = _outproj(hm, od, x, g1, sh2, sc2, norm2_g[l].reshape(1, d), w_out[l][:m_width].astype(BF16),
                             w_out[l][m_width:].astype(BF16), peer_w_query[l].astype(BF16))

        keys = peer_sub_keys[l].reshape(2 * p_heads, n_keys, -1).astype(BF16)
        gmat = _route(q.reshape(bsz * s, -1), keys, p_heads, n_keys)
        x = _experts(h2, peer_u[l], peer_v[l], gmat, x1, g2,
                     final_g.reshape(1, d), final=(l == depth - 1))
    return x
```

```python
import functools
import math

import jax
import jax.numpy as jnp
from jax import lax
from jax.experimental import pallas as pl
from jax.experimental.pallas import tpu as pltpu

F32 = jnp.float32
BF16 = jnp.bfloat16
EPS = 1e-6
LANES = 128
SUBLANES = 8
VMEM_LIMIT = 56 * 1024 * 1024
M_CHUNK = 128
CONV_WIDTH = 4
P_TOPK = 16

NT = (((1,), (1,)), ((), ()))
TN = (((0,), (0,)), ((), ()))


def _params(*sem):
    return pltpu.CompilerParams(dimension_semantics=sem, vmem_limit_bytes=VMEM_LIMIT)


def _adaln_kernel(c_ref, w_ref, b_ref, o_ref):
    c = c_ref[...]
    sc = (c * jax.nn.sigmoid(c)).astype(BF16)
    o_ref[...] = jnp.dot(sc, w_ref[...].astype(BF16), preferred_element_type=F32) + b_ref[...]


def _adaln(c, w, b):
    bsz, d = c.shape
    n = w.shape[1]
    tn = 1536
    return pl.pallas_call(
        _adaln_kernel,
        grid=(n // tn,),
        in_specs=[pl.BlockSpec((bsz, d), lambda j: (0, 0)),
                  pl.BlockSpec((d, tn), lambda j: (0, j)),
                  pl.BlockSpec((1, tn), lambda j: (0, j))],
        out_specs=pl.BlockSpec((bsz, tn), lambda j: (0, j)),
        out_shape=jax.ShapeDtypeStruct((bsz, n), F32),
        compiler_params=_params("arbitrary"),
        name="adaln",
    )(c, w, b.reshape(1, n))


def _inproj_kernel(x_ref, sh_ref, sc_ref, g_ref, wm_ref, wg_ref, wd_ref, pm_ref, pg_ref, pd_ref):
    x = x_ref[0]
    ms = jnp.mean(x * x, axis=-1, keepdims=True)
    h = x * lax.rsqrt(ms + EPS) * g_ref[...]
    h = h * (1.0 + sc_ref[0]) + sh_ref[0]
    hb = h.astype(BF16)
    pm_ref[0] = jnp.dot(hb, wm_ref[...], preferred_element_type=F32)
    pg_ref[0] = jnp.dot(hb, wg_ref[...], preferred_element_type=F32)
    pd_ref[0] = jnp.dot(hb, wd_ref[...], preferred_element_type=F32).astype(BF16)


def _inproj(x, sh, sc, g, wm, wg, wd, tm=1024):
    bsz, s, d = x.shape
    tm = min(tm, s)
    nm, ng, nd = wm.shape[1], wg.shape[1], wd.shape[1]
    tok = lambda b, i: (b, i, 0)
    per_b = lambda b, i: (b, 0, 0)
    const = lambda b, i: (0, 0)
    return pl.pallas_call(
        _inproj_kernel,
        grid=(bsz, s // tm),
        in_specs=[pl.BlockSpec((1, tm, d), tok),
                  pl.BlockSpec((1, 1, d), per_b),
                  pl.BlockSpec((1, 1, d), per_b),
                  pl.BlockSpec((1, d), const),
                  pl.BlockSpec((d, nm), const, pipeline_mode=pl.Buffered(1)),
                  pl.BlockSpec((d, ng), const, pipeline_mode=pl.Buffered(1)),
                  pl.BlockSpec((d, nd), const, pipeline_mode=pl.Buffered(1))],
        out_specs=[pl.BlockSpec((1, tm, nm), tok),
                   pl.BlockSpec((1, tm, ng), tok),
                   pl.BlockSpec((1, tm, nd), tok)],
        out_shape=[jax.ShapeDtypeStruct((bsz, s, nm), F32),
                   jax.ShapeDtypeStruct((bsz, s, ng), F32),
                   jax.ShapeDtypeStruct((bsz, s, nd), BF16)],
        compiler_params=_params("arbitrary", "arbitrary"),
        name="inproj",
    )(x, sh, sc, g, wm, wg, wd)


def _causal_conv(u, tail, w, b):
    ext = jnp.concatenate([tail, u], axis=0)
    y = u * w[CONV_WIDTH - 1:CONV_WIDTH, :]
    for k in range(CONV_WIDTH - 1):
        shift = CONV_WIDTH - 1 - k
        y = y + pltpu.roll(ext, shift, axis=0)[SUBLANES:, :] * w[k:k + 1, :]
    return y + b


def _cumsum_lanes(x):
    lane = lax.broadcasted_iota(jnp.int32, x.shape, 1)
    s = 1
    while s < x.shape[1]:
        x = x + jnp.where(lane >= s, pltpu.roll(x, s, axis=1), 0.0)
        s *= 2
    return x


def _mlstm_kernel(gb_ref, mq_ref, mk_ref, mv_ref, mo_ref, g_ref, cwq_ref, cwk_ref, cbq_ref, cbk_ref,
                  ng_ref, o_ref, b_scr, i_scr, c_scr, *, n_heads, head_dim, group):
    h0 = pl.program_id(1) * group
    L = M_CHUNK
    hd = head_dim
    nc = b_scr.shape[1]
    silu = lambda a: a * jax.nn.sigmoid(a)

    for u in range(group):
        i_scr[u] = g_ref[0, h0 + u] + gb_ref[h0 + u]
        fpre = g_ref[0, n_heads + h0 + u] + gb_ref[n_heads + h0 + u]
        lf = jnp.minimum(fpre, 0.0) - jnp.log1p(jnp.exp(-jnp.abs(fpre)))
        b_scr[u] = _cumsum_lanes(lf)

    r_i = lax.broadcasted_iota(jnp.int32, (L, L), 0)
    c_i = lax.broadcasted_iota(jnp.int32, (L, L), 1)
    eye = r_i == c_i
    causal = c_i <= r_i

    full = lambda col: jnp.broadcast_to(col, (L, LANES))

    def to_col(row):
        return full(jnp.sum(jnp.where(eye, row, 0.0), axis=1, keepdims=True))

    def chunk(c, carry):
        t0 = pl.multiple_of(c * L, L)
        return tuple(head_chunk(c, t0, u, carry[u]) for u in range(group))

    def head_chunk(c, t0, u, carry):
        n, m, q_tail, k_tail = carry
        cols = slice(u * hd, (u + 1) * hd)
        q_raw = mq_ref[0, pl.ds(t0, L), cols]
        k_raw = mk_ref[0, pl.ds(t0, L), cols]
        qc = silu(_causal_conv(q_raw, q_tail, cwq_ref[:, cols], cbq_ref[:, cols])).astype(BF16)
        kc = (silu(_causal_conv(k_raw, k_tail, cwk_ref[:, cols], cbk_ref[:, cols])) * (hd ** -0.5)).astype(BF16)
        vc = mv_ref[0, pl.ds(t0, L), cols].astype(BF16)
        b_row = b_scr[u, pl.ds(c, 1), :]
        i_row = i_scr[u, pl.ds(c, 1), :]
        b_col = to_col(b_row)
        i_col = to_col(i_row)

        logD = jnp.where(causal, b_col - b_row + i_row, -jnp.inf)
        m_t = jnp.maximum(b_col + m, full(jnp.max(logD, axis=1, keepdims=True)))
        Dw = jnp.exp(logD - m_t)
        inter = jnp.exp(b_col + m - m_t)
        sqk = lax.dot_general(qc, kc, NT, preferred_element_type=F32) * Dw
        num = (jnp.dot(sqk.astype(BF16), vc, preferred_element_type=F32)
               + inter * jnp.dot(qc, c_scr[u].astype(BF16), preferred_element_type=F32))
        den = (full(jnp.sum(sqk, axis=1, keepdims=True))
               + inter * full(jnp.sum(qc.astype(F32) * n, axis=1, keepdims=True)))
        hh = num / jnp.maximum(jnp.abs(den), jnp.exp(-m_t))

        bL = b_row[:, L - 1:L]
        m_new = jnp.maximum(bL + m, jnp.max(bL - b_row + i_row, axis=1, keepdims=True))
        w_col = jnp.exp(bL - b_col + i_col - m_new)
        decay = jnp.exp(bL + m - m_new)
        kw = kc.astype(F32) * w_col
        c_scr[u] = decay * c_scr[u] + lax.dot_general(kw.astype(BF16), vc, TN, preferred_element_type=F32)
        n_new = decay * n + jnp.sum(kw, axis=0, keepdims=True)

        y = hh * lax.rsqrt(full(jnp.mean(hh * hh, axis=-1, keepdims=True)) + EPS) * ng_ref[:, cols]
        y = y * jax.nn.sigmoid(mo_ref[0, pl.ds(t0, L), cols])
        o_ref[0, pl.ds(t0, L), cols] = y.astype(o_ref.dtype)
        return n_new, m_new, q_raw[L - SUBLANES:, :], k_raw[L - SUBLANES:, :]

    zeros = lambda *shape: jnp.zeros(shape, F32)
    c_scr[...] = zeros(*c_scr.shape)
    init = (zeros(1, hd), zeros(1, 1), zeros(SUBLANES, hd), zeros(SUBLANES, hd))
    lax.fori_loop(0, nc, chunk, (init,) * group)


def _mlstm(pm, gates, gate_b, conv_w, conv_b, norm_g, n_heads, head_dim, group=4):
    bsz, s, _ = pm.shape
    nc = s // M_CHUNK
    gw = group * head_dim
    n_groups = n_heads // group
    col = lambda off: (lambda b, g: (b, 0, off * n_groups + g))
    wcol = lambda off: (lambda b, g: (0, off * n_groups + g))
    kern = functools.partial(_mlstm_kernel, n_heads=n_heads, head_dim=head_dim, group=group)
    return pl.pallas_call(
        kern,
        grid=(bsz, n_groups),
        in_specs=[pl.BlockSpec(memory_space=pltpu.SMEM),
                  pl.BlockSpec((1, s, gw), col(0)),
                  pl.BlockSpec((1, s, gw), col(1)),
                  pl.BlockSpec((1, s, gw), col(2)),
                  pl.BlockSpec((1, s, gw), col(3)),
                  pl.BlockSpec((1, 2 * n_heads, nc, M_CHUNK), lambda b, g: (b, 0, 0, 0)),
                  pl.BlockSpec((CONV_WIDTH, gw), wcol(0)),
                  pl.BlockSpec((CONV_WIDTH, gw), wcol(1)),
                  pl.BlockSpec((1, gw), wcol(0)),
                  pl.BlockSpec((1, gw), wcol(1)),
                  pl.BlockSpec((1, gw), wcol(0))],
        out_specs=pl.BlockSpec((1, s, gw), col(0)),
        out_shape=jax.ShapeDtypeStruct((bsz, s, n_heads * head_dim), BF16),
        scratch_shapes=[pltpu.VMEM((group, nc, M_CHUNK), F32), pltpu.VMEM((group, nc, M_CHUNK), F32),
                        pltpu.VMEM((group, head_dim, head_dim), F32)],
        compiler_params=_params("arbitrary", "arbitrary"),
        name="mlstm",
    )(gate_b, pm, pm, pm, pm, gates, conv_w, conv_w, conv_b, conv_b, norm_g)


def _diffattn_kernel(q_ref, k_ref, v_ref, lam_ref, ng_ref, o_ref, m_scr, l_scr, acc_scr,
                     *, dh, lam_init, group):
    i = pl.program_id(2)
    tq = q_ref.shape[1]
    hw = 2 * dh
    scale = dh ** -0.5
    exact_scale = math.frexp(scale)[0] == 0.5
    lane = lax.broadcasted_iota(jnp.int32, (tq, hw), 1)
    qs = []
    for u in range(group):
        q = q_ref[0, :, u * hw:(u + 1) * hw]
        if exact_scale:
            q = q * scale
        zero = jnp.zeros_like(q)
        qs += [jnp.where(lane < dh, q, zero), jnp.where(lane < dh, zero, q)]
    lp = lam_ref[...]
    lam = (jnp.exp(jnp.sum(lp[0:1] * lp[1:2], axis=1, keepdims=True))
           - jnp.exp(jnp.sum(lp[2:3] * lp[3:4], axis=1, keepdims=True)) + lam_init)
    on_or_below_diag = (lax.broadcasted_iota(jnp.int32, (tq, tq), 1)
                        <= lax.broadcasted_iota(jnp.int32, (tq, tq), 0))

    def block(k0, width, rows, mask, first):
        n_rows = rows.stop - rows.start
        scores = []
        for st in range(2 * group):
            cols = slice((st // 2) * hw, (st // 2 + 1) * hw)
            kb = k_ref[0, pl.ds(k0, width), cols]
            scores.append(lax.dot_general(qs[st][rows, :], kb, NT, preferred_element_type=F32))
        for st in range(2 * group):
            cols = slice((st // 2) * hw, (st // 2 + 1) * hw)
            vb = v_ref[0, pl.ds(k0, width), cols]
            s = scores[st]
            if not exact_scale:
                s = s * scale
            if mask is not None:
                s = jnp.where(mask, s, -jnp.inf)
            m_new = jnp.broadcast_to(jnp.max(s, axis=1, keepdims=True), (n_rows, LANES))
            if not first:
                m_old = m_scr[st, rows, :]
                m_new = jnp.maximum(m_old, m_new)
                alpha = jnp.exp(m_old - m_new)
            e = [jnp.exp(s[:, c:c + LANES] - m_new) for c in range(0, width, LANES)]
            part = functools.reduce(jnp.add, e)
            pv = jnp.dot(jnp.concatenate([x.astype(BF16) for x in e], axis=1), vb, preferred_element_type=F32)
            l_scr[st, rows, :] = part if first else alpha * l_scr[st, rows, :] + part
            acc_scr[st, rows, :] = pv if first else alpha * acc_scr[st, rows, :] + pv
            m_scr[st, rows, :] = m_new

    def below(j, carry):
        block(pl.multiple_of(j * 2 * tq, 2 * tq), 2 * tq, slice(0, tq), None, False)
        return carry

    half = tq // 2
    d0 = pl.multiple_of(i * tq, tq)
    block(d0, half, slice(0, tq), on_or_below_diag[:, :half], True)
    block(d0 + half, half, slice(half, tq), on_or_below_diag[half:, half:], False)
    lax.fori_loop(0, i // 2, below, 0)

    @pl.when(i % 2 == 1)
    def _():
        block(pl.multiple_of((i - 1) * tq, tq), tq, slice(0, tq), None, False)

    for u in range(group):
        o1 = acc_scr[2 * u] / jnp.sum(l_scr[2 * u], axis=1, keepdims=True)
        o2 = acc_scr[2 * u + 1] / jnp.sum(l_scr[2 * u + 1], axis=1, keepdims=True)
        o = o1 - lam * o2
        y = o * lax.rsqrt(jnp.mean(o * o, axis=-1, keepdims=True) + EPS) * ng_ref[...] * (1.0 - lam_init)
        o_ref[0, :, u * hw:(u + 1) * hw] = y.astype(o_ref.dtype)


def _diffattn(pd, lam_p, norm_g, n_heads, dh, lam_init, tq=512, group=4):
    bsz, s, _ = pd.shape
    tq = min(tq, s)
    hw = 2 * dh
    gw = group * hw
    n_groups = n_heads // group
    kern = functools.partial(_diffattn_kernel, dh=dh, lam_init=lam_init, group=group)
    return pl.pallas_call(
        kern,
        grid=(bsz, n_groups, s // tq),
        in_specs=[pl.BlockSpec((1, tq, gw), lambda b, g, i: (b, i, g)),
                  pl.BlockSpec((1, s, gw), lambda b, g, i: (b, 0, n_groups + g)),
                  pl.BlockSpec((1, s, gw), lambda b, g, i: (b, 0, 2 * n_groups + g)),
                  pl.BlockSpec((4, dh), lambda b, g, i: (0, 0)),
                  pl.BlockSpec((1, hw), lambda b, g, i: (0, 0))],
        out_specs=pl.BlockSpec((1, tq, gw), lambda b, g, i: (b, i, g)),
        out_shape=jax.ShapeDtypeStruct((bsz, s, n_heads * hw), BF16),
        scratch_shapes=[pltpu.VMEM((2 * group, tq, LANES), F32), pltpu.VMEM((2 * group, tq, LANES), F32),
                        pltpu.VMEM((2 * group, tq, hw), F32)],
        compiler_params=_params("arbitrary", "arbitrary", "arbitrary"),
        name="diffattn",
    )(pd, pd, pd, lam_p, norm_g)


def _outproj_kernel(hm_ref, od_ref, x_ref, g1_ref, sh_ref, sc_ref, ng_ref, wm_ref, wd_ref, wq_ref,
                    x1_ref, h2_ref, q_ref):
    y = (jnp.dot(hm_ref[0], wm_ref[...], preferred_element_type=F32)
         + jnp.dot(od_ref[0], wd_ref[...], preferred_element_type=F32))
    x1 = x_ref[0] + g1_ref[0] * y
    x1_ref[0] = x1
    ms = jnp.mean(x1 * x1, axis=-1, keepdims=True)
    h2 = x1 * lax.rsqrt(ms + EPS) * ng_ref[...]
    h2 = (h2 * (1.0 + sc_ref[0]) + sh_ref[0]).astype(BF16)
    h2_ref[0] = h2
    q_ref[0] = jnp.dot(h2, wq_ref[...], preferred_element_type=F32).astype(BF16)


def _outproj(hm, od, x, g1, sh2, sc2, ng, wm, wd, wq, tm=1024):
    bsz, s, d = x.shape
    tm = min(tm, s)
    wm_w, wd_w, nq = hm.shape[2], od.shape[2], wq.shape[1]
    tok = lambda b, i: (b, i, 0)
    per_b = lambda b, i: (b, 0, 0)
    const = lambda b, i: (0, 0)
    return pl.pallas_call(
        _outproj_kernel,
        grid=(bsz, s // tm),
        in_specs=[pl.BlockSpec((1, tm, wm_w), tok),
                  pl.BlockSpec((1, tm, wd_w), tok),
                  pl.BlockSpec((1, tm, d), tok),
                  pl.BlockSpec((1, 1, d), per_b),
                  pl.BlockSpec((1, 1, d), per_b),
                  pl.BlockSpec((1, 1, d), per_b),
                  pl.BlockSpec((1, d), const),
                  pl.BlockSpec((wm_w, d), const, pipeline_mode=pl.Buffered(1)),
                  pl.BlockSpec((wd_w, d), const, pipeline_mode=pl.Buffered(1)),
                  pl.BlockSpec((d, nq), const, pipeline_mode=pl.Buffered(1))],
        out_specs=[pl.BlockSpec((1, tm, d), tok),
                   pl.BlockSpec((1, tm, d), tok),
                   pl.BlockSpec((1, tm, nq), tok)],
        out_shape=[jax.ShapeDtypeStruct((bsz, s, d), F32),
                   jax.ShapeDtypeStruct((bsz, s, d), BF16),
                   jax.ShapeDtypeStruct((bsz, s, nq), BF16)],
        compiler_params=_params("arbitrary", "arbitrary"),
        name="outproj",
    )(hm, od, x, g1, sh2, sc2, ng, wm, wd, wq)


def _oddeven_merge_sort(n):
    def merge(lo, hi, r):
        step = 2 * r
        if step < hi - lo:
            yield from merge(lo, hi, step)
            yield from merge(lo + r, hi, step)
            yield from ((i, i + r) for i in range(lo + r, hi - r, step))
        else:
            yield (lo, lo + r)

    def sort(lo, hi):
        if hi > lo:
            mid = lo + (hi - lo) // 2
            yield from sort(lo, mid)
            yield from sort(mid + 1, hi)
            yield from merge(lo, hi, 1)

    return list(sort(0, n - 1))


def _topk_rows_sorted(s, k, payload=None):
    n_rows, t = s.shape
    nv = n_rows // SUBLANES
    assert nv & (nv - 1) == 0
    slabs = lambda a: [a[SUBLANES * v:SUBLANES * (v + 1), :] for v in range(nv)]
    sub = lax.broadcasted_iota(jnp.int32, (SUBLANES, t), 0)
    out_row = lax.broadcasted_iota(jnp.int32, (k, t), 0)
    val = slabs(s)
    pay = None if payload is None else slabs(payload)
    sid = list(range(nv))
    for i, j in _oddeven_merge_sort(nv):
        va, vb, ia, ib = val[i], val[j], sid[i], sid[j]
        if isinstance(ia, int) and isinstance(ib, int):
            first = va >= vb if ia < ib else va > vb
        else:
            first = (va > vb) | ((va == vb) & (ia < ib))
        val[i], val[j] = jnp.maximum(va, vb), jnp.minimum(va, vb)
        sid[i], sid[j] = jnp.where(first, ia, ib), jnp.where(first, ib, ia)
        if pay is not None:
            pay[i], pay[j] = jnp.where(first, pay[i], pay[j]), jnp.where(first, pay[j], pay[i])
    row = [x * SUBLANES + sub for x in sid]
    vals = jnp.zeros((k, t), s.dtype)
    rows = jnp.zeros((k, t), jnp.int32)
    pays = None if pay is None else jnp.zeros((k, t), payload.dtype)
    for r in range(k):
        m = jnp.max(val[0], axis=0, keepdims=True)
        idx = jnp.min(jnp.where(val[0] == m, row[0], n_rows), axis=0, keepdims=True)
        hit = row[0] == idx
        vals = jnp.where(out_row == r, m, vals)
        rows = jnp.where(out_row == r, idx, rows)
        if pay is not None:
            pays = jnp.where(out_row == r, jnp.sum(jnp.where(hit, pay[0], 0), axis=0, keepdims=True), pays)
        live = min(nv, k - 1 - r)
        for q in range(min(live, nv - 1)):
            val[q] = jnp.where(hit, val[q + 1], val[q])
            row[q] = jnp.where(hit, row[q + 1], row[q])
            if pay is not None:
                pay[q] = jnp.where(hit, pay[q + 1], pay[q])
        if live == nv:
            val[nv - 1] = jnp.where(hit, -jnp.inf, val[nv - 1])
    return vals, rows, pays


def _candidate_layout(k):
    slabs, cur = [], []

    def flush():
        n_valid = len(cur)
        while len(cur) < SUBLANES:
            r = len(cur)
            period = 1
            while period < max(b for _, b in cur[:n_valid]) + 1:
                period *= 2
            b = cur[r - period][1] if r >= period else r
            cur.append((cur[n_valid - 1][0], b))
        slabs.append(([a for a, _ in cur], [b for _, b in cur], n_valid))
        cur.clear()

    for a in range(k):
        nb = k // (a + 1)
        for b0 in range(0, nb, SUBLANES):
            group = [(a, b) for b in range(b0, min(nb, b0 + SUBLANES))]
            if cur and (len(cur) + len(group) > SUBLANES or (nb == 1 and a % SUBLANES == 0)):
                flush()
            cur.extend(group)
    if cur:
        flush()
    return slabs


def _rows_by_pattern(x, pattern, sub):
    p0 = pattern[0]
    if p0 % SUBLANES == 0 and pattern == list(range(p0, p0 + SUBLANES)):
        return x[p0:p0 + SUBLANES, :]
    period = SUBLANES
    while period > 1 and all(pattern[r] == pattern[r % (period // 2)] for r in range(SUBLANES)):
        period //= 2
    pos = sub if period == SUBLANES else sub & (period - 1)
    out = x[p0:p0 + 1, :]
    for r in range(1, period):
        if pattern[r] != pattern[r - 1]:
            out = jnp.where(pos >= r, x[pattern[r]:pattern[r] + 1, :], out)
    return jnp.broadcast_to(out, (SUBLANES, x.shape[1]))


def _route_kernel(q_ref, keys_ref, gm_ref, sv_scr, si_scr, it_scr, jt_scr, gt_scr, il_scr, jl_scr, gl_scr,
                  w_scr, *, n_heads, n_keys, stride):
    K = P_TOPK
    t = q_ref.shape[0]
    slots = n_heads * K
    step = pl.program_id(0)
    cur = step % 2
    prev = 1 - cur
    sub = lax.broadcasted_iota(jnp.int32, (SUBLANES, t), 0)
    key_row = lax.broadcasted_iota(jnp.int32, (n_keys, slots), 0)
    layout = _candidate_layout(K)
    ways2 = 4
    n_loop = n_heads // ways2
    blk_per_iter = n_keys // n_loop

    @pl.when(step == 0)
    def _():
        il_scr[1] = jnp.zeros((t, slots), jnp.int32)
        jl_scr[1] = jnp.zeros((t, slots), jnp.int32)
        gl_scr[1] = jnp.zeros((t, slots), F32)

    def gate_rows(tok):
        i_row = il_scr[prev, pl.ds(tok, 1), :]
        j_row = jl_scr[prev, pl.ds(tok, 1), :]
        g_row = gl_scr[prev, pl.ds(tok, 1), :]
        a_t = jnp.where(key_row == i_row, 1.0, 0.0).astype(BF16)
        b_t = jnp.where(key_row == j_row, g_row, 0.0).astype(BF16)
        m_t = lax.dot_general(a_t, b_t, NT, preferred_element_type=F32)
        w_scr[pl.ds(tok, n_keys, stride=stride), :] = m_t

    ways = 4
    n_loop1 = 2 * n_heads // ways
    tok_per_iter = t // n_loop1

    def stage1(hh, _):
        for p in range(ways):
            hp = ways * hh + p
            c0 = pl.multiple_of(hp * LANES, LANES)
            qb = q_ref[:, pl.ds(c0, LANES)]
            st = lax.dot_general(keys_ref[hp], qb, NT, preferred_element_type=F32)
            v, ix, _ = _topk_rows_sorted(st, K)
            sv_scr[hp] = v
            si_scr[hp] = ix
        t0 = pl.multiple_of(hh * tok_per_iter, tok_per_iter)
        for u in range(tok_per_iter):
            gate_rows(t0 + u)
        return 0

    lax.fori_loop(0, n_loop1, stage1, 0, unroll=True)

    def stage2(hh, _):
        b0 = hh * blk_per_iter
        for u in range(blk_per_iter):
            r0 = pl.multiple_of((b0 + u) * stride, SUBLANES)
            gm_ref[b0 + u] = w_scr[pl.ds(r0, t), :].astype(gm_ref.dtype)
        for u in range(ways2):
            h = ways2 * hh + u
            sv0, sv1 = sv_scr[2 * h], sv_scr[2 * h + 1]
            si0, si1 = si_scr[2 * h], si_scr[2 * h + 1]
            cand, cidx = [], []
            for a_pat, b_pat, n_valid in layout:
                c = _rows_by_pattern(sv0, a_pat, sub) + _rows_by_pattern(sv1, b_pat, sub)
                if n_valid < SUBLANES:
                    c = jnp.where(sub < n_valid, c, -jnp.inf)
                cand.append(c)
                cidx.append(_rows_by_pattern(si0, a_pat, sub) * n_keys + _rows_by_pattern(si1, b_pat, sub))
            while len(cand) & (len(cand) - 1):
                cand.append(jnp.full((SUBLANES, t), -jnp.inf, F32))
                cidx.append(jnp.zeros((SUBLANES, t), jnp.int32))
            fv, _, eidx = _topk_rows_sorted(jnp.concatenate(cand, axis=0), K, jnp.concatenate(cidx, axis=0))
            e = jnp.exp(fv - fv[0:1, :])
            g = e / jnp.sum(e, axis=0, keepdims=True)
            r0 = pl.multiple_of(h * K, K)
            it_scr[pl.ds(r0, K), :] = eidx // n_keys
            jt_scr[pl.ds(r0, K), :] = eidx % n_keys
            gt_scr[pl.ds(r0, K), :] = g
        return 0

    lax.fori_loop(0, n_loop, stage2, 0, unroll=True)
    il_scr[cur] = it_scr[...].T
    jl_scr[cur] = jt_scr[...].T
    gl_scr[cur] = gt_scr[...].T


def _route(q, keys, n_heads, n_keys):
    n = q.shape[0]
    t = LANES
    n_tiles = n // t
    slots = n_heads * P_TOPK
    stride = t + SUBLANES
    kern = functools.partial(_route_kernel, n_heads=n_heads, n_keys=n_keys, stride=stride)
    return pl.pallas_call(
        kern,
        grid=(n_tiles + 1,),
        in_specs=[pl.BlockSpec((t, q.shape[1]), lambda s: (jnp.minimum(s, n_tiles - 1), 0)),
                  pl.BlockSpec(keys.shape, lambda s: (0, 0, 0))],
        out_specs=pl.BlockSpec((n_keys, t, n_keys), lambda s: (0, jnp.maximum(s - 1, 0), 0)),
        out_shape=jax.ShapeDtypeStruct((n_keys, n, n_keys), BF16),
        scratch_shapes=[pltpu.VMEM((2 * n_heads, P_TOPK, t), F32),
                        pltpu.VMEM((2 * n_heads, P_TOPK, t), jnp.int32),
                        pltpu.VMEM((slots, t), jnp.int32),
                        pltpu.VMEM((slots, t), jnp.int32),
                        pltpu.VMEM((slots, t), F32),
                        pltpu.VMEM((2, t, slots), jnp.int32),
                        pltpu.VMEM((2, t, slots), jnp.int32),
                        pltpu.VMEM((2, t, slots), F32),
                        pltpu.VMEM((n_keys * stride, n_keys), F32)],
        compiler_params=_params("arbitrary"),
        name="route",
    )(q, keys)


def _experts_kernel(h2_ref, u_ref, v_ref, m_ref, x1_ref, g2_ref, fg_ref, o_ref, acc_ref, *, final):
    j = pl.program_id(2)

    @pl.when(j == 0)
    def _():
        acc_ref[...] = jnp.zeros_like(acc_ref)

    s = lax.dot_general(h2_ref[0], u_ref[...].astype(BF16), NT, preferred_element_type=F32)
    act = 0.5 * s * (1.0 + lax.erf(s * (2.0 ** -0.5)))
    gate = jnp.concatenate([m_ref[ib] for ib in range(m_ref.shape[0])], axis=1)
    w = (gate.astype(F32) * act).astype(BF16)
    acc_ref[...] += jnp.dot(w, v_ref[...].astype(BF16), preferred_element_type=F32)

    @pl.when(j == pl.num_programs(2) - 1)
    def _():
        x2 = x1_ref[0] + g2_ref[0] * acc_ref[...]
        if final:
            x2 = x2 * lax.rsqrt(jnp.mean(x2 * x2, axis=-1, keepdims=True) + EPS) * fg_ref[...]
        o_ref[0] = x2


def _experts(h2, u, v, gmat, x1, g2, final_g, final, tm=1024, te=1024):
    bsz, s, d = x1.shape
    tm = min(tm, s)
    ne = u.shape[0]
    n_keys = gmat.shape[2]
    n_tiles = s // tm
    kern = functools.partial(_experts_kernel, final=final)
    return pl.pallas_call(
        kern,
        grid=(bsz, n_tiles, ne // te),
        in_specs=[pl.BlockSpec((1, tm, d), lambda b, i, j: (b, i, 0)),
                  pl.BlockSpec((te, d), lambda b, i, j: (j, 0)),
                  pl.BlockSpec((te, d), lambda b, i, j: (j, 0)),
                  pl.BlockSpec((te // n_keys, tm, n_keys), lambda b, i, j: (j, b * n_tiles + i, 0)),
                  pl.BlockSpec((1, tm, d), lambda b, i, j: (b, i, 0)),
                  pl.BlockSpec((1, 1, d), lambda b, i, j: (b, 0, 0)),
                  pl.BlockSpec((1, d), lambda b, i, j: (0, 0))],
        out_specs=pl.BlockSpec((1, tm, d), lambda b, i, j: (b, i, 0)),
        out_shape=jax.ShapeDtypeStruct((bsz, s, d), F32),
        scratch_shapes=[pltpu.VMEM((tm, d), F32)],
        compiler_params=_params("arbitrary", "arbitrary", "arbitrary"),
        name="experts",
    )(h2, u, v, gmat, x1, g2, final_g)


def kernel(x, c, ada_w, ada_b, norm1_g, w_in, conv_w, conv_b, mlstm_gate_b, mlstm_norm_g, lambda_q1,
           lambda_k1, lambda_q2, lambda_k2, diff_norm_g, w_out, norm2_g, peer_w_query, peer_sub_keys,
           peer_u, peer_v, final_g):
    bsz, s, d = x.shape
    depth = ada_w.shape[0]
    m_heads = mlstm_gate_b.shape[1] // 2
    m_width = mlstm_norm_g.shape[1]
    m_hd = m_width // m_heads
    dh = lambda_q1.shape[1]
    d_width = w_out.shape[1] - m_width
    d_heads = d_width // (2 * dh)
    p_heads, _, n_keys, _ = peer_sub_keys.shape[1:]
    n_m = 4 * m_width
    n_g = 2 * m_heads
    assert s % M_CHUNK == 0 and m_hd == LANES and 2 * dh == LANES and n_keys == LANES
    assert peer_sub_keys.shape[-1] == LANES and p_heads % 2 == 0 and n_keys == SUBLANES * P_TOPK

    for l in range(depth):
        mod = _adaln(c, ada_w[l], ada_b[l])
        sh1, sc1, g1, sh2, sc2, g2 = (a.reshape(bsz, 1, d) for a in jnp.split(mod, 6, axis=-1))

        w_m = w_in[l][:, :n_m].astype(BF16)
        w_g = jnp.pad(w_in[l][:, n_m:n_m + n_g].astype(BF16), ((0, 0), (0, LANES - n_g)))
        w_d = w_in[l][:, n_m + n_g:].astype(BF16)
        pm, pg, pd = _inproj(x, sh1, sc1, norm1_g[l].reshape(1, d), w_m, w_g, w_d)

        gates = pg[:, :, :n_g].transpose(0, 2, 1).reshape(bsz, n_g, s // M_CHUNK, M_CHUNK)
        hm = _mlstm(pm, gates, mlstm_gate_b[l], conv_w[l], conv_b[l].reshape(1, -1),
                    mlstm_norm_g[l].reshape(1, -1), m_heads, m_hd)

        lam_init = 0.8 - 0.6 * math.exp(-0.3 * l)
        lam_p = jnp.stack([lambda_q1[l], lambda_k1[l], lambda_q2[l], lambda_k2[l]])
        od = _diffattn(pd, lam_p, diff_norm_g[l].reshape(1, -1), d_heads, dh, lam_init)

        x1, h2, q = _outproj(hm, od, x, g1, sh2, sc2, norm2_g[l].reshape(1, d), w_out[l][:m_width].astype(BF16),
                             w_out[l][m_width:].astype(BF16), peer_w_query[l].astype(BF16))

        keys = peer_sub_keys[l].reshape(2 * p_heads, n_keys, -1).astype(BF16)
        gmat = _route(q.reshape(bsz * s, -1), keys, p_heads, n_keys)
        x = _experts(h2, peer_u[l], peer_v[l], gmat, x1, g2,
                     final_g.reshape(1, d), final=(l == depth - 1))
    return x
```
